```python
import functools
import jax
import jax.numpy as jnp
from jax import lax
import numpy as np

D_MODEL = 2048
BATCH = 4
SEQ = 2048
DEPTH = 2
DEC_BATCH = 8
DEC_SEQ = 1
PAST_LEN = 16384
PAGE_SIZE = 128

N_EVEN = (DEPTH + 1) // 2
N_ODD = DEPTH // 2
HALF = D_MODEL // 2

MLSTM_HEADS = 4
MLSTM_DH = HALF // MLSTM_HEADS
MLSTM_CHUNK = 64

ATT_HEADS = 8
ATT_DH = HALF // ATT_HEADS
ATT_KV_HEADS = 2
IDX_HEADS = 8
IDX_DIM = 64
TOPK_MAX = 256
Q_BLOCK = 128

HGRN_HEADS = 8
HGRN_DK = 128
HGRN_DV = HALF // HGRN_HEADS
HGRN_CHUNK = 64

RET_HEADS = 4
RET_DK = HALF // RET_HEADS
RET_DV = HALF // RET_HEADS
RET_CHUNK = 64
RET_THETA = 10000.0

D_FF = 5632
P_DIM = 256
ALPHA = (2 * DEPTH) ** 0.25
BETA = (8 * DEPTH) ** -0.25
LN_EPS = 1e-5

EVEN_SPLITS = (HALF, HALF, HALF, HALF, MLSTM_HEADS, MLSTM_HEADS,
               ATT_HEADS * ATT_DH, ATT_KV_HEADS * ATT_DH, ATT_KV_HEADS * ATT_DH,
               IDX_HEADS * IDX_DIM, IDX_DIM, IDX_HEADS)
EVEN_WIDTH = sum(EVEN_SPLITS)
ODD_SPLITS = (HGRN_HEADS * HGRN_DK, HGRN_HEADS * HGRN_DK, HGRN_HEADS * HGRN_DV, HGRN_HEADS * HGRN_DV,
              RET_HEADS * RET_DK, RET_HEADS * RET_DK, RET_HEADS * RET_DV, RET_HEADS * RET_DV)
ODD_WIDTH = sum(ODD_SPLITS)

kernel_name = 'hybrid_mlstm_dsa_hgrn2_retention_step'


def split_cols(z, sizes):
    return jnp.split(z, np.cumsum(sizes)[:-1].tolist(), axis=-1)


def heads_first(a, n_heads):
    B, L, _ = a.shape
    return a.reshape(B, L, n_heads, -1).transpose(0, 2, 1, 3)


def heads_last(a):
    B, H, L, d = a.shape
    return a.transpose(0, 2, 1, 3).reshape(B, L, H * d)


def layer_norm(x, g, b):
    xf = x.astype(jnp.float32)
    mu = jnp.mean(xf, -1, keepdims=True)
    var = jnp.mean(jnp.square(xf - mu), -1, keepdims=True)
    return ((xf - mu) * lax.rsqrt(var + LN_EPS) * g.astype(jnp.float32) + b.astype(jnp.float32)).astype(x.dtype)


def head_rms_norm(h, gain):
    hf = h.astype(jnp.float32)
    hf = hf * lax.rsqrt(jnp.mean(hf * hf, -1, keepdims=True) + LN_EPS)
    return heads_last(hf) * gain.astype(jnp.float32)


def swiglu(x, w_up, w_down):
    g, u = jnp.split(x @ w_up, 2, axis=-1)
    return (jax.nn.silu(g) * u) @ w_down


def rotate(x, pos):
    d = x.shape[-1]
    inv = 1.0 / (RET_THETA ** jnp.linspace(0.0, 1.0, d // 2, dtype=jnp.float32))
    ang = pos.astype(jnp.float32)[:, None] * inv[None, :]
    cos, sin = jnp.cos(ang), jnp.sin(ang)
    x1, x2 = x[..., :d // 2], x[..., d // 2:]
    return jnp.concatenate([x1 * cos - x2 * sin, x1 * sin + x2 * cos], -1)


def run_chunks(fn, state, xs, chunk, chunked):
    if not chunked:
        return fn(state, *xs)
    B, H, S = xs[0].shape[:3]
    n = S // chunk

    def to_chunks(a):
        return jnp.moveaxis(a.reshape((B, H, n, chunk) + a.shape[3:]), 2, 0)

    state, out = lax.scan(lambda st, xc: fn(st, *xc), state, tuple(to_chunks(a) for a in xs))
    out = jnp.moveaxis(out, 0, 2)
    return state, out.reshape((B, H, S) + out.shape[4:])


def mlstm_chunk(state, q, k, v, ig, lf):
    C, n, m = state
    L = q.shape[2]
    b = jnp.cumsum(lf, axis=-1)
    causal = jnp.tril(jnp.ones((L, L), dtype=bool))
    dlog = jnp.where(causal, b[..., :, None] - b[..., None, :] + ig[..., None, :], -jnp.inf)
    g = b + m[..., None]
    m_t = jnp.maximum(g, jnp.max(dlog, -1))
    dw = jnp.exp(dlog - m_t[..., None])
    gw = jnp.exp(g - m_t)
    s = jnp.einsum('bhtd,bhsd->bhts', q, k) * dw
    num = jnp.einsum('bhts,bhsv->bhtv', s, v) + gw[..., None] * jnp.einsum('bhtd,bhdv->bhtv', q, C)
    den = jnp.sum(s, -1) + gw * jnp.einsum('bhtd,bhd->bht', q, n)
    h = num / jnp.maximum(jnp.abs(den), jnp.exp(-m_t))[..., None]
    bL = b[..., -1]
    wlog = bL[..., None] - b + ig
    m_new = jnp.maximum(bL + m, jnp.max(wlog, -1))
    w = jnp.exp(wlog - m_new[..., None])
    decay = jnp.exp(bL + m - m_new)
    C_new = decay[..., None, None] * C + jnp.einsum('bhsd,bhsv->bhdv', k * w[..., None], v)
    n_new = decay[..., None] * n + jnp.einsum('bhs,bhsd->bhd', w, k)
    return (C_new, n_new, m_new), h


def hgrn_chunk(S, q, k, v, lf):
    L = q.shape[2]
    b = jnp.cumsum(lf, axis=2)
    causal = jnp.tril(jnp.ones((L, L), dtype=bool))[:, :, None]
    dlog = jnp.where(causal, b[:, :, :, None, :] - b[:, :, None, :, :], -jnp.inf)
    attn = jnp.einsum('bhtd,bhsd,bhtsd->bhts', q, k, jnp.exp(dlog))
    o = jnp.einsum('bhts,bhsv->bhtv', attn, v) + jnp.einsum('bhtd,bhdv->bhtv', q * jnp.exp(b), S)
    bL = b[:, :, -1:, :]
    S_new = jnp.exp(bL[:, :, 0, :, None]) * S + jnp.einsum('bhsd,bhsv->bhdv', k * jnp.exp(bL - b), v)
    return S_new, o


def retention_chunk(S, q, k, v):
    H, L = q.shape[1], q.shape[2]
    log_g = jnp.log1p(-jnp.exp2(-5.0 - jnp.arange(H, dtype=jnp.float32)))
    pos = jnp.arange(L, dtype=jnp.float32)
    rel = pos[:, None] - pos[None, :]
    dmask = jnp.where(rel >= 0, jnp.exp(log_g[:, None, None] * jnp.maximum(rel, 0.0)), 0.0)
    scores = jnp.einsum('bhtd,bhsd->bhts', q, k) * dmask
    inter = jnp.exp(log_g[:, None] * (pos + 1.0))
    o = jnp.einsum('bhts,bhsv->bhtv', scores, v) + jnp.einsum('bhtd,bhdv->bhtv', q, S) * inter[None, :, :, None]
    tail = jnp.exp(log_g[:, None] * (L - 1.0 - pos))
    S_new = jnp.exp(log_g * L)[None, :, None, None] * S + jnp.einsum('bhsd,bhsv->bhdv', k * tail[None, :, :, None], v)
    return S_new, o


def indexer_scores(qi, ki, wi):
    rel = jax.nn.relu(jnp.einsum('bthd,bsd->bths', qi, ki) * IDX_DIM ** -0.5)
    return jnp.einsum('bths,bth->bts', rel, wi * IDX_HEADS ** -0.5)


def sparse_attend(q, kg, vg, valid):
    B, T = q.shape[:2]
    qg = q.reshape(B, T, ATT_KV_HEADS, ATT_HEADS // ATT_KV_HEADS, ATT_DH)
    logits = jnp.einsum('btkgd,btjkd->btkgj', qg, kg) * ATT_DH ** -0.5
    logits = jnp.where(valid[:, :, None, None, :], logits, -jnp.inf)
    p = jax.nn.softmax(logits.astype(jnp.float32), axis=-1)
    out = jnp.einsum('btkgj,btjkd->btkgd', p, vg)
    return out.reshape(B, T, ATT_HEADS * ATT_DH)


def dsa_prompt(q, k, v, qi, ki, wi):
    B, S = q.shape[:2]
    n_sel = min(TOPK_MAX, S // 4)
    nb = S // Q_BLOCK
    key_pos = jnp.arange(S)

    def blocks(a):
        return jnp.moveaxis(a.reshape((B, nb, Q_BLOCK) + a.shape[2:]), 1, 0)

    def one_block(args):
        qb, qib, wib, t0 = args
        qpos = t0 + jnp.arange(Q_BLOCK)
        score = indexer_scores(qib, ki, wib)
        score = jnp.where(key_pos[None, None, :] <= qpos[None, :, None], score, -jnp.inf)
        _, idx = lax.top_k(score, n_sel)
        valid = idx <= qpos[None, :, None]
        kg = jax.vmap(lambda kk, ii: kk[ii])(k, idx)
        vg = jax.vmap(lambda vv, ii: vv[ii])(v, idx)
        return sparse_attend(qb, kg, vg, valid)

    out = lax.map(one_block, (blocks(q), blocks(qi), blocks(wi), jnp.arange(nb) * Q_BLOCK))
    return jnp.moveaxis(out, 0, 1).reshape(B, S, ATT_HEADS * ATT_DH)


def dsa_sample(q, k, v, qi, ki, wi, cache_k, cache_v, cache_ik, page_table):
    B, T = q.shape[:2]
    past = page_table.shape[1] * PAGE_SIZE
    L = past + T
    n_sel = min(TOPK_MAX, L // 4)
    ki_past = cache_ik[page_table].reshape(B, past, IDX_DIM).astype(jnp.float32)
    ki_all = jnp.concatenate([ki_past, ki], 1)
    qpos = past + jnp.arange(T)
    score = indexer_scores(qi, ki_all, wi)
    score = jnp.where(jnp.arange(L)[None, None, :] <= qpos[None, :, None], score, -jnp.inf)
    _, idx = lax.top_k(score, n_sel)
    valid = idx <= qpos[None, :, None]
    in_past = idx < past
    ip = jnp.minimum(idx, past - 1)
    phys = jax.vmap(lambda pt, ii: pt[ii])(page_table, ip // PAGE_SIZE)
    off = ip % PAGE_SIZE
    inew = jnp.clip(idx - past, 0, T - 1)

    def pick(pool, new):
        from_past = pool[phys, off].astype(jnp.float32)
        from_new = jax.vmap(lambda nn, ii: nn[ii])(new, inew)
        return jnp.where(in_past[..., None, None], from_past, from_new)

    return sparse_attend(q, pick(cache_k, k), pick(cache_v, v), valid)


def mixer_even(x, w_in, b_gate, g_mlstm, w_out, mstate, attend, chunked):
    B, L, _ = x.shape
    z = (x @ w_in).astype(jnp.float32)
    qa, ka, va, oa, ia, fa, qb, kb, vb, qi, ki, wi = split_cols(z, EVEN_SPLITS)
    gates = jnp.concatenate([ia, fa], -1) + b_gate.astype(jnp.float32)
    ia, fa = jnp.split(gates, 2, axis=-1)
    q = heads_first(qa, MLSTM_HEADS) * MLSTM_DH ** -0.5
    k = heads_first(ka, MLSTM_HEADS)
    v = heads_first(va, MLSTM_HEADS)
    ig = ia.transpose(0, 2, 1)
    lf = jax.nn.log_sigmoid(fa).transpose(0, 2, 1)
    new_m, h = run_chunks(mlstm_chunk, mstate, (q, k, v, ig, lf), MLSTM_CHUNK, chunked)
    ya = head_rms_norm(h, g_mlstm) * jax.nn.sigmoid(oa)
    qb = qb.reshape(B, L, ATT_HEADS, ATT_DH)
    kb = kb.reshape(B, L, ATT_KV_HEADS, ATT_DH)
    vb = vb.reshape(B, L, ATT_KV_HEADS, ATT_DH)
    qi = qi.reshape(B, L, IDX_HEADS, IDX_DIM)
    yb = attend(qb, kb, vb, qi, ki, wi)
    y = jnp.concatenate([ya, yb], -1).astype(x.dtype) @ w_out
    return y, new_m, (kb, vb, ki)


def mixer_odd(x, w_in, lb, g_hgrn, g_ret, w_out, hstate, rstate, pos, chunked):
    z = (x @ w_in).astype(jnp.float32)
    qc, fc, ic, gc, qd, kd, vd, gd = split_cols(z, ODD_SPLITS)
    f = lb + (1.0 - lb) * jax.nn.sigmoid(fc)
    q = heads_first(jax.nn.silu(qc), HGRN_HEADS)
    k = heads_first(1.0 - f, HGRN_HEADS)
    lf = heads_first(jnp.log(f), HGRN_HEADS)
    v = heads_first(ic, HGRN_HEADS)
    new_h, oc = run_chunks(hgrn_chunk, hstate, (q, k, v, lf), HGRN_CHUNK, chunked)
    yc = head_rms_norm(oc, g_hgrn) * jax.nn.silu(gc)
    qr = rotate(heads_first(qd, RET_HEADS), pos)
    kr = rotate(heads_first(kd, RET_HEADS), pos) * RET_DK ** -0.5
    vr = heads_first(vd, RET_HEADS)
    new_r, od = run_chunks(retention_chunk, rstate, (qr, kr, vr), RET_CHUNK, chunked)
    yd = head_rms_norm(od, g_ret) * jax.nn.silu(gd)
    y = jnp.concatenate([yc, yd], -1).astype(x.dtype) @ w_out
    return y, new_h, new_r


def trunk(x, p, W, m_states, h_states, r_states, attends, pos, chunked, lb_all):
    new_m, new_kv, new_h, new_r = [], [], [], []
    for i in range(DEPTH):
        j = i // 2
        x = layer_norm(ALPHA * x + 0.5 * swiglu(x, W['w_ffn_up'][i, 0], W['w_ffn_down'][i, 0]), W['ln_g'][i, 0], W['ln_b'][i, 0])
        if i % 2 == 0:
            y, st, kv = mixer_even(x, W['w_in_even'][j], W['b_gate_mlstm'][j], W['g_mlstm'][j], W['w_out'][i],
                                   m_states[j], attends[j], chunked)
            new_m.append(st)
            new_kv.append(kv)
        else:
            y, hs, rs = mixer_odd(x, W['w_in_odd'][j], lb_all[i], W['g_hgrn'][j], W['g_ret'][j], W['w_out'][i],
                                  h_states[j], r_states[j], pos, chunked)
            new_h.append(hs)
            new_r.append(rs)
        x = layer_norm(ALPHA * x + y, W['ln_g'][i, 1], W['ln_b'][i, 1])
        x = layer_norm(ALPHA * x + 0.5 * swiglu(x, W['w_ffn_up'][i, 1], W['w_ffn_down'][i, 1]), W['ln_g'][i, 2], W['ln_b'][i, 2])
        x = x + jax.nn.sigmoid(x @ W['w_pe_gate'][i]) * (p[i] @ W['w_pe_proj'][i])
    return x, new_m, new_kv, new_h, new_r


def setup_inputs(seed: int = 0) -> dict:
    key = jax.random.key(seed)
    keys = iter(jax.random.split(key, 48))

    def nrm(shape, scale=1.0):
        return jax.random.normal(next(keys), shape, jnp.float32) * scale

    n_pages = PAST_LEN // PAGE_SIZE
    n_used = DEC_BATCH * n_pages
    n_pool = (5 * n_used) // 4
    perm = jax.random.permutation(next(keys), n_pool)
    page_table = perm[:n_used].reshape(DEC_BATCH, n_pages).astype(jnp.int32)
    b_gate = jnp.concatenate([nrm((N_EVEN, MLSTM_HEADS), 0.1),
                              jnp.linspace(3.0, 6.0, MLSTM_HEADS, dtype=jnp.float32)[None, :] + nrm((N_EVEN, MLSTM_HEADS), 0.1)], -1)
    return {
        'x_prompt': nrm((BATCH, SEQ, D_MODEL)),
        'x_sample': nrm((DEC_BATCH, DEC_SEQ, D_MODEL)),
        'state_mlstm_C': nrm((N_EVEN, DEC_BATCH, MLSTM_HEADS, MLSTM_DH, MLSTM_DH), 0.5),
        'state_mlstm_n': nrm((N_EVEN, DEC_BATCH, MLSTM_HEADS, MLSTM_DH), 0.5),
        'state_mlstm_m': nrm((N_EVEN, DEC_BATCH, MLSTM_HEADS)),
        'cache_k': nrm((N_EVEN, n_pool, PAGE_SIZE, ATT_KV_HEADS, ATT_DH)),
        'cache_v': nrm((N_EVEN, n_pool, PAGE_SIZE, ATT_KV_HEADS, ATT_DH)),
        'cache_idx_k': nrm((N_EVEN, n_pool, PAGE_SIZE, IDX_DIM)),
        'state_hgrn': nrm((N_ODD, DEC_BATCH, HGRN_HEADS, HGRN_DK, HGRN_DV), 0.5),
        'state_ret': nrm((N_ODD, DEC_BATCH, RET_HEADS, RET_DK, RET_DV), 0.5),
        'page_table': page_table,
        'p_prompt': nrm((DEPTH, BATCH, SEQ, P_DIM)),
        'p_sample': nrm((DEPTH, DEC_BATCH, DEC_SEQ, P_DIM)),
        'ln_g': 1.0 + nrm((DEPTH, 3, D_MODEL), 0.02),
        'ln_b': nrm((DEPTH, 3, D_MODEL), 0.02),
        'w_ffn_up': nrm((DEPTH, 2, D_MODEL, 2 * D_FF), D_MODEL ** -0.5),
        'w_ffn_down': nrm((DEPTH, 2, D_FF, D_MODEL), BETA * D_FF ** -0.5),
        'w_in_even': nrm((N_EVEN, D_MODEL, EVEN_WIDTH), D_MODEL ** -0.5),
        'b_gate_mlstm': b_gate,
        'g_mlstm': 1.0 + nrm((N_EVEN, HALF), 0.02),
        'w_in_odd': nrm((N_ODD, D_MODEL, ODD_WIDTH), D_MODEL ** -0.5),
        'hgrn_lb': nrm((DEPTH, HGRN_HEADS * HGRN_DK), 0.1),
        'g_hgrn': 1.0 + nrm((N_ODD, HALF), 0.02),
        'g_ret': 1.0 + nrm((N_ODD, HALF), 0.02),
        'w_out': nrm((DEPTH, 2 * HALF, D_MODEL), BETA * (2 * HALF) ** -0.5),
        'w_pe_gate': nrm((DEPTH, D_MODEL, D_MODEL), D_MODEL ** -0.5),
        'w_pe_proj': nrm((DEPTH, P_DIM, D_MODEL), P_DIM ** -0.5),
    }


def reference(x_prompt, x_sample, state_mlstm_C, state_mlstm_n, state_mlstm_m, cache_k, cache_v, cache_idx_k,
              state_hgrn, state_ret, page_table, p_prompt, p_sample, ln_g, ln_b, w_ffn_up, w_ffn_down,
              w_in_even, b_gate_mlstm, g_mlstm, w_in_odd, hgrn_lb, g_hgrn, g_ret, w_out, w_pe_gate, w_pe_proj):
    f32 = jnp.float32
    W = {'ln_g': ln_g, 'ln_b': ln_b, 'w_ffn_up': w_ffn_up, 'w_ffn_down': w_ffn_down,
         'w_in_even': w_in_even, 'b_gate_mlstm': b_gate_mlstm, 'g_mlstm': g_mlstm,
         'w_in_odd': w_in_odd, 'g_hgrn': g_hgrn, 'g_ret': g_ret, 'w_out': w_out,
         'w_pe_gate': w_pe_gate, 'w_pe_proj': w_pe_proj}
    lb_cum = jnp.cumsum(jax.nn.softmax(hgrn_lb.astype(f32), axis=0), axis=0)
    lb_all = lb_cum - lb_cum[:1]
    B, S = x_prompt.shape[:2]
    T = x_sample.shape[1]
    past = page_table.shape[1] * PAGE_SIZE

    m0 = (jnp.zeros((B, MLSTM_HEADS, MLSTM_DH, MLSTM_DH), f32), jnp.zeros((B, MLSTM_HEADS, MLSTM_DH), f32),
          jnp.zeros((B, MLSTM_HEADS), f32))
    y_prompt, m_p, kv_p, h_p, r_p = trunk(
        x_prompt, p_prompt, W, [m0] * N_EVEN,
        [jnp.zeros((B, HGRN_HEADS, HGRN_DK, HGRN_DV), f32)] * N_ODD,
        [jnp.zeros((B, RET_HEADS, RET_DK, RET_DV), f32)] * N_ODD,
        [dsa_prompt] * N_EVEN, jnp.arange(S), True, lb_all)

    m_init = [(state_mlstm_C[j].astype(f32), state_mlstm_n[j].astype(f32), state_mlstm_m[j].astype(f32))
              for j in range(N_EVEN)]
    att_s = [functools.partial(dsa_sample, cache_k=cache_k[j], cache_v=cache_v[j], cache_ik=cache_idx_k[j],
                               page_table=page_table) for j in range(N_EVEN)]
    y_sample, m_s, kv_s, h_s, r_s = trunk(
        x_sample, p_sample, W, m_init,
        [state_hgrn[j].astype(f32) for j in range(N_ODD)],
        [state_ret[j].astype(f32) for j in range(N_ODD)],
        att_s, past + jnp.arange(T), False, lb_all)

    def stk(items, idx, like):
        return jnp.stack([it[idx] for it in items]).astype(like.dtype)

    mC_p, mC_s = stk(m_p, 0, state_mlstm_C), stk(m_s, 0, state_mlstm_C)
    mn_p, mn_s = stk(m_p, 1, state_mlstm_n), stk(m_s, 1, state_mlstm_n)
    mm_p, mm_s = stk(m_p, 2, state_mlstm_m), stk(m_s, 2, state_mlstm_m)
    k_p, k_s = stk(kv_p, 0, cache_k), stk(kv_s, 0, cache_k)
    v_p, v_s = stk(kv_p, 1, cache_v), stk(kv_s, 1, cache_v)
    ik_p, ik_s = stk(kv_p, 2, cache_idx_k), stk(kv_s, 2, cache_idx_k)
    hg_p, hg_s = jnp.stack(h_p).astype(state_hgrn.dtype), jnp.stack(h_s).astype(state_hgrn.dtype)
    rt_p, rt_s = jnp.stack(r_p).astype(state_ret.dtype), jnp.stack(r_s).astype(state_ret.dtype)
    return (y_prompt, y_sample, mC_p, mC_s, mn_p, mn_s, mm_p, mm_s, k_p, k_s, v_p, v_s, ik_p, ik_s, hg_p, hg_s, rt_p, rt_s)
```

```python
import functools
import math

import jax
import jax.numpy as jnp
from jax import lax
from jax.experimental import pallas as pl
from jax.experimental.pallas import tpu as pltpu

F32 = jnp.float32
BF16 = jnp.bfloat16

D_MODEL = 2048
DEPTH = 2
PAGE_SIZE = 128
HALF = D_MODEL // 2
MLSTM_HEADS = 4
MLSTM_DH = HALF // MLSTM_HEADS
ATT_HEADS = 8
ATT_DH = HALF // ATT_HEADS
ATT_KV_HEADS = 2
ATT_GROUP = ATT_HEADS // ATT_KV_HEADS
IDX_HEADS = 8
IDX_DIM = 64
TOPK_MAX = 256
Q_BLOCK = 128
HGRN_HEADS = 8
HGRN_DK = 128
HGRN_DV = HALF // HGRN_HEADS
RET_HEADS = 4
RET_DK = HALF // RET_HEADS
RET_THETA = 10000.0
D_FF = 5632
P_DIM = 256
ALPHA = (2 * DEPTH) ** 0.25
LN_EPS = 1e-5

EV_QA, EV_KA, EV_VA, EV_OA = 0, HALF, 2 * HALF, 3 * HALF
EV_QB = 4 * HALF
EV_KB = EV_QB + ATT_HEADS * ATT_DH
EV_VB = EV_KB + ATT_KV_HEADS * ATT_DH
EV_QI = EV_VB + ATT_KV_HEADS * ATT_DH
EV_MISC = EV_QI + IDX_HEADS * IDX_DIM
LANES = 128
EV_WIDTH = EV_MISC + LANES
MISC_KI, MISC_WI = 0, IDX_DIM
MISC_IG = MISC_WI + IDX_HEADS
MISC_FG = MISC_IG + MLSTM_HEADS

CHUNK = 64
HGRN_SUB = 16
VMEM_LIMIT = 56 * 1024 * 1024
INT_MIN = -2 ** 31
NEG_BIG = -1e30

NT_DIMS = (((1,), (1,)), ((), ()))
TN_DIMS = (((0,), (0,)), ((), ()))


def _cparams(*sem):
    return pltpu.CompilerParams(dimension_semantics=sem, vmem_limit_bytes=VMEM_LIMIT)


def _row_tile(m):
    for t in (512, 256, 128, 64, 32, 16, 8):
        if m % t == 0:
            return t
    raise ValueError(f"row count {m} is not a multiple of 8")


def _col_tile(n):
    for t in (1024, 896, 512, 256, 128):
        if n % t == 0:
            return t
    raise ValueError(f"column count {n} is not a multiple of 128")


def _layer_norm(y, g, b):
    mu = jnp.mean(y, -1, keepdims=True)
    d = y - mu
    var = jnp.mean(d * d, -1, keepdims=True)
    return d * lax.rsqrt(var + LN_EPS) * g + b


def _rms_norm(h):
    return h * lax.rsqrt(jnp.mean(h * h, -1, keepdims=True) + LN_EPS)


def _sigmoid(x):
    return jax.nn.sigmoid(x)


def _silu(x):
    return x * jax.nn.sigmoid(x)


def _log_sigmoid(x):
    return jnp.minimum(x, 0.0) - jnp.log1p(jnp.exp(-jnp.abs(x)))


def _iota(shape, dim):
    return lax.broadcasted_iota(jnp.int32, shape, dim)


def _row_to_col(row):
    n = row.shape[1]
    eye = _iota((n, n), 0) == _iota((n, n), 1)
    return jnp.sum(jnp.where(eye, row, 0.0), axis=1, keepdims=True)


def _bdot(a, b):
    return jnp.dot(a.astype(BF16), b.astype(BF16), preferred_element_type=F32)


def _bdot_nt(a, b):
    return lax.dot_general(a.astype(BF16), b.astype(BF16), NT_DIMS, preferred_element_type=F32)


def _bdot_tn(a, b):
    return lax.dot_general(a.astype(BF16), b.astype(BF16), TN_DIMS, preferred_element_type=F32)


def _ffn_kernel(x_ref, wg_ref, wu_ref, wd_ref, g_ref, b_ref, o_ref, xb_ref, acc_ref):
    j = pl.program_id(1)

    @pl.when(j == 0)
    def _():
        xb_ref[...] = x_ref[...].astype(BF16)
        acc_ref[...] = jnp.zeros_like(acc_ref)

    xb = xb_ref[...]
    hg = jnp.dot(xb, wg_ref[...], preferred_element_type=F32)
    hu = jnp.dot(xb, wu_ref[...], preferred_element_type=F32)
    act = _silu(hg) * hu
    acc_ref[...] += jnp.dot(act.astype(BF16), wd_ref[...], preferred_element_type=F32)

    @pl.when(j == pl.num_programs(1) - 1)
    def _():
        y = ALPHA * x_ref[...] + 0.5 * acc_ref[...]
        o_ref[...] = _layer_norm(y, g_ref[...], b_ref[...])


def _ffn(x, w_up, w_down, ln_g, ln_b, layer, which, ln_idx):
    m = x.shape[0]
    tm, tf = _row_tile(m), 512
    nf = D_FF // tf
    ln_row = layer * 3 + ln_idx
    return pl.pallas_call(
        _ffn_kernel,
        grid=(m // tm, nf),
        in_specs=[
            pl.BlockSpec((tm, D_MODEL), lambda i, j: (i, 0)),
            pl.BlockSpec((None, None, D_MODEL, tf), lambda i, j: (layer, which, 0, j)),
            pl.BlockSpec((None, None, D_MODEL, tf), lambda i, j: (layer, which, 0, j + nf)),
            pl.BlockSpec((None, None, tf, D_MODEL), lambda i, j: (layer, which, j, 0)),
            pl.BlockSpec((None, 1, D_MODEL), lambda i, j: (ln_row, 0, 0)),
            pl.BlockSpec((None, 1, D_MODEL), lambda i, j: (ln_row, 0, 0)),
        ],
        out_specs=pl.BlockSpec((tm, D_MODEL), lambda i, j: (i, 0)),
        out_shape=jax.ShapeDtypeStruct((m, D_MODEL), F32),
        scratch_shapes=[pltpu.VMEM((tm, D_MODEL), BF16), pltpu.VMEM((tm, D_MODEL), F32)],
        compiler_params=_cparams("parallel", "arbitrary"),
        name="ffn",
    )(x, w_up, w_up, w_down, ln_g, ln_b)


def _proj_kernel(x_ref, w_ref, o_ref, xb_ref):
    @pl.when(pl.program_id(1) == 0)
    def _():
        xb_ref[...] = x_ref[...].astype(BF16)

    o_ref[...] = jnp.dot(xb_ref[...], w_ref[...], preferred_element_type=F32)


def _proj(x, w, idx):
    m, n = x.shape[0], w.shape[2]
    tm, tn = _row_tile(m), _col_tile(n)
    return pl.pallas_call(
        _proj_kernel,
        grid=(m // tm, n // tn),
        in_specs=[
            pl.BlockSpec((tm, D_MODEL), lambda i, j: (i, 0)),
            pl.BlockSpec((None, D_MODEL, tn), lambda i, j: (idx, 0, j)),
        ],
        out_specs=pl.BlockSpec((tm, tn), lambda i, j: (i, j)),
        out_shape=jax.ShapeDtypeStruct((m, n), F32),
        scratch_shapes=[pltpu.VMEM((tm, D_MODEL), BF16)],
        compiler_params=_cparams("parallel", "arbitrary"),
        name="in_proj",
    )(x, w)


def _outproj_kernel(x_ref, ya_ref, yb_ref, wa_ref, wb_ref, g_ref, b_ref, o_ref):
    y = jnp.dot(ya_ref[...], wa_ref[...], preferred_element_type=F32)
    y += jnp.dot(yb_ref[...], wb_ref[...], preferred_element_type=F32)
    o_ref[...] = _layer_norm(ALPHA * x_ref[...] + y, g_ref[...], b_ref[...])


def _outproj(x, ya, yb, w_out, ln_g, ln_b, layer):
    m = x.shape[0]
    tm = _row_tile(m)
    ln_row = layer * 3 + 1
    return pl.pallas_call(
        _outproj_kernel,
        grid=(m // tm,),
        in_specs=[
            pl.BlockSpec((tm, D_MODEL), lambda i: (i, 0)),
            pl.BlockSpec((tm, HALF), lambda i: (i, 0)),
            pl.BlockSpec((tm, HALF), lambda i: (i, 0)),
            pl.BlockSpec((None, HALF, D_MODEL), lambda i: (layer, 0, 0)),
            pl.BlockSpec((None, HALF, D_MODEL), lambda i: (layer, 1, 0)),
            pl.BlockSpec((None, 1, D_MODEL), lambda i: (ln_row, 0, 0)),
            pl.BlockSpec((None, 1, D_MODEL), lambda i: (ln_row, 0, 0)),
        ],
        out_specs=pl.BlockSpec((tm, D_MODEL), lambda i: (i, 0)),
        out_shape=jax.ShapeDtypeStruct((m, D_MODEL), F32),
        compiler_params=_cparams("parallel"),
        name="out_proj",
    )(x, ya, yb, w_out, w_out, ln_g, ln_b)


def _pe_kernel(x_ref, p_ref, wg_ref, wp_ref, o_ref):
    x = x_ref[...]
    gate = _sigmoid(jnp.dot(x.astype(BF16), wg_ref[...], preferred_element_type=F32))
    o_ref[...] = x + gate * jnp.dot(p_ref[...].astype(BF16), wp_ref[...], preferred_element_type=F32)


def _pe(x, p, w_gate, w_proj, layer):
    m = x.shape[0]
    tm = _row_tile(m)
    return pl.pallas_call(
        _pe_kernel,
        grid=(m // tm,),
        in_specs=[
            pl.BlockSpec((tm, D_MODEL), lambda i: (i, 0)),
            pl.BlockSpec((None, tm, P_DIM), lambda i: (layer, i, 0)),
            pl.BlockSpec((None, D_MODEL, D_MODEL), lambda i: (layer, 0, 0)),
            pl.BlockSpec((None, P_DIM, D_MODEL), lambda i: (layer, 0, 0)),
        ],
        out_specs=pl.BlockSpec((tm, D_MODEL), lambda i: (i, 0)),
        out_shape=jax.ShapeDtypeStruct((m, D_MODEL), F32),
        compiler_params=_cparams("parallel"),
        name="pe_gate",
    )(x, p, w_gate, w_proj)


def _mlstm_prompt_kernel(q_ref, k_ref, v_ref, o_ref, misc_ref, bias_ref, gain_ref,
                         y_ref, c_ref, n_ref, m_ref, m_sc):
    h = pl.program_id(1)
    chunk = q_ref.shape[0]

    @pl.when(pl.program_id(2) == 0)
    def _():
        c_ref[...] = jnp.zeros_like(c_ref)
        n_ref[...] = jnp.zeros_like(n_ref)
        m_sc[...] = jnp.zeros_like(m_sc)

    gates = misc_ref[...] + bias_ref[...]
    lane = _iota(gates.shape, 1)
    ig_col = jnp.sum(jnp.where(lane == MISC_IG + h, gates, 0.0), axis=1, keepdims=True)
    fg_col = jnp.sum(jnp.where(lane == MISC_FG + h, gates, 0.0), axis=1, keepdims=True)
    lf_col = _log_sigmoid(fg_col)
    ti = _iota((chunk, chunk), 0)
    si = _iota((chunk, chunk), 1)
    eye = ti == si
    causal = si <= ti
    lf_row = jnp.sum(jnp.where(eye, lf_col, 0.0), axis=0, keepdims=True)
    ig_row = jnp.sum(jnp.where(eye, ig_col, 0.0), axis=0, keepdims=True)
    b_col = jnp.sum(jnp.where(causal, lf_row, 0.0), axis=1, keepdims=True)
    b_row = jnp.sum(jnp.where(ti <= si, lf_col, 0.0), axis=0, keepdims=True)
    m_prev = m_sc[:, :1]
    dlog = jnp.where(causal, b_col - b_row + ig_row, -jnp.inf)
    g_col = b_col + m_prev
    m_t = jnp.maximum(g_col, jnp.max(dlog, axis=1, keepdims=True))
    dw = jnp.exp(dlog - m_t)
    gw = jnp.exp(g_col - m_t)

    q = q_ref[...] * (MLSTM_DH ** -0.5)
    k = k_ref[...]
    v = v_ref[...]
    c_state = c_ref[...]
    n_state = n_ref[...]
    s = _bdot_nt(q, k) * dw
    num = _bdot(s, v) + gw * _bdot(q, c_state)
    den = jnp.sum(s, axis=1, keepdims=True) + gw * jnp.sum(q * n_state, axis=1, keepdims=True)
    hid = num / jnp.maximum(jnp.abs(den), jnp.exp(-m_t))

    b_last = b_col[chunk - 1:chunk, :]
    wlog = b_last - b_col + ig_col
    m_new = jnp.maximum(b_last + m_prev, jnp.max(wlog, axis=0, keepdims=True))
    w_col = jnp.exp(wlog - m_new)
    decay = jnp.exp(b_last + m_prev - m_new)
    kw = k * w_col
    c_ref[...] = decay * c_state + _bdot_tn(kw, v)
    n_ref[...] = decay * n_state + jnp.sum(kw, axis=0, keepdims=True)
    m_row = jnp.broadcast_to(m_new, m_sc.shape)
    m_sc[...] = m_row
    m_ref[...] = m_row

    y_ref[...] = (_rms_norm(hid) * gain_ref[...] * _sigmoid(o_ref[...])).astype(BF16)


def _mlstm_prompt(z, bias_row, gain, batch, seq, layer_j):
    nc = seq // CHUNK
    dh = MLSTM_DH

    def col(base):
        return lambda b, h, c: (b * nc + c, base // dh + h)

    state = lambda b, h, c: (b, h, 0, 0)
    return pl.pallas_call(
        _mlstm_prompt_kernel,
        grid=(batch, MLSTM_HEADS, nc),
        in_specs=[
            pl.BlockSpec((CHUNK, dh), col(EV_QA)),
            pl.BlockSpec((CHUNK, dh), col(EV_KA)),
            pl.BlockSpec((CHUNK, dh), col(EV_VA)),
            pl.BlockSpec((CHUNK, dh), col(EV_OA)),
            pl.BlockSpec((CHUNK, LANES), lambda b, h, c: (b * nc + c, EV_MISC // LANES)),
            pl.BlockSpec((1, LANES), lambda b, h, c: (0, 0)),
            pl.BlockSpec((None, 1, dh), lambda b, h, c: (layer_j, 0, h)),
        ],
        out_specs=[
            pl.BlockSpec((CHUNK, dh), lambda b, h, c: (b * nc + c, h)),
            pl.BlockSpec((None, None, dh, dh), state),
            pl.BlockSpec((None, None, 1, dh), state),
            pl.BlockSpec((None, None, 1, LANES), state),
        ],
        out_shape=[
            jax.ShapeDtypeStruct((batch * seq, HALF), BF16),
            jax.ShapeDtypeStruct((batch, MLSTM_HEADS, dh, dh), F32),
            jax.ShapeDtypeStruct((batch, MLSTM_HEADS, 1, dh), F32),
            jax.ShapeDtypeStruct((batch, MLSTM_HEADS, 1, LANES), F32),
        ],
        scratch_shapes=[pltpu.VMEM((1, LANES), F32)],
        compiler_params=_cparams("parallel", "parallel", "arbitrary"),
        name="mlstm_prompt",
    )(z, z, z, z, z, bias_row, gain)


def _hgrn_lower_bound(lb_ref, layer):
    rows = [lb_ref[j] for j in range(DEPTH)]
    mx = functools.reduce(jnp.maximum, rows)
    e = [jnp.exp(r - mx) for r in rows]
    total = functools.reduce(jnp.add, e)
    acc = jnp.zeros_like(total)
    for j in range(1, layer + 1):
        acc = acc + e[j] / total
    return acc


def _hgrn_prompt_kernel(q_ref, f_ref, i_ref, g_ref, lb_ref, gain_ref, y_ref, s_ref, *, layer):
    chunk = q_ref.shape[0]

    @pl.when(pl.program_id(2) == 0)
    def _():
        s_ref[...] = jnp.zeros_like(s_ref)

    lb = _hgrn_lower_bound(lb_ref, layer)
    f = lb + (1.0 - lb) * _sigmoid(f_ref[...])
    lf = jnp.log(f)
    kk = 1.0 - f
    q = _silu(q_ref[...])
    v = i_ref[...]
    tri = (_iota((chunk, chunk), 1) <= _iota((chunk, chunk), 0)).astype(F32)
    b = jnp.dot(tri, lf, preferred_element_type=F32, precision=lax.Precision.HIGHEST)
    b_last = b[chunk - 1:chunk, :]
    state = s_ref[...]

    o = _bdot(q * jnp.exp(b), state)
    parts = []
    for blk in range(chunk // HGRN_SUB):
        r = blk * HGRN_SUB
        o_blk = o[r:r + HGRN_SUB]
        if blk > 0:
            b_ref_row = b[r - 1:r]
            qt = q[r:r + HGRN_SUB] * jnp.exp(b[r:r + HGRN_SUB] - b_ref_row)
            kt = kk[:r] * jnp.exp(b_ref_row - b[:r])
            o_blk = o_blk + _bdot(_bdot_nt(qt, kt), v[:r])
        parts.append(o_blk)
    o = jnp.concatenate(parts, axis=0)

    t_in_blk = _iota((chunk, 1), 0) % HGRN_SUB
    for dlt in range(HGRN_SUB):
        if dlt == 0:
            k_s, b_s, v_s = kk, b, v
        else:
            k_s, b_s, v_s = (pltpu.roll(a, dlt, 0) for a in (kk, b, v))
        e = jnp.exp(jnp.minimum(b - b_s, 0.0))
        a = jnp.sum(q * k_s * e, axis=1, keepdims=True)
        o = o + jnp.where(t_in_blk >= dlt, a, 0.0) * v_s

    s_ref[...] = _row_to_col(jnp.exp(b_last)) * state + _bdot_tn(kk * jnp.exp(b_last - b), v)
    y_ref[...] = (_rms_norm(o) * gain_ref[...] * _silu(g_ref[...])).astype(BF16)


def _hgrn_prompt(z, lb3, gain, batch, seq, layer, layer_j):
    nc = seq // CHUNK
    dk = HGRN_DK

    def col(group):
        return lambda b, h, c: (b * nc + c, group * HGRN_HEADS + h)

    return pl.pallas_call(
        functools.partial(_hgrn_prompt_kernel, layer=layer),
        grid=(batch, HGRN_HEADS, nc),
        in_specs=[
            pl.BlockSpec((CHUNK, dk), col(0)),
            pl.BlockSpec((CHUNK, dk), col(1)),
            pl.BlockSpec((CHUNK, dk), col(2)),
            pl.BlockSpec((CHUNK, dk), col(3)),
            pl.BlockSpec((DEPTH, 1, dk), lambda b, h, c: (0, 0, h)),
            pl.BlockSpec((None, 1, dk), lambda b, h, c: (layer_j, 0, h)),
        ],
        out_specs=[
            pl.BlockSpec((CHUNK, dk), lambda b, h, c: (b * nc + c, h)),
            pl.BlockSpec((None, None, dk, HGRN_DV), lambda b, h, c: (b, h, 0, 0)),
        ],
        out_shape=[
            jax.ShapeDtypeStruct((batch * seq, HALF), BF16),
            jax.ShapeDtypeStruct((batch, HGRN_HEADS, dk, HGRN_DV), F32),
        ],
        compiler_params=_cparams("parallel", "parallel", "arbitrary"),
        name="hgrn_prompt",
    )(z, z, z, z, lb3, gain)


def _ret_log_gamma(h):
    lg = jnp.float32(math.log1p(-2.0 ** (-5 - (RET_HEADS - 1))))
    for hh in range(RET_HEADS - 2, -1, -1):
        lg = jnp.where(h == hh, jnp.float32(math.log1p(-2.0 ** (-5 - hh))), lg)
    return jnp.full((1, 1), lg, F32)


def _rotate(x, cos, sin):
    half = x.shape[1] // 2
    x1, x2 = x[:, :half], x[:, half:]
    return jnp.concatenate([x1 * cos - x2 * sin, x1 * sin + x2 * cos], axis=1)


def _ret_prompt_kernel(q_ref, k_ref, v_ref, g_ref, cos_ref, sin_ref, gain_ref, y_ref, s_ref):
    chunk = q_ref.shape[0]

    @pl.when(pl.program_id(2) == 0)
    def _():
        s_ref[...] = jnp.zeros_like(s_ref)

    lg = _ret_log_gamma(pl.program_id(1))
    cos, sin = cos_ref[...], sin_ref[...]
    qr = _rotate(q_ref[...], cos, sin)
    kr = _rotate(k_ref[...], cos, sin) * (RET_DK ** -0.5)
    v = v_ref[...]
    rel = (_iota((chunk, chunk), 0) - _iota((chunk, chunk), 1)).astype(F32)
    dmask = jnp.where(rel >= 0.0, jnp.exp(lg * jnp.maximum(rel, 0.0)), 0.0)
    pos = _iota((chunk, 1), 0).astype(F32)
    state = s_ref[...]
    o = _bdot(_bdot_nt(qr, kr) * dmask, v) + _bdot(qr, state) * jnp.exp(lg * (pos + 1.0))
    tail = jnp.exp(lg * (chunk - 1.0 - pos))
    s_ref[...] = jnp.exp(lg * float(chunk)) * state + _bdot_tn(kr * tail, v)
    y_ref[...] = (_rms_norm(o) * gain_ref[...] * _silu(g_ref[...])).astype(BF16)


def _ret_prompt(z, cos, sin, gain, batch, seq, layer_j):
    nc = seq // CHUNK
    dk = RET_DK

    def col(group):
        return lambda b, h, c: (b * nc + c, (4 + group) * RET_HEADS + h)

    return pl.pallas_call(
        _ret_prompt_kernel,
        grid=(batch, RET_HEADS, nc),
        in_specs=[
            pl.BlockSpec((CHUNK, dk), col(0)),
            pl.BlockSpec((CHUNK, dk), col(1)),
            pl.BlockSpec((CHUNK, dk), col(2)),
            pl.BlockSpec((CHUNK, dk), col(3)),
            pl.BlockSpec((CHUNK, dk // 2), lambda b, h, c: (c, 0)),
            pl.BlockSpec((CHUNK, dk // 2), lambda b, h, c: (c, 0)),
            pl.BlockSpec((None, 1, dk), lambda b, h, c: (layer_j, 0, h)),
        ],
        out_specs=[
            pl.BlockSpec((CHUNK, dk), lambda b, h, c: (b * nc + c, h)),
            pl.BlockSpec((None, None, dk, dk), lambda b, h, c: (b, h, 0, 0)),
        ],
        out_shape=[
            jax.ShapeDtypeStruct((batch * seq, HALF), BF16),
            jax.ShapeDtypeStruct((batch, RET_HEADS, dk, dk), F32),
        ],
        compiler_params=_cparams("parallel", "parallel", "arbitrary"),
        name="ret_prompt",
    )(z, z, z, z, cos, sin, gain)


def _score_key(score):
    bits = lax.bitcast_convert_type(score, jnp.int32)
    return jnp.where(bits < 0, -(bits & 0x7FFFFFFF), bits)


def _count(mask):
    return jnp.sum(jnp.where(mask, 1.0, 0.0), axis=1, keepdims=True)


def _kth_largest_key(count_ge, rows, n_sel):
    def body(it, t_off):
        cand = t_off | lax.shift_left(jnp.int32(1), 31 - it)
        return jnp.where(count_ge(cand ^ INT_MIN) >= n_sel, cand, t_off)

    t_off = lax.fori_loop(0, 32, body, jnp.zeros((rows, 1), jnp.int32))
    return t_off ^ INT_MIN


def _tie_cutoff(count_eq_below, rows, budget, n_bits):
    def body(it, cut):
        cand = cut | lax.shift_left(jnp.int32(1), n_bits - 1 - it)
        return jnp.where(count_eq_below(cand) <= budget, cand, cut)

    return lax.fori_loop(0, n_bits, body, jnp.zeros((rows, 1), jnp.int32))


def _dsa_prompt_kernel(q_ref, k_ref, v_ref, qi_ref, qmisc_ref, kmisc_ref, y_ref, *, n_sel):
    qblk, seq = q_ref.shape[0], k_ref.shape[0]
    ki = kmisc_ref[:, MISC_KI:MISC_KI + IDX_DIM].astype(BF16)
    qmisc = qmisc_ref[...]
    score = jnp.zeros((qblk, seq), F32)
    for h in range(IDX_HEADS):
        r = _bdot_nt(qi_ref[:, h * IDX_DIM:(h + 1) * IDX_DIM], ki) * (IDX_DIM ** -0.5)
        w = qmisc[:, MISC_WI + h:MISC_WI + h + 1] * (IDX_HEADS ** -0.5)
        score = score + jnp.maximum(r, 0.0) * w

    qpos = pl.program_id(1) * qblk + _iota((qblk, seq), 0)
    kpos = _iota((qblk, seq), 1)
    valid = kpos <= qpos
    key = jnp.where(valid, _score_key(score), INT_MIN)
    thr = _kth_largest_key(lambda t: _count(key >= t), qblk, n_sel)
    above = key > thr
    tied = key == thr
    budget = n_sel - _count(above)
    cut = _tie_cutoff(lambda j: _count(tied & (kpos < j)), qblk, budget, (seq + 1).bit_length())
    sel = valid & (above | (tied & (kpos < cut)))

    scale = ATT_DH ** -0.5
    outs = []
    for g in range(ATT_KV_HEADS):
        kg = k_ref[:, g * ATT_DH:(g + 1) * ATT_DH].astype(BF16)
        vg = v_ref[:, g * ATT_DH:(g + 1) * ATT_DH].astype(BF16)
        for hh in range(ATT_GROUP):
            h = g * ATT_GROUP + hh
            logits = _bdot_nt(q_ref[:, h * ATT_DH:(h + 1) * ATT_DH], kg) * scale
            logits = jnp.where(sel, logits, -jnp.inf)
            e = jnp.exp(logits - jnp.max(logits, axis=1, keepdims=True))
            p = e / jnp.sum(e, axis=1, keepdims=True)
            outs.append(_bdot(p, vg))
    y_ref[...] = jnp.concatenate(outs, axis=1).astype(BF16)


def _dsa_prompt(z, batch, seq):
    nb = seq // Q_BLOCK
    n_sel = min(TOPK_MAX, seq // 4)
    kv_w = ATT_KV_HEADS * ATT_DH
    qi_w = IDX_HEADS * IDX_DIM
    return pl.pallas_call(
        functools.partial(_dsa_prompt_kernel, n_sel=n_sel),
        grid=(batch, nb),
        in_specs=[
            pl.BlockSpec((Q_BLOCK, HALF), lambda b, i: (b * nb + i, EV_QB // HALF)),
            pl.BlockSpec((seq, kv_w), lambda b, i: (b, EV_KB // kv_w)),
            pl.BlockSpec((seq, kv_w), lambda b, i: (b, EV_VB // kv_w)),
            pl.BlockSpec((Q_BLOCK, qi_w), lambda b, i: (b * nb + i, EV_QI // qi_w)),
            pl.BlockSpec((Q_BLOCK, LANES), lambda b, i: (b * nb + i, EV_MISC // LANES)),
            pl.BlockSpec((seq, LANES), lambda b, i: (b, EV_MISC // LANES)),
        ],
        out_specs=pl.BlockSpec((Q_BLOCK, HALF), lambda b, i: (b * nb + i, 0)),
        out_shape=jax.ShapeDtypeStruct((batch * seq, HALF), BF16),
        compiler_params=_cparams("parallel", "arbitrary"),
        name="dsa_prompt",
    )(z, z, z, z, z, z)


def _dsa_sample_score_kernel(pt_ref, ik_ref, qi_ref, wi_ref, o_ref):
    r = _bdot_nt(qi_ref[...], ik_ref[...]) * (IDX_DIM ** -0.5)
    w = wi_ref[...] * (IDX_HEADS ** -0.5)
    o_ref[pl.ds(pl.program_id(1), 1), :] = jnp.sum(jnp.maximum(r, 0.0) * w, axis=0, keepdims=True)


def _dsa_sample_scores(page_table, cache_ik, qi, wi, layer_j):
    dec_batch, n_pages = page_table.shape
    grid_spec = pltpu.PrefetchScalarGridSpec(
        num_scalar_prefetch=1,
        grid=(dec_batch, n_pages),
        in_specs=[
            pl.BlockSpec((None, None, PAGE_SIZE, IDX_DIM),
                         lambda b, p, pt: (layer_j, pt[b * n_pages + p], 0, 0)),
            pl.BlockSpec((None, IDX_HEADS, IDX_DIM), lambda b, p, pt: (b, 0, 0)),
            pl.BlockSpec((None, IDX_HEADS, 1), lambda b, p, pt: (b, 0, 0)),
        ],
        out_specs=pl.BlockSpec((None, n_pages, PAGE_SIZE), lambda b, p, pt: (b, 0, 0)),
    )
    return pl.pallas_call(
        _dsa_sample_score_kernel,
        grid_spec=grid_spec,
        out_shape=jax.ShapeDtypeStruct((dec_batch, n_pages, PAGE_SIZE), F32),
        compiler_params=_cparams("parallel", "arbitrary"),
        name="dsa_sample_scores",
    )(page_table.reshape(-1), cache_ik, qi, wi)


def _dsa_sample_attn_kernel(pt_ref, sc_ref, q_ref, qi_ref, wi_ref, kin_ref, knew_ref, vnew_ref,
                            kc_ref, vc_ref, y_ref, thr_sc, cut_sc, new_sc, m_sc, l_sc, acc_sc,
                            *, n_sel):
    p = pl.program_id(1)
    n_pages = pl.num_programs(1)
    n_pages_static = sc_ref.shape[0]
    past = n_pages_static * PAGE_SIZE
    lane = _iota((1, PAGE_SIZE), 1)

    @pl.when(p == 0)
    def _():
        r_new = jnp.sum(qi_ref[...] * kin_ref[...], axis=1, keepdims=True) * (IDX_DIM ** -0.5)
        s_new = jnp.sum(jnp.maximum(r_new, 0.0) * (wi_ref[...] * (IDX_HEADS ** -0.5)),
                        axis=0, keepdims=True)
        key_new = _score_key(s_new)
        key = _score_key(sc_ref[...])
        idx = _iota(key.shape, 0) * PAGE_SIZE + _iota(key.shape, 1)

        def total(mask, mask_new):
            return (jnp.sum(_count(mask), axis=0, keepdims=True) + jnp.where(mask_new, 1.0, 0.0))

        thr = _kth_largest_key(lambda t: total(key >= t, key_new >= t), 1, n_sel)
        above, tied = key > thr, key == thr
        budget = n_sel - total(above, key_new > thr)
        cut = _tie_cutoff(lambda j: total(tied & (idx < j), (key_new == thr) & (past < j)),
                          1, budget, (past + 2).bit_length())
        thr_sc[...] = jnp.broadcast_to(thr, thr_sc.shape)
        cut_sc[...] = jnp.broadcast_to(cut, cut_sc.shape)
        new_sc[...] = jnp.broadcast_to(key_new, new_sc.shape)
        m_sc[...] = jnp.full_like(m_sc, NEG_BIG)
        l_sc[...] = jnp.zeros_like(l_sc)
        acc_sc[...] = jnp.zeros_like(acc_sc)

    thr = thr_sc[:, :1]
    cut = cut_sc[:, :1]
    scale = ATT_DH ** -0.5
    q = q_ref[...]
    first_group = _iota((ATT_HEADS, 1), 0) < ATT_GROUP

    def by_group(fn):
        return jnp.where(first_group, fn(0), fn(1))

    def online_update(logits, pv):
        m_old = m_sc[...]
        m_new = jnp.maximum(m_old, jnp.max(logits, axis=1, keepdims=True))
        alpha = jnp.exp(m_old - m_new)
        pr = jnp.exp(logits - m_new)
        l_sc[...] = alpha * l_sc[...] + jnp.sum(pr, axis=1, keepdims=True)
        acc_sc[...] = alpha * acc_sc[...] + pv(pr)
        m_sc[...] = m_new

    key = _score_key(sc_ref[pl.ds(p, 1), :])
    idx = p * PAGE_SIZE + lane
    sel = (key > thr) | ((key == thr) & (idx < cut))
    kc, vc = kc_ref[...], vc_ref[...]
    logits = by_group(lambda g: _bdot_nt(q, kc[:, g * ATT_DH:(g + 1) * ATT_DH])) * scale
    logits = jnp.where(sel, logits, -jnp.inf)
    online_update(logits, lambda pr: by_group(lambda g: _bdot(pr, vc[:, g * ATT_DH:(g + 1) * ATT_DH])))

    @pl.when(p == n_pages - 1)
    def _():
        key_new = new_sc[:, :1]
        sel_new = (key_new > thr) | ((key_new == thr) & (past < cut))
        k_new = by_group(lambda g: knew_ref[g:g + 1, :])
        v_new = by_group(lambda g: vnew_ref[g:g + 1, :])
        logit_new = jnp.sum(q * k_new, axis=1, keepdims=True) * scale
        logit_new = jnp.where(sel_new, logit_new, -jnp.inf)
        online_update(logit_new, lambda pr: pr * v_new)
        y_ref[...] = (acc_sc[...] / l_sc[...]).astype(BF16)


def _dsa_sample_attn(page_table, scores, q, qi, wi, ki_new, k_new, v_new, cache_k, cache_v, layer_j):
    dec_batch, n_pages = page_table.shape
    n_sel = min(TOPK_MAX, (n_pages * PAGE_SIZE + 1) // 4)
    kv_w = ATT_KV_HEADS * ATT_DH
    per_b = lambda b, p, pt: (b, 0, 0)
    page = lambda b, p, pt: (layer_j, pt[b * n_pages + p], 0, 0)
    grid_spec = pltpu.PrefetchScalarGridSpec(
        num_scalar_prefetch=1,
        grid=(dec_batch, n_pages),
        in_specs=[
            pl.BlockSpec((None, n_pages, PAGE_SIZE), per_b),
            pl.BlockSpec((None, ATT_HEADS, ATT_DH), per_b),
            pl.BlockSpec((None, IDX_HEADS, IDX_DIM), per_b),
            pl.BlockSpec((None, IDX_HEADS, 1), per_b),
            pl.BlockSpec((None, 1, IDX_DIM), per_b),
            pl.BlockSpec((None, ATT_KV_HEADS, ATT_DH), per_b),
            pl.BlockSpec((None, ATT_KV_HEADS, ATT_DH), per_b),
            pl.BlockSpec((None, None, PAGE_SIZE, kv_w), page),
            pl.BlockSpec((None, None, PAGE_SIZE, kv_w), page),
        ],
        out_specs=pl.BlockSpec((None, ATT_HEADS, ATT_DH), per_b),
        scratch_shapes=[
            pltpu.VMEM((1, LANES), jnp.int32),
            pltpu.VMEM((1, LANES), jnp.int32),
            pltpu.VMEM((1, LANES), jnp.int32),
            pltpu.VMEM((ATT_HEADS, 1), F32),
            pltpu.VMEM((ATT_HEADS, 1), F32),
            pltpu.VMEM((ATT_HEADS, ATT_DH), F32),
        ],
    )
    return pl.pallas_call(
        functools.partial(_dsa_sample_attn_kernel, n_sel=n_sel),
        grid_spec=grid_spec,
        out_shape=jax.ShapeDtypeStruct((dec_batch, ATT_HEADS, ATT_DH), BF16),
        compiler_params=_cparams("parallel", "arbitrary"),
        name="dsa_sample_attn",
    )(page_table.reshape(-1), scores, q, qi, wi, ki_new, k_new, v_new, cache_k, cache_v)


def _vec_mat(row, mat):
    return jnp.sum(_row_to_col(row) * mat, axis=0, keepdims=True)


def _mlstm_sample_kernel(x_ref, gate_ref, bias_ref, gain_ref, c_ref, n_ref, m_ref,
                         y_ref, c_out, n_out, m_out):
    gates = gate_ref[...] + bias_ref[...]
    for h in range(MLSTM_HEADS):
        q = x_ref[h:h + 1, :] * (MLSTM_DH ** -0.5)
        k = x_ref[MLSTM_HEADS + h:MLSTM_HEADS + h + 1, :]
        v = x_ref[2 * MLSTM_HEADS + h:2 * MLSTM_HEADS + h + 1, :]
        og = x_ref[3 * MLSTM_HEADS + h:3 * MLSTM_HEADS + h + 1, :]
        ig = gates[h:h + 1, :]
        lf = _log_sigmoid(gates[MLSTM_HEADS + h:MLSTM_HEADS + h + 1, :])
        m_prev = m_ref[h:h + 1, :]
        c_state = c_ref[h]
        n_state = n_ref[h:h + 1, :]
        g = lf + m_prev
        m_t = jnp.maximum(g, ig)
        dw = jnp.exp(ig - m_t)
        gw = jnp.exp(g - m_t)
        s = jnp.sum(q * k, axis=1, keepdims=True) * dw
        num = s * v + gw * _vec_mat(q, c_state)
        den = s + gw * jnp.sum(q * n_state, axis=1, keepdims=True)
        hid = num / jnp.maximum(jnp.abs(den), jnp.exp(-m_t))
        w = jnp.exp(ig - m_t)
        decay = jnp.exp(g - m_t)
        c_out[h] = decay * c_state + _row_to_col(k * w) * v
        n_out[h:h + 1, :] = decay * n_state + w * k
        m_out[h:h + 1, :] = m_t
        y_ref[h:h + 1, :] = (_rms_norm(hid) * gain_ref[h:h + 1, :] * _sigmoid(og)).astype(BF16)


def _mlstm_sample(x, gates, bias, gain, c_state, n_state, m_state, layer_j):
    db = x.shape[0]
    nh, dh = MLSTM_HEADS, MLSTM_DH
    per_b3 = lambda b: (b, 0, 0)
    return pl.pallas_call(
        _mlstm_sample_kernel,
        grid=(db,),
        in_specs=[
            pl.BlockSpec((None, 4 * nh, dh), per_b3),
            pl.BlockSpec((None, 2 * nh, 1), per_b3),
            pl.BlockSpec((2 * nh, 1), lambda b: (0, 0)),
            pl.BlockSpec((None, nh, dh), lambda b: (layer_j, 0, 0)),
            pl.BlockSpec((None, None, nh, dh, dh), lambda b: (layer_j, b, 0, 0, 0)),
            pl.BlockSpec((None, None, nh, dh), lambda b: (layer_j, b, 0, 0)),
            pl.BlockSpec((None, None, nh, 1), lambda b: (layer_j, b, 0, 0)),
        ],
        out_specs=[
            pl.BlockSpec((None, nh, dh), per_b3),
            pl.BlockSpec((None, nh, dh, dh), lambda b: (b, 0, 0, 0)),
            pl.BlockSpec((None, nh, dh), per_b3),
            pl.BlockSpec((None, nh, 1), per_b3),
        ],
        out_shape=[
            jax.ShapeDtypeStruct((db, nh, dh), BF16),
            jax.ShapeDtypeStruct((db, nh, dh, dh), F32),
            jax.ShapeDtypeStruct((db, nh, dh), F32),
            jax.ShapeDtypeStruct((db, nh, 1), F32),
        ],
        compiler_params=_cparams("parallel"),
        name="mlstm_sample",
    )(x, gates, bias, gain, c_state, n_state, m_state)


def _odd_sample_kernel(xh_ref, xr_ref, lb_ref, cos_ref, sin_ref, gh_ref, gr_ref, sh_ref, sr_ref,
                       yh_ref, yr_ref, sh_out, sr_out, *, layer):
    nh = HGRN_HEADS
    lb = _hgrn_lower_bound(lb_ref, layer)
    f = lb + (1.0 - lb) * _sigmoid(xh_ref[nh:2 * nh, :])
    q = _silu(xh_ref[0:nh, :])
    kk = 1.0 - f
    v = xh_ref[2 * nh:3 * nh, :]
    gate = xh_ref[3 * nh:4 * nh, :]
    qk = jnp.sum(q * kk, axis=1, keepdims=True)
    for h in range(nh):
        state = sh_ref[h]
        f_col = _row_to_col(f[h:h + 1, :])
        k_col = _row_to_col(kk[h:h + 1, :])
        v_row = v[h:h + 1, :]
        o = qk[h:h + 1, :] * v_row + _vec_mat(q[h:h + 1, :] * f[h:h + 1, :], state)
        sh_out[h] = f_col * state + k_col * v_row
        yh_ref[h:h + 1, :] = (_rms_norm(o) * gh_ref[h:h + 1, :] * _silu(gate[h:h + 1, :])).astype(BF16)

    nr = RET_HEADS
    cos, sin = cos_ref[...], sin_ref[...]
    qr = _rotate(xr_ref[0:nr, :], cos, sin)
    kr = _rotate(xr_ref[nr:2 * nr, :], cos, sin) * (RET_DK ** -0.5)
    vr = xr_ref[2 * nr:3 * nr, :]
    gr = xr_ref[3 * nr:4 * nr, :]
    qkr = jnp.sum(qr * kr, axis=1, keepdims=True)
    for h in range(nr):
        gamma = jnp.exp(_ret_log_gamma(h))
        state = sr_ref[h]
        v_row = vr[h:h + 1, :]
        o = qkr[h:h + 1, :] * v_row + _vec_mat(qr[h:h + 1, :], state) * gamma
        sr_out[h] = gamma * state + _row_to_col(kr[h:h + 1, :]) * v_row
        yr_ref[h:h + 1, :] = (_rms_norm(o) * gr_ref[h:h + 1, :] * _silu(gr[h:h + 1, :])).astype(BF16)


def _odd_sample(xh, xr, lb3, cos, sin, gain_h, gain_r, state_h, state_r, layer, layer_j):
    db = xh.shape[0]
    per_b3 = lambda b: (b, 0, 0)
    return pl.pallas_call(
        functools.partial(_odd_sample_kernel, layer=layer),
        grid=(db,),
        in_specs=[
            pl.BlockSpec((None, 4 * HGRN_HEADS, HGRN_DK), per_b3),
            pl.BlockSpec((None, 4 * RET_HEADS, RET_DK), per_b3),
            pl.BlockSpec((DEPTH, HGRN_HEADS, HGRN_DK), lambda b: (0, 0, 0)),
            pl.BlockSpec((1, RET_DK // 2), lambda b: (0, 0)),
            pl.BlockSpec((1, RET_DK // 2), lambda b: (0, 0)),
            pl.BlockSpec((None, HGRN_HEADS, HGRN_DV), lambda b: (layer_j, 0, 0)),
            pl.BlockSpec((None, RET_HEADS, RET_DK), lambda b: (layer_j, 0, 0)),
            pl.BlockSpec((None, None, HGRN_HEADS, HGRN_DK, HGRN_DV), lambda b: (layer_j, b, 0, 0, 0)),
            pl.BlockSpec((None, None, RET_HEADS, RET_DK, RET_DK), lambda b: (layer_j, b, 0, 0, 0)),
        ],
        out_specs=[
            pl.BlockSpec((None, HGRN_HEADS, HGRN_DV), per_b3),
            pl.BlockSpec((None, RET_HEADS, RET_DK), per_b3),
            pl.BlockSpec((None, HGRN_HEADS, HGRN_DK, HGRN_DV), lambda b: (b, 0, 0, 0)),
            pl.BlockSpec((None, RET_HEADS, RET_DK, RET_DK), lambda b: (b, 0, 0, 0)),
        ],
        out_shape=[
            jax.ShapeDtypeStruct((db, HGRN_HEADS, HGRN_DV), BF16),
            jax.ShapeDtypeStruct((db, RET_HEADS, RET_DK), BF16),
            jax.ShapeDtypeStruct((db, HGRN_HEADS, HGRN_DK, HGRN_DV), F32),
            jax.ShapeDtypeStruct((db, RET_HEADS, RET_DK, RET_DK), F32),
        ],
        compiler_params=_cparams("parallel"),
        name="odd_sample",
    )(xh, xr, lb3, cos, sin, gain_h, gain_r, state_h, state_r)


def _repack_even(w):
    gates0 = 4 * HALF
    qb0 = gates0 + 2 * MLSTM_HEADS
    ki0 = qb0 + (EV_MISC - EV_QB)
    wi0 = ki0 + IDX_DIM
    pad = jnp.zeros(w.shape[:-1] + (EV_WIDTH - EV_MISC - IDX_DIM - IDX_HEADS - 2 * MLSTM_HEADS,), w.dtype)
    return jnp.concatenate(
        [w[..., :gates0], w[..., qb0:ki0], w[..., ki0:wi0], w[..., wi0:wi0 + IDX_HEADS],
         w[..., gates0:qb0], pad], axis=-1)


def _rot_tables(pos):
    inv = 1.0 / (RET_THETA ** jnp.linspace(0.0, 1.0, RET_DK // 2, dtype=F32))
    ang = pos.astype(F32)[:, None] * inv[None, :]
    return jnp.cos(ang), jnp.sin(ang)


def kernel(x_prompt, x_sample, state_mlstm_C, state_mlstm_n, state_mlstm_m, cache_k, cache_v, cache_idx_k,
           state_hgrn, state_ret, page_table, p_prompt, p_sample, ln_g, ln_b, w_ffn_up, w_ffn_down,
           w_in_even, b_gate_mlstm, g_mlstm, w_in_odd, hgrn_lb, g_hgrn, g_ret, w_out, w_pe_gate, w_pe_proj):
    batch, seq, _ = x_prompt.shape
    db, dec_seq, _ = x_sample.shape
    assert dec_seq == 1 and seq % Q_BLOCK == 0 and seq % CHUNK == 0
    n_even = w_in_even.shape[0]
    n_odd = w_in_odd.shape[0]
    n_pages = page_table.shape[1]
    past = n_pages * PAGE_SIZE
    mp = batch * seq

    w_up = w_ffn_up.astype(BF16)
    w_down = w_ffn_down.astype(BF16)
    w_even = _repack_even(w_in_even).astype(BF16)
    w_odd = w_in_odd.astype(BF16)
    w_o = w_out.astype(BF16)
    w_pg = w_pe_gate.astype(BF16)
    w_pp = w_pe_proj.astype(BF16)
    ln_g3 = ln_g.reshape(DEPTH * 3, 1, D_MODEL)
    ln_b3 = ln_b.reshape(DEPTH * 3, 1, D_MODEL)
    pp = p_prompt.reshape(DEPTH, mp, P_DIM)
    ps = p_sample.reshape(DEPTH, db, P_DIM)
    cos_p, sin_p = _rot_tables(jnp.arange(seq))
    cos_s, sin_s = _rot_tables(past + jnp.arange(dec_seq))
    lb3 = hgrn_lb.reshape(DEPTH, 1, HGRN_HEADS * HGRN_DK)
    lb_heads = hgrn_lb.reshape(DEPTH, HGRN_HEADS, HGRN_DK)
    bias_row = jnp.zeros((n_even, 1, LANES), F32).at[:, 0, MISC_IG:MISC_IG + 2 * MLSTM_HEADS].set(b_gate_mlstm)
    bias_col = b_gate_mlstm.reshape(n_even, 2 * MLSTM_HEADS, 1)
    gain_m3 = g_mlstm.reshape(n_even, 1, HALF)
    gain_h3 = g_hgrn.reshape(n_odd, 1, HALF)
    gain_r3 = g_ret.reshape(n_odd, 1, HALF)
    cache_k3 = cache_k.reshape(cache_k.shape[:3] + (ATT_KV_HEADS * ATT_DH,))
    cache_v3 = cache_v.reshape(cache_v.shape[:3] + (ATT_KV_HEADS * ATT_DH,))
    m_state = state_mlstm_m.reshape(n_even, db, MLSTM_HEADS, 1)

    xp = x_prompt.reshape(mp, D_MODEL)
    xs = x_sample.reshape(db, D_MODEL)
    out_even = {k: [] for k in ("C_p", "C_s", "n_p", "n_s", "m_p", "m_s", "k_p", "k_s", "v_p", "v_s", "ik_p", "ik_s")}
    out_odd = {k: [] for k in ("h_p", "h_s", "r_p", "r_s")}

    for layer in range(DEPTH):
        j = layer // 2
        xp = _ffn(xp, w_up, w_down, ln_g3, ln_b3, layer, 0, 0)
        xs = _ffn(xs, w_up, w_down, ln_g3, ln_b3, layer, 0, 0)
        if layer % 2 == 0:
            zp = _proj(xp, w_even, j)
            zs = _proj(xs, w_even, j)
            ya_p, c_p, n_p, m_p = _mlstm_prompt(zp, bias_row[j], gain_m3, batch, seq, j)
            yb_p = _dsa_prompt(zp, batch, seq)

            x4 = zs[:, :4 * HALF].reshape(db, 4 * MLSTM_HEADS, MLSTM_DH)
            gates_s = zs[:, EV_MISC + MISC_IG:EV_MISC + MISC_IG + 2 * MLSTM_HEADS].reshape(db, 2 * MLSTM_HEADS, 1)
            ya_s, c_s, n_s, m_s = _mlstm_sample(x4, gates_s, bias_col[j], g_mlstm.reshape(n_even, MLSTM_HEADS, MLSTM_DH),
                                                state_mlstm_C, state_mlstm_n, m_state, j)
            q_s = zs[:, EV_QB:EV_KB].reshape(db, ATT_HEADS, ATT_DH)
            k_s = zs[:, EV_KB:EV_VB].reshape(db, ATT_KV_HEADS, ATT_DH)
            v_s = zs[:, EV_VB:EV_QI].reshape(db, ATT_KV_HEADS, ATT_DH)
            qi_s = zs[:, EV_QI:EV_MISC].reshape(db, IDX_HEADS, IDX_DIM)
            ki_s = zs[:, EV_MISC + MISC_KI:EV_MISC + MISC_KI + IDX_DIM].reshape(db, 1, IDX_DIM)
            wi_s = zs[:, EV_MISC + MISC_WI:EV_MISC + MISC_WI + IDX_HEADS].reshape(db, IDX_HEADS, 1)
            scores = _dsa_sample_scores(page_table, cache_idx_k, qi_s, wi_s, j)
            yb_s = _dsa_sample_attn(page_table, scores, q_s, qi_s, wi_s, ki_s, k_s, v_s, cache_k3, cache_v3, j)
            ya_s = ya_s.reshape(db, HALF)
            yb_s = yb_s.reshape(db, HALF)

            out_even["C_p"].append(c_p)
            out_even["C_s"].append(c_s)
            out_even["n_p"].append(n_p.reshape(batch, MLSTM_HEADS, MLSTM_DH))
            out_even["n_s"].append(n_s)
            out_even["m_p"].append(m_p[:, :, 0, 0])
            out_even["m_s"].append(m_s[:, :, 0])
            out_even["k_p"].append(zp[:, EV_KB:EV_VB].reshape(batch, seq, ATT_KV_HEADS, ATT_DH))
            out_even["k_s"].append(k_s.reshape(db, dec_seq, ATT_KV_HEADS, ATT_DH))
            out_even["v_p"].append(zp[:, EV_VB:EV_QI].reshape(batch, seq, ATT_KV_HEADS, ATT_DH))
            out_even["v_s"].append(v_s.reshape(db, dec_seq, ATT_KV_HEADS, ATT_DH))
            out_even["ik_p"].append(zp[:, EV_MISC + MISC_KI:EV_MISC + MISC_KI + IDX_DIM].reshape(batch, seq, IDX_DIM))
            out_even["ik_s"].append(ki_s.reshape(db, dec_seq, IDX_DIM))
        else:
            zp = _proj(xp, w_odd, j)
            zs = _proj(xs, w_odd, j)
            ya_p, h_p = _hgrn_prompt(zp, lb3, gain_h3, batch, seq, layer, j)
            yb_p, r_p = _ret_prompt(zp, cos_p, sin_p, gain_r3, batch, seq, j)
            xh = zs[:, :4 * HALF].reshape(db, 4 * HGRN_HEADS, HGRN_DK)
            xr = zs[:, 4 * HALF:].reshape(db, 4 * RET_HEADS, RET_DK)
            ya_s, yb_s, h_s, r_s = _odd_sample(
                xh, xr, lb_heads, cos_s, sin_s, g_hgrn.reshape(n_odd, HGRN_HEADS, HGRN_DV),
                g_ret.reshape(n_odd, RET_HEADS, RET_DK), state_hgrn, state_ret, layer, j)
            ya_s = ya_s.reshape(db, HALF)
            yb_s = yb_s.reshape(db, HALF)
            out_odd["h_p"].append(h_p)
            out_odd["h_s"].append(h_s)
            out_odd["r_p"].append(r_p)
            out_odd["r_s"].append(r_s)

        xp = _outproj(xp, ya_p, yb_p, w_o, ln_g3, ln_b3, layer)
        xs = _outproj(xs, ya_s, yb_s, w_o, ln_g3, ln_b3, layer)
        xp = _ffn(xp, w_up, w_down, ln_g3, ln_b3, layer, 1, 2)
        xs = _ffn(xs, w_up, w_down, ln_g3, ln_b3, layer, 1, 2)
        xp = _pe(xp, pp, w_pg, w_pp, layer)
        xs = _pe(xs, ps, w_pg, w_pp, layer)

    def stk(name, table, like):
        return jnp.stack(table[name]).astype(like.dtype)

    return (
        xp.reshape(batch, seq, D_MODEL), xs.reshape(db, dec_seq, D_MODEL),
        stk("C_p", out_even, state_mlstm_C), stk("C_s", out_even, state_mlstm_C),
        stk("n_p", out_even, state_mlstm_n), stk("n_s", out_even, state_mlstm_n),
        stk("m_p", out_even, state_mlstm_m), stk("m_s", out_even, state_mlstm_m),
        stk("k_p", out_even, cache_k), stk("k_s", out_even, cache_k),
        stk("v_p", out_even, cache_v), stk("v_s", out_even, cache_v),
        stk("ik_p", out_even, cache_idx_k), stk("ik_s", out_even, cache_idx_k),
        stk("h_p", out_odd, state_hgrn), stk("h_s", out_odd, state_hgrn),
        stk("r_p", out_odd, state_ret), stk("r_s", out_odd, state_ret),
    )
```

```python
import functools
import math

import jax
import jax.numpy as jnp
from jax import lax
from jax.experimental import pallas as pl
from jax.experimental.pallas import tpu as pltpu

F32 = jnp.float32
BF16 = jnp.bfloat16

D_MODEL = 2048
DEPTH = 2
PAGE_SIZE = 128
HALF = D_MODEL // 2
MLSTM_HEADS = 4
MLSTM_DH = HALF // MLSTM_HEADS
ATT_HEADS = 8
ATT_DH = HALF // ATT_HEADS
ATT_KV_HEADS = 2
ATT_GROUP = ATT_HEADS // ATT_KV_HEADS
IDX_HEADS = 8
IDX_DIM = 64
TOPK_MAX = 256
Q_BLOCK = 128
HGRN_HEADS = 8
HGRN_DK = 128
HGRN_DV = HALF // HGRN_HEADS
RET_HEADS = 4
RET_DK = HALF // RET_HEADS
RET_THETA = 10000.0
D_FF = 5632
P_DIM = 256
ALPHA = (2 * DEPTH) ** 0.25
LN_EPS = 1e-5

EV_QA, EV_KA, EV_VA, EV_OA = 0, HALF, 2 * HALF, 3 * HALF
EV_QB = 4 * HALF
EV_KB = EV_QB + ATT_HEADS * ATT_DH
EV_VB = EV_KB + ATT_KV_HEADS * ATT_DH
EV_QI = EV_VB + ATT_KV_HEADS * ATT_DH
EV_MISC = EV_QI + IDX_HEADS * IDX_DIM
LANES = 128
EV_WIDTH = EV_MISC + LANES
MISC_KI, MISC_WI = 0, IDX_DIM
MISC_IG = MISC_WI + IDX_HEADS
MISC_FG = MISC_IG + MLSTM_HEADS

CHUNK = 64
HGRN_SUB = 16
HGRN_GROUP = 4
DSA_WIDTH_STEP = 512
PAGE_GROUP = 8
VMEM_LIMIT = 56 * 1024 * 1024
INT_MIN = -2 ** 31
NEG_BIG = -1e30

NT_DIMS = (((1,), (1,)), ((), ()))
TN_DIMS = (((0,), (0,)), ((), ()))


def _cparams(*sem):
    return pltpu.CompilerParams(dimension_semantics=sem, vmem_limit_bytes=VMEM_LIMIT)


def _row_tile(m, largest=512):
    for t in (1024, 512, 256, 128, 64, 32, 16, 8):
        if t <= largest and m % t == 0:
            return t
    raise ValueError(f"row count {m} is not a multiple of 8")


def _col_tile(n):
    for t in (1024, 896, 512, 256, 128):
        if n % t == 0:
            return t
    raise ValueError(f"column count {n} is not a multiple of 128")


def _layer_norm(y, g, b):
    mu = jnp.mean(y, -1, keepdims=True)
    d = y - mu
    var = jnp.mean(d * d, -1, keepdims=True)
    return d * lax.rsqrt(var + LN_EPS) * g + b


def _rms_norm(h):
    return h * lax.rsqrt(jnp.mean(h * h, -1, keepdims=True) + LN_EPS)


def _sigmoid(x):
    return jax.nn.sigmoid(x)


def _silu(x):
    return x * jax.nn.sigmoid(x)


def _log_sigmoid(x):
    return jnp.minimum(x, 0.0) - jnp.log1p(jnp.exp(-jnp.abs(x)))


def _iota(shape, dim):
    return lax.broadcasted_iota(jnp.int32, shape, dim)


def _row_to_col(row):
    n = row.shape[1]
    eye = _iota((n, n), 0) == _iota((n, n), 1)
    return jnp.sum(jnp.where(eye, row, 0.0), axis=1, keepdims=True)


def _bdot(a, b):
    return jnp.dot(a.astype(BF16), b.astype(BF16), preferred_element_type=F32)


def _bdot_nt(a, b):
    return lax.dot_general(a.astype(BF16), b.astype(BF16), NT_DIMS, preferred_element_type=F32)


def _bdot_tn(a, b):
    return lax.dot_general(a.astype(BF16), b.astype(BF16), TN_DIMS, preferred_element_type=F32)


def _ffn_kernel(x_ref, wg_ref, wu_ref, wd_ref, g_ref, b_ref, o_ref, ob_ref, xb_ref, acc_ref):
    j = pl.program_id(1)

    @pl.when(j == 0)
    def _():
        xb_ref[...] = x_ref[...].astype(BF16)
        acc_ref[...] = jnp.zeros_like(acc_ref)

    xb = xb_ref[...]
    hg = jnp.dot(xb, wg_ref[...], preferred_element_type=F32)
    hu = jnp.dot(xb, wu_ref[...], preferred_element_type=F32)
    act = _silu(hg) * hu
    acc_ref[...] += jnp.dot(act.astype(BF16), wd_ref[...], preferred_element_type=F32)

    @pl.when(j == pl.num_programs(1) - 1)
    def _():
        y = _layer_norm(ALPHA * x_ref[...] + 0.5 * acc_ref[...], g_ref[...], b_ref[...])
        o_ref[...] = y
        ob_ref[...] = y.astype(BF16)


def _ffn(x, w_up, w_down, ln_g, ln_b, layer, which, ln_idx):
    m = x.shape[0]
    tm, tf = _row_tile(m), 512
    nf = D_FF // tf
    ln_row = layer * 3 + ln_idx
    return pl.pallas_call(
        _ffn_kernel,
        grid=(m // tm, nf),
        in_specs=[
            pl.BlockSpec((tm, D_MODEL), lambda i, j: (i, 0)),
            pl.BlockSpec((None, None, D_MODEL, tf), lambda i, j: (layer, which, 0, j)),
            pl.BlockSpec((None, None, D_MODEL, tf), lambda i, j: (layer, which, 0, j + nf)),
            pl.BlockSpec((None, None, tf, D_MODEL), lambda i, j: (layer, which, j, 0)),
            pl.BlockSpec((None, 1, D_MODEL), lambda i, j: (ln_row, 0, 0)),
            pl.BlockSpec((None, 1, D_MODEL), lambda i, j: (ln_row, 0, 0)),
        ],
        out_specs=[pl.BlockSpec((tm, D_MODEL), lambda i, j: (i, 0))] * 2,
        out_shape=[jax.ShapeDtypeStruct((m, D_MODEL), F32), jax.ShapeDtypeStruct((m, D_MODEL), BF16)],
        scratch_shapes=[pltpu.VMEM((tm, D_MODEL), BF16), pltpu.VMEM((tm, D_MODEL), F32)],
        compiler_params=_cparams("parallel", "arbitrary"),
        name="ffn",
    )(x, w_up, w_up, w_down, ln_g, ln_b)


def _proj_kernel(x_ref, w_ref, o_ref, wb_ref):
    @pl.when(pl.program_id(1) == 0)
    def _():
        wb_ref[...] = w_ref[...].astype(BF16)

    o_ref[...] = jnp.dot(x_ref[...], wb_ref[...], preferred_element_type=F32)


def _proj(xb, w, idx):
    m, n = xb.shape[0], w.shape[2]
    tm, tn = _row_tile(m, 1024), _col_tile(n)
    return pl.pallas_call(
        _proj_kernel,
        grid=(n // tn, m // tm),
        in_specs=[
            pl.BlockSpec((tm, D_MODEL), lambda j, i: (i, 0)),
            pl.BlockSpec((None, D_MODEL, tn), lambda j, i: (idx, 0, j)),
        ],
        out_specs=pl.BlockSpec((tm, tn), lambda j, i: (i, j)),
        out_shape=jax.ShapeDtypeStruct((m, n), F32),
        scratch_shapes=[pltpu.VMEM((D_MODEL, tn), BF16)],
        compiler_params=_cparams("parallel", "arbitrary"),
        name="in_proj",
    )(xb, w)


def _outproj_kernel(x_ref, ya_ref, yb_ref, wa_ref, wb_ref, g_ref, b_ref, o_ref):
    y = jnp.dot(ya_ref[...], wa_ref[...], preferred_element_type=F32)
    y += jnp.dot(yb_ref[...], wb_ref[...], preferred_element_type=F32)
    o_ref[...] = _layer_norm(ALPHA * x_ref[...] + y, g_ref[...], b_ref[...])


def _outproj(x, ya, yb, w_out, ln_g, ln_b, layer):
    m = x.shape[0]
    tm = _row_tile(m)
    ln_row = layer * 3 + 1
    return pl.pallas_call(
        _outproj_kernel,
        grid=(m // tm,),
        in_specs=[
            pl.BlockSpec((tm, D_MODEL), lambda i: (i, 0)),
            pl.BlockSpec((tm, HALF), lambda i: (i, 0)),
            pl.BlockSpec((tm, HALF), lambda i: (i, 0)),
            pl.BlockSpec((None, HALF, D_MODEL), lambda i: (layer, 0, 0)),
            pl.BlockSpec((None, HALF, D_MODEL), lambda i: (layer, 1, 0)),
            pl.BlockSpec((None, 1, D_MODEL), lambda i: (ln_row, 0, 0)),
            pl.BlockSpec((None, 1, D_MODEL), lambda i: (ln_row, 0, 0)),
        ],
        out_specs=pl.BlockSpec((tm, D_MODEL), lambda i: (i, 0)),
        out_shape=jax.ShapeDtypeStruct((m, D_MODEL), F32),
        compiler_params=_cparams("parallel"),
        name="out_proj",
    )(x, ya, yb, w_out, w_out, ln_g, ln_b)


def _pe_kernel(x_ref, xb_ref, p_ref, wg_ref, wp_ref, o_ref):
    gate = _sigmoid(jnp.dot(xb_ref[...], wg_ref[...], preferred_element_type=F32))
    o_ref[...] = x_ref[...] + gate * jnp.dot(p_ref[...].astype(BF16), wp_ref[...], preferred_element_type=F32)


def _pe(x, xb, p, w_gate, w_proj, layer):
    m = x.shape[0]
    tm = _row_tile(m)
    return pl.pallas_call(
        _pe_kernel,
        grid=(m // tm,),
        in_specs=[
            pl.BlockSpec((tm, D_MODEL), lambda i: (i, 0)),
            pl.BlockSpec((tm, D_MODEL), lambda i: (i, 0)),
            pl.BlockSpec((None, tm, P_DIM), lambda i: (layer, i, 0)),
            pl.BlockSpec((None, D_MODEL, D_MODEL), lambda i: (layer, 0, 0)),
            pl.BlockSpec((None, P_DIM, D_MODEL), lambda i: (layer, 0, 0)),
        ],
        out_specs=pl.BlockSpec((tm, D_MODEL), lambda i: (i, 0)),
        out_shape=jax.ShapeDtypeStruct((m, D_MODEL), F32),
        compiler_params=_cparams("parallel"),
        name="pe_gate",
    )(x, xb, p, w_gate, w_proj)


def _mlstm_prompt_kernel(q_ref, k_ref, v_ref, o_ref, misc_ref, bias_ref, gain_ref,
                         y_ref, c_ref, n_ref, m_ref, m_sc):
    chunk = q_ref.shape[0]
    dh = MLSTM_DH

    @pl.when(pl.program_id(1) == 0)
    def _():
        c_ref[...] = jnp.zeros_like(c_ref)
        n_ref[...] = jnp.zeros_like(n_ref)
        m_sc[...] = jnp.zeros_like(m_sc)

    gates = misc_ref[...] + bias_ref[...]
    ti = _iota((chunk, chunk), 0)
    si = _iota((chunk, chunk), 1)
    eye = ti == si
    causal = si <= ti
    for h in range(MLSTM_HEADS):
        cols = slice(h * dh, (h + 1) * dh)
        ig_col = gates[:, MISC_IG + h:MISC_IG + h + 1]
        lf_col = _log_sigmoid(gates[:, MISC_FG + h:MISC_FG + h + 1])
        lf_row = jnp.sum(jnp.where(eye, lf_col, 0.0), axis=0, keepdims=True)
        ig_row = jnp.sum(jnp.where(eye, ig_col, 0.0), axis=0, keepdims=True)
        b_col = jnp.sum(jnp.where(causal, lf_row, 0.0), axis=1, keepdims=True)
        b_row = jnp.sum(jnp.where(ti <= si, lf_col, 0.0), axis=0, keepdims=True)
        m_prev = m_sc[h][:, :1]
        dlog = jnp.where(causal, b_col - b_row + ig_row, -jnp.inf)
        g_col = b_col + m_prev
        m_t = jnp.maximum(g_col, jnp.max(dlog, axis=1, keepdims=True))
        dw = jnp.exp(dlog - m_t)
        gw = jnp.exp(g_col - m_t)

        q = q_ref[:, cols] * (dh ** -0.5)
        k = k_ref[:, cols]
        v = v_ref[:, cols]
        c_state = c_ref[h]
        n_state = n_ref[h]
        s = _bdot_nt(q, k) * dw
        num = _bdot(s, v) + gw * _bdot(q, c_state)
        den = jnp.sum(s, axis=1, keepdims=True) + gw * jnp.sum(q * n_state, axis=1, keepdims=True)
        hid = num / jnp.maximum(jnp.abs(den), jnp.exp(-m_t))

        b_last = b_col[chunk - 1:chunk, :]
        wlog = b_last - b_col + ig_col
        m_new = jnp.maximum(b_last + m_prev, jnp.max(wlog, axis=0, keepdims=True))
        w_col = jnp.exp(wlog - m_new)
        decay = jnp.exp(b_last + m_prev - m_new)
        kw = k * w_col
        c_ref[h] = decay * c_state + _bdot_tn(kw, v)
        n_ref[h] = decay * n_state + jnp.sum(kw, axis=0, keepdims=True)
        m_row = jnp.broadcast_to(m_new, (1, LANES))
        m_sc[h] = m_row
        m_ref[h] = m_row

        y_ref[:, cols] = (_rms_norm(hid) * gain_ref[:, cols] * _sigmoid(o_ref[:, cols])).astype(BF16)


def _mlstm_prompt(z, bias_row, gain, batch, seq, layer_j):
    nc = seq // CHUNK
    nh, dh = MLSTM_HEADS, MLSTM_DH

    def col(base):
        return lambda b, c: (b * nc + c, base // HALF)

    state = lambda b, c: (b, 0, 0, 0)
    return pl.pallas_call(
        _mlstm_prompt_kernel,
        grid=(batch, nc),
        in_specs=[
            pl.BlockSpec((CHUNK, HALF), col(EV_QA)),
            pl.BlockSpec((CHUNK, HALF), col(EV_KA)),
            pl.BlockSpec((CHUNK, HALF), col(EV_VA)),
            pl.BlockSpec((CHUNK, HALF), col(EV_OA)),
            pl.BlockSpec((CHUNK, LANES), lambda b, c: (b * nc + c, EV_MISC // LANES)),
            pl.BlockSpec((1, LANES), lambda b, c: (0, 0)),
            pl.BlockSpec((None, 1, HALF), lambda b, c: (layer_j, 0, 0)),
        ],
        out_specs=[
            pl.BlockSpec((CHUNK, HALF), lambda b, c: (b * nc + c, 0)),
            pl.BlockSpec((None, nh, dh, dh), state),
            pl.BlockSpec((None, nh, 1, dh), state),
            pl.BlockSpec((None, nh, 1, LANES), state),
        ],
        out_shape=[
            jax.ShapeDtypeStruct((batch * seq, HALF), BF16),
            jax.ShapeDtypeStruct((batch, nh, dh, dh), F32),
            jax.ShapeDtypeStruct((batch, nh, 1, dh), F32),
            jax.ShapeDtypeStruct((batch, nh, 1, LANES), F32),
        ],
        scratch_shapes=[pltpu.VMEM((nh, 1, LANES), F32)],
        compiler_params=_cparams("parallel", "arbitrary"),
        name="mlstm_prompt",
    )(z, z, z, z, z, bias_row, gain)


def _hgrn_lower_bound(lb_ref, layer):
    rows = [lb_ref[j] for j in range(DEPTH)]
    mx = functools.reduce(jnp.maximum, rows)
    e = [jnp.exp(r - mx) for r in rows]
    total = functools.reduce(jnp.add, e)
    acc = jnp.zeros_like(total)
    for j in range(1, layer + 1):
        acc = acc + e[j] / total
    return acc


def _hgrn_prompt_kernel(q_ref, f_ref, i_ref, g_ref, lb_ref, gain_ref, y_ref, s_ref, *, layer):
    chunk = q_ref.shape[0]

    @pl.when(pl.program_id(2) == 0)
    def _():
        s_ref[...] = jnp.zeros_like(s_ref)

    lb = _hgrn_lower_bound(lb_ref, layer)
    f_all = lb + (1.0 - lb) * _sigmoid(f_ref[...])
    lf = jnp.log(f_all)
    tri = (_iota((chunk, chunk), 1) <= _iota((chunk, chunk), 0)).astype(F32)
    b_all = jnp.dot(tri, lf, preferred_element_type=F32, precision=lax.Precision.HIGHEST)
    t_in_blk = _iota((chunk, 1), 0) % HGRN_SUB

    for h in range(HGRN_GROUP):
        cols = slice(h * HGRN_DK, (h + 1) * HGRN_DK)
        b = b_all[:, cols]
        kk = 1.0 - f_all[:, cols]
        q = _silu(q_ref[:, cols])
        v = i_ref[:, cols]
        b_last = b[chunk - 1:chunk, :]
        state = s_ref[h]

        o = _bdot(q * jnp.exp(b), state)
        parts = []
        for blk in range(chunk // HGRN_SUB):
            r = blk * HGRN_SUB
            o_blk = o[r:r + HGRN_SUB]
            if blk > 0:
                b_ref_row = b[r - 1:r]
                qt = q[r:r + HGRN_SUB] * jnp.exp(b[r:r + HGRN_SUB] - b_ref_row)
                kt = kk[:r] * jnp.exp(b_ref_row - b[:r])
                o_blk = o_blk + _bdot(_bdot_nt(qt, kt), v[:r])
            parts.append(o_blk)
        o = jnp.concatenate(parts, axis=0)

        for dlt in range(HGRN_SUB):
            if dlt == 0:
                k_s, b_s, v_s = kk, b, v
            else:
                k_s, b_s, v_s = (pltpu.roll(a, dlt, 0) for a in (kk, b, v))
            e = jnp.exp(jnp.minimum(b - b_s, 0.0))
            a = jnp.sum(q * k_s * e, axis=1, keepdims=True)
            o = o + jnp.where(t_in_blk >= dlt, a, 0.0) * v_s

        s_ref[h] = _row_to_col(jnp.exp(b_last)) * state + _bdot_tn(kk * jnp.exp(b_last - b), v)
        y_ref[:, cols] = (_rms_norm(o) * gain_ref[:, cols] * _silu(g_ref[:, cols])).astype(BF16)


def _hgrn_prompt(z, lb3, gain, batch, seq, layer, layer_j):
    nc = seq // CHUNK
    n_groups = HGRN_HEADS // HGRN_GROUP
    dk = HGRN_GROUP * HGRN_DK

    def col(group):
        return lambda b, h, c: (b * nc + c, group * n_groups + h)

    return pl.pallas_call(
        functools.partial(_hgrn_prompt_kernel, layer=layer),
        grid=(batch, n_groups, nc),
        in_specs=[
            pl.BlockSpec((CHUNK, dk), col(0)),
            pl.BlockSpec((CHUNK, dk), col(1)),
            pl.BlockSpec((CHUNK, dk), col(2)),
            pl.BlockSpec((CHUNK, dk), col(3)),
            pl.BlockSpec((DEPTH, 1, dk), lambda b, h, c: (0, 0, h)),
            pl.BlockSpec((None, 1, dk), lambda b, h, c: (layer_j, 0, h)),
        ],
        out_specs=[
            pl.BlockSpec((CHUNK, dk), lambda b, h, c: (b * nc + c, h)),
            pl.BlockSpec((None, HGRN_GROUP, HGRN_DK, HGRN_DV), lambda b, h, c: (b, h, 0, 0)),
        ],
        out_shape=[
            jax.ShapeDtypeStruct((batch * seq, HALF), BF16),
            jax.ShapeDtypeStruct((batch, HGRN_HEADS, HGRN_DK, HGRN_DV), F32),
        ],
        compiler_params=_cparams("parallel", "parallel", "arbitrary"),
        name="hgrn_prompt",
    )(z, z, z, z, lb3, gain)


def _ret_log_gamma(h):
    return math.log1p(-2.0 ** (-5 - h))


def _rotate(x, cos, sin):
    half = x.shape[1] // 2
    x1, x2 = x[:, :half], x[:, half:]
    return jnp.concatenate([x1 * cos - x2 * sin, x1 * sin + x2 * cos], axis=1)


def _ret_prompt_kernel(q_ref, k_ref, v_ref, g_ref, cos_ref, sin_ref, gain_ref, y_ref, s_ref):
    chunk = q_ref.shape[0]

    @pl.when(pl.program_id(1) == 0)
    def _():
        s_ref[...] = jnp.zeros_like(s_ref)

    cos, sin = cos_ref[...], sin_ref[...]
    rel = (_iota((chunk, chunk), 0) - _iota((chunk, chunk), 1)).astype(F32)
    pos = _iota((chunk, 1), 0).astype(F32)
    for h in range(RET_HEADS):
        cols = slice(h * RET_DK, (h + 1) * RET_DK)
        lg = _ret_log_gamma(h)
        qr = _rotate(q_ref[:, cols], cos, sin)
        kr = _rotate(k_ref[:, cols], cos, sin) * (RET_DK ** -0.5)
        v = v_ref[:, cols]
        dmask = jnp.where(rel >= 0.0, jnp.exp(lg * jnp.maximum(rel, 0.0)), 0.0)
        state = s_ref[h]
        o = _bdot(_bdot_nt(qr, kr) * dmask, v) + _bdot(qr, state) * jnp.exp(lg * (pos + 1.0))
        tail = jnp.exp(lg * (chunk - 1.0 - pos))
        s_ref[h] = math.exp(lg * chunk) * state + _bdot_tn(kr * tail, v)
        y_ref[:, cols] = (_rms_norm(o) * gain_ref[:, cols] * _silu(g_ref[:, cols])).astype(BF16)


def _ret_prompt(z, cos, sin, gain, batch, seq, layer_j):
    nc = seq // CHUNK
    dk = RET_DK

    def col(group):
        return lambda b, c: (b * nc + c, 4 + group)

    return pl.pallas_call(
        _ret_prompt_kernel,
        grid=(batch, nc),
        in_specs=[
            pl.BlockSpec((CHUNK, HALF), col(0)),
            pl.BlockSpec((CHUNK, HALF), col(1)),
            pl.BlockSpec((CHUNK, HALF), col(2)),
            pl.BlockSpec((CHUNK, HALF), col(3)),
            pl.BlockSpec((CHUNK, dk // 2), lambda b, c: (c, 0)),
            pl.BlockSpec((CHUNK, dk // 2), lambda b, c: (c, 0)),
            pl.BlockSpec((None, 1, HALF), lambda b, c: (layer_j, 0, 0)),
        ],
        out_specs=[
            pl.BlockSpec((CHUNK, HALF), lambda b, c: (b * nc + c, 0)),
            pl.BlockSpec((None, RET_HEADS, dk, dk), lambda b, c: (b, 0, 0, 0)),
        ],
        out_shape=[
            jax.ShapeDtypeStruct((batch * seq, HALF), BF16),
            jax.ShapeDtypeStruct((batch, RET_HEADS, dk, dk), F32),
        ],
        compiler_params=_cparams("parallel", "arbitrary"),
        name="ret_prompt",
    )(z, z, z, z, cos, sin, gain)


def _score_key(score):
    bits = lax.bitcast_convert_type(score, jnp.int32)
    return jnp.where(bits < 0, -(bits & 0x7FFFFFFF), bits)


def _count(mask):
    return jnp.sum(jnp.where(mask, 1.0, 0.0), axis=1, keepdims=True)


def _kth_largest_key(count_ge, rows, n_sel):
    def body(it, t_off):
        cand = t_off | lax.shift_left(jnp.int32(1), 31 - it)
        return jnp.where(count_ge(cand ^ INT_MIN) >= n_sel, cand, t_off)

    t_off = lax.fori_loop(0, 32, body, jnp.zeros((rows, 1), jnp.int32))
    return t_off ^ INT_MIN


def _tie_cutoff(count_eq_below, rows, budget, n_bits):
    def body(it, cut):
        cand = cut | lax.shift_left(jnp.int32(1), n_bits - 1 - it)
        return jnp.where(count_eq_below(cand) <= budget, cand, cut)

    return lax.fori_loop(0, n_bits, body, jnp.zeros((rows, 1), jnp.int32))


def _select_top(key, valid, kpos, n_sel, n_idx_bits):
    rows = key.shape[0]
    thr = _kth_largest_key(lambda t: _count(key >= t), rows, n_sel)
    above = key > thr
    tied = key == thr
    n_above = _count(above)
    need_cut = jnp.max(n_above + _count(tied & valid)) > n_sel
    cut = lax.cond(
        need_cut,
        lambda: _tie_cutoff(lambda j: _count(tied & (kpos < j)), rows, n_sel - n_above, n_idx_bits),
        lambda: jnp.full((rows, 1), 2 ** 31 - 1, jnp.int32))
    return valid & (above | (tied & (kpos < cut)))


def _dsa_prompt_body(q_ref, k_ref, v_ref, qi_ref, qmisc_ref, kmisc_ref, y_ref, n_sel, width):
    qblk = q_ref.shape[0]
    ki = kmisc_ref[:width, MISC_KI:MISC_KI + IDX_DIM].astype(BF16)
    qmisc = qmisc_ref[...]
    score = jnp.zeros((qblk, width), F32)
    for h in range(IDX_HEADS):
        w = qmisc[:, MISC_WI + h:MISC_WI + h + 1] * (IDX_HEADS ** -0.5) * (IDX_DIM ** -0.5)
        score = score + jnp.maximum(_bdot_nt(qi_ref[:, h * IDX_DIM:(h + 1) * IDX_DIM], ki), 0.0) * w

    qpos = pl.program_id(1) * qblk + _iota((qblk, width), 0)
    kpos = _iota((qblk, width), 1)
    valid = kpos <= qpos
    key = jnp.where(valid, _score_key(score), INT_MIN)
    sel = _select_top(key, valid, kpos, n_sel, (width + 1).bit_length())
    bias = jnp.where(sel, 0.0, -jnp.inf)

    exp2_scale = (ATT_DH ** -0.5) * math.log2(math.e)
    for g in range(ATT_KV_HEADS):
        kg = k_ref[:width, g * ATT_DH:(g + 1) * ATT_DH].astype(BF16)
        vg = v_ref[:width, g * ATT_DH:(g + 1) * ATT_DH].astype(BF16)
        for hh in range(ATT_GROUP):
            cols = slice((g * ATT_GROUP + hh) * ATT_DH, (g * ATT_GROUP + hh + 1) * ATT_DH)
            logits = _bdot_nt(q_ref[:, cols], kg) + bias
            e = jnp.exp2((logits - jnp.max(logits, axis=1, keepdims=True)) * exp2_scale)
            y_ref[:, cols] = (_bdot(e, vg) / jnp.sum(e, axis=1, keepdims=True)).astype(BF16)


def _dsa_prompt_kernel(q_ref, k_ref, v_ref, qi_ref, qmisc_ref, kmisc_ref, y_ref, *, n_sel):
    qblk, seq = q_ref.shape[0], k_ref.shape[0]
    step = DSA_WIDTH_STEP if seq % DSA_WIDTH_STEP == 0 else seq
    blocks_per_step = step // qblk
    i = pl.program_id(1)
    for var in range(seq // step):
        @pl.when((i >= var * blocks_per_step) & (i < (var + 1) * blocks_per_step))
        def _(width=(var + 1) * step):
            _dsa_prompt_body(q_ref, k_ref, v_ref, qi_ref, qmisc_ref, kmisc_ref, y_ref, n_sel, width)


def _dsa_prompt(z, batch, seq):
    nb = seq // Q_BLOCK
    n_sel = min(TOPK_MAX, seq // 4)
    kv_w = ATT_KV_HEADS * ATT_DH
    qi_w = IDX_HEADS * IDX_DIM
    return pl.pallas_call(
        functools.partial(_dsa_prompt_kernel, n_sel=n_sel),
        grid=(batch, nb),
        in_specs=[
            pl.BlockSpec((Q_BLOCK, HALF), lambda b, i: (b * nb + i, EV_QB // HALF)),
            pl.BlockSpec((seq, kv_w), lambda b, i: (b, EV_KB // kv_w)),
            pl.BlockSpec((seq, kv_w), lambda b, i: (b, EV_VB // kv_w)),
            pl.BlockSpec((Q_BLOCK, qi_w), lambda b, i: (b * nb + i, EV_QI // qi_w)),
            pl.BlockSpec((Q_BLOCK, LANES), lambda b, i: (b * nb + i, EV_MISC // LANES)),
            pl.BlockSpec((seq, LANES), lambda b, i: (b, EV_MISC // LANES)),
        ],
        out_specs=pl.BlockSpec((Q_BLOCK, HALF), lambda b, i: (b * nb + i, 0)),
        out_shape=jax.ShapeDtypeStruct((batch * seq, HALF), BF16),
        compiler_params=_cparams("parallel", "arbitrary"),
        name="dsa_prompt",
    )(z, z, z, z, z, z)


def _page_specs(block, n_pages, layer_j):
    def spec(g):
        return pl.BlockSpec((None, None) + block,
                            lambda b, s, pt: (layer_j, pt[b * n_pages + s * PAGE_GROUP + g], 0, 0))

    return [spec(g) for g in range(PAGE_GROUP)]


def _dsa_sample_score_kernel(pt_ref, *refs):
    ik_refs = refs[:PAGE_GROUP]
    qi_ref, wi_ref, o_ref = refs[PAGE_GROUP:]
    qi = qi_ref[...]
    w = wi_ref[...] * (IDX_HEADS ** -0.5)
    rows = []
    for ik_ref in ik_refs:
        r = _bdot_nt(qi, ik_ref[...]) * (IDX_DIM ** -0.5)
        rows.append(jnp.sum(jnp.maximum(r, 0.0) * w, axis=0, keepdims=True))
    o_ref[...] = jnp.concatenate(rows, axis=0)


def _dsa_sample_scores(page_table, cache_ik, qi, wi, layer_j):
    dec_batch, n_pages = page_table.shape
    assert n_pages % PAGE_GROUP == 0
    grid_spec = pltpu.PrefetchScalarGridSpec(
        num_scalar_prefetch=1,
        grid=(dec_batch, n_pages // PAGE_GROUP),
        in_specs=_page_specs((PAGE_SIZE, IDX_DIM), n_pages, layer_j) + [
            pl.BlockSpec((None, IDX_HEADS, IDX_DIM), lambda b, s, pt: (b, 0, 0)),
            pl.BlockSpec((None, IDX_HEADS, 1), lambda b, s, pt: (b, 0, 0)),
        ],
        out_specs=pl.BlockSpec((None, PAGE_GROUP, PAGE_SIZE), lambda b, s, pt: (b, s, 0)),
    )
    return pl.pallas_call(
        _dsa_sample_score_kernel,
        grid_spec=grid_spec,
        out_shape=jax.ShapeDtypeStruct((dec_batch, n_pages, PAGE_SIZE), F32),
        compiler_params=_cparams("parallel", "arbitrary"),
        name="dsa_sample_scores",
    )(page_table.reshape(-1), *([cache_ik] * PAGE_GROUP), qi, wi)


def _dsa_sample_attn_kernel(pt_ref, sc_ref, q_ref, qi_ref, wi_ref, kin_ref, knew_ref, vnew_ref, *refs, n_sel):
    kc_refs = refs[:PAGE_GROUP]
    vc_refs = refs[PAGE_GROUP:2 * PAGE_GROUP]
    y_ref, thr_sc, cut_sc, new_sc, m_sc, l_sc, acc_sc = refs[2 * PAGE_GROUP:]
    step = pl.program_id(1)
    past = sc_ref.shape[0] * PAGE_SIZE
    group_w = PAGE_GROUP * PAGE_SIZE * ATT_KV_HEADS

    @pl.when(step == 0)
    def _():
        r_new = jnp.sum(qi_ref[...] * kin_ref[...], axis=1, keepdims=True) * (IDX_DIM ** -0.5)
        s_new = jnp.sum(jnp.maximum(r_new, 0.0) * (wi_ref[...] * (IDX_HEADS ** -0.5)),
                        axis=0, keepdims=True)
        key_new = _score_key(s_new)
        key = _score_key(sc_ref[...])
        idx = _iota(key.shape, 0) * PAGE_SIZE + _iota(key.shape, 1)

        def total(mask, mask_new):
            return (jnp.sum(_count(mask), axis=0, keepdims=True) + jnp.where(mask_new, 1.0, 0.0))

        thr = _kth_largest_key(lambda t: total(key >= t, key_new >= t), 1, n_sel)
        above, tied = key > thr, key == thr
        budget = n_sel - total(above, key_new > thr)
        cut = _tie_cutoff(lambda j: total(tied & (idx < j), (key_new == thr) & (past < j)),
                          1, budget, (past + 2).bit_length())
        thr_sc[...] = jnp.broadcast_to(thr, thr_sc.shape)
        cut_sc[...] = jnp.broadcast_to(cut, cut_sc.shape)
        new_sc[...] = jnp.broadcast_to(key_new, new_sc.shape)
        m_sc[...] = jnp.full_like(m_sc, NEG_BIG)
        l_sc[...] = jnp.zeros_like(l_sc)
        acc_sc[...] = jnp.zeros_like(acc_sc)

    thr = thr_sc[:, :1]
    cut = cut_sc[:, :1]
    scale = ATT_DH ** -0.5
    q = q_ref[...]
    first_group = _iota((ATT_HEADS, 1), 0) < ATT_GROUP

    def by_group(fn):
        return jnp.where(first_group, fn(0), fn(1))

    def online_update(logits, pv):
        m_old = m_sc[...]
        m_new = jnp.maximum(m_old, jnp.max(logits, axis=1, keepdims=True))
        alpha = jnp.exp(m_old - m_new)
        pr = jnp.exp(logits - m_new)
        l_sc[...] = alpha * l_sc[...] + jnp.sum(pr, axis=1, keepdims=True)
        acc_sc[...] = alpha * acc_sc[...] + pv(pr)
        m_sc[...] = m_new

    first_page = pl.multiple_of(step * PAGE_GROUP, PAGE_GROUP)
    key = _score_key(sc_ref[pl.ds(first_page, PAGE_GROUP), :])
    idx = (first_page + _iota(key.shape, 0)) * PAGE_SIZE + _iota(key.shape, 1)
    sel = (key > thr) | ((key == thr) & (idx < cut))
    exp_shape = (PAGE_SIZE, PAGE_SIZE * ATT_KV_HEADS)
    expand = jnp.where(_iota(exp_shape, 1) // ATT_KV_HEADS == _iota(exp_shape, 0), 1.0, 0.0)
    sel_wide = _bdot(jnp.where(sel, 1.0, 0.0), expand)
    sel_row = jnp.concatenate([sel_wide[g:g + 1, :] for g in range(PAGE_GROUP)], axis=1)
    col_head = _iota((ATT_HEADS, group_w), 1) % ATT_KV_HEADS
    row_head = _iota((ATT_HEADS, group_w), 0) // ATT_GROUP
    mask = (sel_row > 0.5) & (col_head == row_head)
    kc = jnp.concatenate([r[...] for r in kc_refs], axis=0)
    vc = jnp.concatenate([r[...] for r in vc_refs], axis=0)
    logits = jnp.where(mask, _bdot_nt(q, kc) * scale, -jnp.inf)
    online_update(logits, lambda pr: _bdot(pr, vc))

    @pl.when(step == pl.num_programs(1) - 1)
    def _():
        key_new = new_sc[:, :1]
        sel_new = (key_new > thr) | ((key_new == thr) & (past < cut))
        k_new = by_group(lambda g: knew_ref[g:g + 1, :])
        v_new = by_group(lambda g: vnew_ref[g:g + 1, :])
        logit_new = jnp.sum(q * k_new, axis=1, keepdims=True) * scale
        logit_new = jnp.where(sel_new, logit_new, -jnp.inf)
        online_update(logit_new, lambda pr: pr * v_new)
        y_ref[...] = (acc_sc[...] / l_sc[...]).astype(BF16)


def _dsa_sample_attn(page_table, scores, q, qi, wi, ki_new, k_new, v_new, cache_k, cache_v, layer_j):
    dec_batch, n_pages = page_table.shape
    n_sel = min(TOPK_MAX, (n_pages * PAGE_SIZE + 1) // 4)
    per_b = lambda b, s, pt: (b, 0, 0)
    page_specs = _page_specs((PAGE_SIZE * ATT_KV_HEADS, ATT_DH), n_pages, layer_j)
    grid_spec = pltpu.PrefetchScalarGridSpec(
        num_scalar_prefetch=1,
        grid=(dec_batch, n_pages // PAGE_GROUP),
        in_specs=[
            pl.BlockSpec((None, n_pages, PAGE_SIZE), per_b),
            pl.BlockSpec((None, ATT_HEADS, ATT_DH), per_b),
            pl.BlockSpec((None, IDX_HEADS, IDX_DIM), per_b),
            pl.BlockSpec((None, IDX_HEADS, 1), per_b),
            pl.BlockSpec((None, 1, IDX_DIM), per_b),
            pl.BlockSpec((None, ATT_KV_HEADS, ATT_DH), per_b),
            pl.BlockSpec((None, ATT_KV_HEADS, ATT_DH), per_b),
        ] + page_specs + page_specs,
        out_specs=pl.BlockSpec((None, ATT_HEADS, ATT_DH), per_b),
        scratch_shapes=[
            pltpu.VMEM((1, LANES), jnp.int32),
            pltpu.VMEM((1, LANES), jnp.int32),
            pltpu.VMEM((1, LANES), jnp.int32),
            pltpu.VMEM((ATT_HEADS, 1), F32),
            pltpu.VMEM((ATT_HEADS, 1), F32),
            pltpu.VMEM((ATT_HEADS, ATT_DH), F32),
        ],
    )
    return pl.pallas_call(
        functools.partial(_dsa_sample_attn_kernel, n_sel=n_sel),
        grid_spec=grid_spec,
        out_shape=jax.ShapeDtypeStruct((dec_batch, ATT_HEADS, ATT_DH), BF16),
        compiler_params=_cparams("parallel", "arbitrary"),
        name="dsa_sample_attn",
    )(page_table.reshape(-1), scores, q, qi, wi, ki_new, k_new, v_new,
      *([cache_k] * PAGE_GROUP), *([cache_v] * PAGE_GROUP))


def _vec_mat(row, mat):
    return jnp.sum(_row_to_col(row) * mat, axis=0, keepdims=True)


def _mlstm_sample_kernel(x_ref, gate_ref, bias_ref, gain_ref, c_ref, n_ref, m_ref,
                         y_ref, c_out, n_out, m_out):
    gates = gate_ref[...] + bias_ref[...]
    for h in range(MLSTM_HEADS):
        q = x_ref[h:h + 1, :] * (MLSTM_DH ** -0.5)
        k = x_ref[MLSTM_HEADS + h:MLSTM_HEADS + h + 1, :]
        v = x_ref[2 * MLSTM_HEADS + h:2 * MLSTM_HEADS + h + 1, :]
        og = x_ref[3 * MLSTM_HEADS + h:3 * MLSTM_HEADS + h + 1, :]
        ig = gates[h:h + 1, :]
        lf = _log_sigmoid(gates[MLSTM_HEADS + h:MLSTM_HEADS + h + 1, :])
        m_prev = m_ref[h:h + 1, :]
        c_state = c_ref[h]
        n_state = n_ref[h:h + 1, :]
        g = lf + m_prev
        m_t = jnp.maximum(g, ig)
        dw = jnp.exp(ig - m_t)
        gw = jnp.exp(g - m_t)
        s = jnp.sum(q * k, axis=1, keepdims=True) * dw
        num = s * v + gw * _vec_mat(q, c_state)
        den = s + gw * jnp.sum(q * n_state, axis=1, keepdims=True)
        hid = num / jnp.maximum(jnp.abs(den), jnp.exp(-m_t))
        w = jnp.exp(ig - m_t)
        decay = jnp.exp(g - m_t)
        c_out[h] = decay * c_state + _row_to_col(k * w) * v
        n_out[h:h + 1, :] = decay * n_state + w * k
        m_out[h:h + 1, :] = m_t
        y_ref[h:h + 1, :] = (_rms_norm(hid) * gain_ref[h:h + 1, :] * _sigmoid(og)).astype(BF16)


def _mlstm_sample(x, gates, bias, gain, c_state, n_state, m_state, layer_j):
    db = x.shape[0]
    nh, dh = MLSTM_HEADS, MLSTM_DH
    per_b3 = lambda b: (b, 0, 0)
    return pl.pallas_call(
        _mlstm_sample_kernel,
        grid=(db,),
        in_specs=[
            pl.BlockSpec((None, 4 * nh, dh), per_b3),
            pl.BlockSpec((None, 2 * nh, 1), per_b3),
            pl.BlockSpec((2 * nh, 1), lambda b: (0, 0)),
            pl.BlockSpec((None, nh, dh), lambda b: (layer_j, 0, 0)),
            pl.BlockSpec((None, None, nh, dh, dh), lambda b: (layer_j, b, 0, 0, 0)),
            pl.BlockSpec((None, None, nh, dh), lambda b: (layer_j, b, 0, 0)),
            pl.BlockSpec((None, None, nh, 1), lambda b: (layer_j, b, 0, 0)),
        ],
        out_specs=[
            pl.BlockSpec((None, nh, dh), per_b3),
            pl.BlockSpec((None, nh, dh, dh), lambda b: (b, 0, 0, 0)),
            pl.BlockSpec((None, nh, dh), per_b3),
            pl.BlockSpec((None, nh, 1), per_b3),
        ],
        out_shape=[
            jax.ShapeDtypeStruct((db, nh, dh), BF16),
            jax.ShapeDtypeStruct((db, nh, dh, dh), F32),
            jax.ShapeDtypeStruct((db, nh, dh), F32),
            jax.ShapeDtypeStruct((db, nh, 1), F32),
        ],
        compiler_params=_cparams("parallel"),
        name="mlstm_sample",
    )(x, gates, bias, gain, c_state, n_state, m_state)


def _odd_sample_kernel(xh_ref, xr_ref, lb_ref, cos_ref, sin_ref, gh_ref, gr_ref, sh_ref, sr_ref,
                       yh_ref, yr_ref, sh_out, sr_out, *, layer):
    nh = HGRN_HEADS
    lb = _hgrn_lower_bound(lb_ref, layer)
    f = lb + (1.0 - lb) * _sigmoid(xh_ref[nh:2 * nh, :])
    q = _silu(xh_ref[0:nh, :])
    kk = 1.0 - f
    v = xh_ref[2 * nh:3 * nh, :]
    gate = xh_ref[3 * nh:4 * nh, :]
    qk = jnp.sum(q * kk, axis=1, keepdims=True)
    for h in range(nh):
        state = sh_ref[h]
        f_col = _row_to_col(f[h:h + 1, :])
        k_col = _row_to_col(kk[h:h + 1, :])
        v_row = v[h:h + 1, :]
        o = qk[h:h + 1, :] * v_row + _vec_mat(q[h:h + 1, :] * f[h:h + 1, :], state)
        sh_out[h] = f_col * state + k_col * v_row
        yh_ref[h:h + 1, :] = (_rms_norm(o) * gh_ref[h:h + 1, :] * _silu(gate[h:h + 1, :])).astype(BF16)

    nr = RET_HEADS
    cos, sin = cos_ref[...], sin_ref[...]
    qr = _rotate(xr_ref[0:nr, :], cos, sin)
    kr = _rotate(xr_ref[nr:2 * nr, :], cos, sin) * (RET_DK ** -0.5)
    vr = xr_ref[2 * nr:3 * nr, :]
    gr = xr_ref[3 * nr:4 * nr, :]
    qkr = jnp.sum(qr * kr, axis=1, keepdims=True)
    for h in range(nr):
        gamma = math.exp(_ret_log_gamma(h))
        state = sr_ref[h]
        v_row = vr[h:h + 1, :]
        o = qkr[h:h + 1, :] * v_row + _vec_mat(qr[h:h + 1, :], state) * gamma
        sr_out[h] = gamma * state + _row_to_col(kr[h:h + 1, :]) * v_row
        yr_ref[h:h + 1, :] = (_rms_norm(o) * gr_ref[h:h + 1, :] * _silu(gr[h:h + 1, :])).astype(BF16)


def _odd_sample(xh, xr, lb3, cos, sin, gain_h, gain_r, state_h, state_r, layer, layer_j):
    db = xh.shape[0]
    per_b3 = lambda b: (b, 0, 0)
    return pl.pallas_call(
        functools.partial(_odd_sample_kernel, layer=layer),
        grid=(db,),
        in_specs=[
            pl.BlockSpec((None, 4 * HGRN_HEADS, HGRN_DK), per_b3),
            pl.BlockSpec((None, 4 * RET_HEADS, RET_DK), per_b3),
            pl.BlockSpec((DEPTH, HGRN_HEADS, HGRN_DK), lambda b: (0, 0, 0)),
            pl.BlockSpec((1, RET_DK // 2), lambda b: (0, 0)),
            pl.BlockSpec((1, RET_DK // 2), lambda b: (0, 0)),
            pl.BlockSpec((None, HGRN_HEADS, HGRN_DV), lambda b: (layer_j, 0, 0)),
            pl.BlockSpec((None, RET_HEADS, RET_DK), lambda b: (layer_j, 0, 0)),
            pl.BlockSpec((None, None, HGRN_HEADS, HGRN_DK, HGRN_DV), lambda b: (layer_j, b, 0, 0, 0)),
            pl.BlockSpec((None, None, RET_HEADS, RET_DK, RET_DK), lambda b: (layer_j, b, 0, 0, 0)),
        ],
        out_specs=[
            pl.BlockSpec((None, HGRN_HEADS, HGRN_DV), per_b3),
            pl.BlockSpec((None, RET_HEADS, RET_DK), per_b3),
            pl.BlockSpec((None, HGRN_HEADS, HGRN_DK, HGRN_DV), lambda b: (b, 0, 0, 0)),
            pl.BlockSpec((None, RET_HEADS, RET_DK, RET_DK), lambda b: (b, 0, 0, 0)),
        ],
        out_shape=[
            jax.ShapeDtypeStruct((db, HGRN_HEADS, HGRN_DV), BF16),
            jax.ShapeDtypeStruct((db, RET_HEADS, RET_DK), BF16),
            jax.ShapeDtypeStruct((db, HGRN_HEADS, HGRN_DK, HGRN_DV), F32),
            jax.ShapeDtypeStruct((db, RET_HEADS, RET_DK, RET_DK), F32),
        ],
        compiler_params=_cparams("parallel"),
        name="odd_sample",
    )(xh, xr, lb3, cos, sin, gain_h, gain_r, state_h, state_r)


def _repack_even(w):
    gates0 = 4 * HALF
    qb0 = gates0 + 2 * MLSTM_HEADS
    ki0 = qb0 + (EV_MISC - EV_QB)
    wi0 = ki0 + IDX_DIM
    pad = jnp.zeros(w.shape[:-1] + (EV_WIDTH - EV_MISC - IDX_DIM - IDX_HEADS - 2 * MLSTM_HEADS,), w.dtype)
    return jnp.concatenate(
        [w[..., :gates0], w[..., qb0:ki0], w[..., ki0:wi0], w[..., wi0:wi0 + IDX_HEADS],
         w[..., gates0:qb0], pad], axis=-1)


def _rot_tables(pos):
    inv = 1.0 / (RET_THETA ** jnp.linspace(0.0, 1.0, RET_DK // 2, dtype=F32))
    ang = pos.astype(F32)[:, None] * inv[None, :]
    return jnp.cos(ang), jnp.sin(ang)


def kernel(x_prompt, x_sample, state_mlstm_C, state_mlstm_n, state_mlstm_m, cache_k, cache_v, cache_idx_k,
           state_hgrn, state_ret, page_table, p_prompt, p_sample, ln_g, ln_b, w_ffn_up, w_ffn_down,
           w_in_even, b_gate_mlstm, g_mlstm, w_in_odd, hgrn_lb, g_hgrn, g_ret, w_out, w_pe_gate, w_pe_proj):
    batch, seq, _ = x_prompt.shape
    db, dec_seq, _ = x_sample.shape
    assert dec_seq == 1 and seq % Q_BLOCK == 0 and seq % CHUNK == 0
    n_even = w_in_even.shape[0]
    n_odd = w_in_odd.shape[0]
    n_pages = page_table.shape[1]
    past = n_pages * PAGE_SIZE
    mp = batch * seq

    w_up = w_ffn_up.astype(BF16)
    w_down = w_ffn_down.astype(BF16)
    w_even = _repack_even(w_in_even).astype(BF16)
    w_o = w_out.astype(BF16)
    w_pg = w_pe_gate.astype(BF16)
    w_pp = w_pe_proj.astype(BF16)
    ln_g3 = ln_g.reshape(DEPTH * 3, 1, D_MODEL)
    ln_b3 = ln_b.reshape(DEPTH * 3, 1, D_MODEL)
    pp = p_prompt.reshape(DEPTH, mp, P_DIM)
    ps = p_sample.reshape(DEPTH, db, P_DIM)
    cos_p, sin_p = _rot_tables(jnp.arange(seq))
    cos_s, sin_s = _rot_tables(past + jnp.arange(dec_seq))
    lb3 = hgrn_lb.reshape(DEPTH, 1, HGRN_HEADS * HGRN_DK)
    lb_heads = hgrn_lb.reshape(DEPTH, HGRN_HEADS, HGRN_DK)
    bias_row = jnp.zeros((n_even, 1, LANES), F32).at[:, 0, MISC_IG:MISC_IG + 2 * MLSTM_HEADS].set(b_gate_mlstm)
    bias_col = b_gate_mlstm.reshape(n_even, 2 * MLSTM_HEADS, 1)
    gain_m3 = g_mlstm.reshape(n_even, 1, HALF)
    gain_h3 = g_hgrn.reshape(n_odd, 1, HALF)
    gain_r3 = g_ret.reshape(n_odd, 1, HALF)
    cache_k3 = cache_k.reshape(cache_k.shape[:2] + (PAGE_SIZE * ATT_KV_HEADS, ATT_DH))
    cache_v3 = cache_v.reshape(cache_v.shape[:2] + (PAGE_SIZE * ATT_KV_HEADS, ATT_DH))
    m_state = state_mlstm_m.reshape(n_even, db, MLSTM_HEADS, 1)

    xp = x_prompt.reshape(mp, D_MODEL)
    xs = x_sample.reshape(db, D_MODEL)
    out_even = {k: [] for k in ("C_p", "C_s", "n_p", "n_s", "m_p", "m_s", "k_p", "k_s", "v_p", "v_s", "ik_p", "ik_s")}
    out_odd = {k: [] for k in ("h_p", "h_s", "r_p", "r_s")}

    for layer in range(DEPTH):
        j = layer // 2
        xp, xp_b = _ffn(xp, w_up, w_down, ln_g3, ln_b3, layer, 0, 0)
        xs, xs_b = _ffn(xs, w_up, w_down, ln_g3, ln_b3, layer, 0, 0)
        if layer % 2 == 0:
            zp = _proj(xp_b, w_even, j)
            zs = _proj(xs_b, w_even, j)
            ya_p, c_p, n_p, m_p = _mlstm_prompt(zp, bias_row[j], gain_m3, batch, seq, j)
            yb_p = _dsa_prompt(zp, batch, seq)

            x4 = zs[:, :4 * HALF].reshape(db, 4 * MLSTM_HEADS, MLSTM_DH)
            gates_s = zs[:, EV_MISC + MISC_IG:EV_MISC + MISC_IG + 2 * MLSTM_HEADS].reshape(db, 2 * MLSTM_HEADS, 1)
            ya_s, c_s, n_s, m_s = _mlstm_sample(x4, gates_s, bias_col[j], g_mlstm.reshape(n_even, MLSTM_HEADS, MLSTM_DH),
                                                state_mlstm_C, state_mlstm_n, m_state, j)
            q_s = zs[:, EV_QB:EV_KB].reshape(db, ATT_HEADS, ATT_DH)
            k_s = zs[:, EV_KB:EV_VB].reshape(db, ATT_KV_HEADS, ATT_DH)
            v_s = zs[:, EV_VB:EV_QI].reshape(db, ATT_KV_HEADS, ATT_DH)
            qi_s = zs[:, EV_QI:EV_MISC].reshape(db, IDX_HEADS, IDX_DIM)
            ki_s = zs[:, EV_MISC + MISC_KI:EV_MISC + MISC_KI + IDX_DIM].reshape(db, 1, IDX_DIM)
            wi_s = zs[:, EV_MISC + MISC_WI:EV_MISC + MISC_WI + IDX_HEADS].reshape(db, IDX_HEADS, 1)
            scores = _dsa_sample_scores(page_table, cache_idx_k, qi_s, wi_s, j)
            yb_s = _dsa_sample_attn(page_table, scores, q_s, qi_s, wi_s, ki_s, k_s, v_s, cache_k3, cache_v3, j)
            ya_s = ya_s.reshape(db, HALF)
            yb_s = yb_s.reshape(db, HALF)

            out_even["C_p"].append(c_p)
            out_even["C_s"].append(c_s)
            out_even["n_p"].append(n_p.reshape(batch, MLSTM_HEADS, MLSTM_DH))
            out_even["n_s"].append(n_s)
            out_even["m_p"].append(m_p[:, :, 0, 0])
            out_even["m_s"].append(m_s[:, :, 0])
            out_even["k_p"].append(zp[:, EV_KB:EV_VB].reshape(batch, seq, ATT_KV_HEADS, ATT_DH))
            out_even["k_s"].append(k_s.reshape(db, dec_seq, ATT_KV_HEADS, ATT_DH))
            out_even["v_p"].append(zp[:, EV_VB:EV_QI].reshape(batch, seq, ATT_KV_HEADS, ATT_DH))
            out_even["v_s"].append(v_s.reshape(db, dec_seq, ATT_KV_HEADS, ATT_DH))
            out_even["ik_p"].append(zp[:, EV_MISC + MISC_KI:EV_MISC + MISC_KI + IDX_DIM].reshape(batch, seq, IDX_DIM))
            out_even["ik_s"].append(ki_s.reshape(db, dec_seq, IDX_DIM))
        else:
            zp = _proj(xp_b, w_in_odd, j)
            zs = _proj(xs_b, w_in_odd, j)
            ya_p, h_p = _hgrn_prompt(zp, lb3, gain_h3, batch, seq, layer, j)
            yb_p, r_p = _ret_prompt(zp, cos_p, sin_p, gain_r3, batch, seq, j)
            xh = zs[:, :4 * HALF].reshape(db, 4 * HGRN_HEADS, HGRN_DK)
            xr = zs[:, 4 * HALF:].reshape(db, 4 * RET_HEADS, RET_DK)
            ya_s, yb_s, h_s, r_s = _odd_sample(
                xh, xr, lb_heads, cos_s, sin_s, g_hgrn.reshape(n_odd, HGRN_HEADS, HGRN_DV),
                g_ret.reshape(n_odd, RET_HEADS, RET_DK), state_hgrn, state_ret, layer, j)
            ya_s = ya_s.reshape(db, HALF)
            yb_s = yb_s.reshape(db, HALF)
            out_odd["h_p"].append(h_p)
            out_odd["h_s"].append(h_s)
            out_odd["r_p"].append(r_p)
            out_odd["r_s"].append(r_s)

        xp = _outproj(xp, ya_p, yb_p, w_o, ln_g3, ln_b3, layer)
        xs = _outproj(xs, ya_s, yb_s, w_o, ln_g3, ln_b3, layer)
        xp, xp_b = _ffn(xp, w_up, w_down, ln_g3, ln_b3, layer, 1, 2)
        xs, xs_b = _ffn(xs, w_up, w_down, ln_g3, ln_b3, layer, 1, 2)
        xp = _pe(xp, xp_b, pp, w_pg, w_pp, layer)
        xs = _pe(xs, xs_b, ps, w_pg, w_pp, layer)

    def stk(name, table, like):
        return jnp.stack(table[name]).astype(like.dtype)

    return (
        xp.reshape(batch, seq, D_MODEL), xs.reshape(db, dec_seq, D_MODEL),
        stk("C_p", out_even, state_mlstm_C), stk("C_s", out_even, state_mlstm_C),
        stk("n_p", out_even, state_mlstm_n), stk("n_s", out_even, state_mlstm_n),
        stk("m_p", out_even, state_mlstm_m), stk("m_s", out_even, state_mlstm_m),
        stk("k_p", out_even, cache_k), stk("k_s", out_even, cache_k),
        stk("v_p", out_even, cache_v), stk("v_s", out_even, cache_v),
        stk("ik_p", out_even, cache_idx_k), stk("ik_s", out_even, cache_idx_k),
        stk("h_p", out_odd, state_hgrn), stk("h_s", out_odd, state_hgrn),
        stk("r_p", out_odd, state_ret), stk("r_s", out_odd, state_ret),
    )
```

```python
import functools
import math

import jax
import jax.numpy as jnp
from jax import lax
from jax.experimental import pallas as pl
from jax.experimental.pallas import tpu as pltpu

F32 = jnp.float32
BF16 = jnp.bfloat16

D_MODEL = 2048
DEPTH = 2
PAGE_SIZE = 128
HALF = D_MODEL // 2
MLSTM_HEADS = 4
MLSTM_DH = HALF // MLSTM_HEADS
ATT_HEADS = 8
ATT_DH = HALF // ATT_HEADS
ATT_KV_HEADS = 2
ATT_GROUP = ATT_HEADS // ATT_KV_HEADS
IDX_HEADS = 8
IDX_DIM = 64
TOPK_MAX = 256
HGRN_HEADS = 8
HGRN_DK = 128
HGRN_DV = HALF // HGRN_HEADS
RET_HEADS = 4
RET_DK = HALF // RET_HEADS
RET_THETA = 10000.0
D_FF = 5632
P_DIM = 256
ALPHA = (2 * DEPTH) ** 0.25
LN_EPS = 1e-5

EV_QA, EV_KA, EV_VA, EV_OA = 0, HALF, 2 * HALF, 3 * HALF
EV_QB = 4 * HALF
EV_KB = EV_QB + ATT_HEADS * ATT_DH
EV_VB = EV_KB + ATT_KV_HEADS * ATT_DH
EV_QI = EV_VB + ATT_KV_HEADS * ATT_DH
EV_MISC = EV_QI + IDX_HEADS * IDX_DIM
LANES = 128
EV_WIDTH = EV_MISC + LANES
MISC_KI, MISC_WI = 0, IDX_DIM
MISC_IG = MISC_WI + IDX_HEADS
MISC_FG = MISC_IG + MLSTM_HEADS

MLSTM_CHUNK = 128
RET_CHUNK = 128
HGRN_CHUNK = 64
HGRN_SUB = 16
HGRN_GROUP = 4
DSA_Q_TILE = 256
DSA_WIDTH_STEP = 512
PAGE_GROUP = 16
VMEM_LIMIT = 56 * 1024 * 1024
INT_MIN = -2 ** 31
NEG_BIG = -1e30

NT_DIMS = (((1,), (1,)), ((), ()))
TN_DIMS = (((0,), (0,)), ((), ()))


def _cparams(*sem):
    return pltpu.CompilerParams(dimension_semantics=sem, vmem_limit_bytes=VMEM_LIMIT)


def _row_tile(m, largest=512):
    for t in (1024, 512, 256, 128, 64, 32, 16, 8):
        if t <= largest and m % t == 0:
            return t
    raise ValueError(f"row count {m} is not a multiple of 8")


def _col_tile(n):
    for t in (1024, 896, 512, 256, 128):
        if n % t == 0:
            return t
    raise ValueError(f"column count {n} is not a multiple of 128")


def _layer_norm(y, g, b):
    mu = jnp.mean(y, -1, keepdims=True)
    d = y - mu
    var = jnp.mean(d * d, -1, keepdims=True)
    return d * lax.rsqrt(var + LN_EPS) * g + b


def _rms_norm(h):
    return h * lax.rsqrt(jnp.mean(h * h, -1, keepdims=True) + LN_EPS)


def _sigmoid(x):
    return jax.nn.sigmoid(x)


def _silu(x):
    return x * jax.nn.sigmoid(x)


def _log_sigmoid(x):
    return jnp.minimum(x, 0.0) - jnp.log1p(jnp.exp(-jnp.abs(x)))


def _iota(shape, dim):
    return lax.broadcasted_iota(jnp.int32, shape, dim)


def _row_to_col(row):
    n = row.shape[1]
    eye = _iota((n, n), 0) == _iota((n, n), 1)
    return jnp.sum(jnp.where(eye, row, 0.0), axis=1, keepdims=True)


def _bdot(a, b):
    return jnp.dot(a.astype(BF16), b.astype(BF16), preferred_element_type=F32)


def _bdot_nt(a, b):
    return lax.dot_general(a.astype(BF16), b.astype(BF16), NT_DIMS, preferred_element_type=F32)


def _bdot_tn(a, b):
    return lax.dot_general(a.astype(BF16), b.astype(BF16), TN_DIMS, preferred_element_type=F32)


def _ffn_kernel(x_ref, wg_ref, wu_ref, wd_ref, g_ref, b_ref, o_ref, ob_ref, xb_ref, acc_ref):
    j = pl.program_id(1)

    @pl.when(j == 0)
    def _():
        xb_ref[...] = x_ref[...].astype(BF16)
        acc_ref[...] = jnp.zeros_like(acc_ref)

    xb = xb_ref[...]
    hg = jnp.dot(xb, wg_ref[...], preferred_element_type=F32)
    hu = jnp.dot(xb, wu_ref[...], preferred_element_type=F32)
    act = _silu(hg) * hu
    acc_ref[...] += jnp.dot(act.astype(BF16), wd_ref[...], preferred_element_type=F32)

    @pl.when(j == pl.num_programs(1) - 1)
    def _():
        y = _layer_norm(ALPHA * x_ref[...] + 0.5 * acc_ref[...], g_ref[...], b_ref[...])
        o_ref[...] = y
        ob_ref[...] = y.astype(BF16)


def _ffn(x, w_up, w_down, ln_g, ln_b, layer, which, ln_idx):
    m = x.shape[0]
    tm, tf = _row_tile(m), 512
    nf = D_FF // tf
    ln_row = layer * 3 + ln_idx
    return pl.pallas_call(
        _ffn_kernel,
        grid=(m // tm, nf),
        in_specs=[
            pl.BlockSpec((tm, D_MODEL), lambda i, j: (i, 0)),
            pl.BlockSpec((None, None, D_MODEL, tf), lambda i, j: (layer, which, 0, j)),
            pl.BlockSpec((None, None, D_MODEL, tf), lambda i, j: (layer, which, 0, j + nf)),
            pl.BlockSpec((None, None, tf, D_MODEL), lambda i, j: (layer, which, j, 0)),
            pl.BlockSpec((None, 1, D_MODEL), lambda i, j: (ln_row, 0, 0)),
            pl.BlockSpec((None, 1, D_MODEL), lambda i, j: (ln_row, 0, 0)),
        ],
        out_specs=[pl.BlockSpec((tm, D_MODEL), lambda i, j: (i, 0))] * 2,
        out_shape=[jax.ShapeDtypeStruct((m, D_MODEL), F32), jax.ShapeDtypeStruct((m, D_MODEL), BF16)],
        scratch_shapes=[pltpu.VMEM((tm, D_MODEL), BF16), pltpu.VMEM((tm, D_MODEL), F32)],
        compiler_params=_cparams("parallel", "arbitrary"),
        name="ffn",
    )(x, w_up, w_up, w_down, ln_g, ln_b)


def _proj_kernel(x_ref, w_ref, o_ref, wb_ref):
    @pl.when(pl.program_id(1) == 0)
    def _():
        wb_ref[...] = w_ref[...].astype(BF16)

    o_ref[...] = jnp.dot(x_ref[...], wb_ref[...], preferred_element_type=F32)


def _proj(xb, w, idx):
    m, n = xb.shape[0], w.shape[2]
    tm, tn = _row_tile(m, 1024), _col_tile(n)
    return pl.pallas_call(
        _proj_kernel,
        grid=(n // tn, m // tm),
        in_specs=[
            pl.BlockSpec((tm, D_MODEL), lambda j, i: (i, 0)),
            pl.BlockSpec((None, D_MODEL, tn), lambda j, i: (idx, 0, j)),
        ],
        out_specs=pl.BlockSpec((tm, tn), lambda j, i: (i, j)),
        out_shape=jax.ShapeDtypeStruct((m, n), F32),
        scratch_shapes=[pltpu.VMEM((D_MODEL, tn), BF16)],
        compiler_params=_cparams("parallel", "arbitrary"),
        name="in_proj",
    )(xb, w)


def _outproj_kernel(x_ref, ya_ref, yb_ref, wa_ref, wb_ref, g_ref, b_ref, o_ref):
    y = jnp.dot(ya_ref[...], wa_ref[...], preferred_element_type=F32)
    y += jnp.dot(yb_ref[...], wb_ref[...], preferred_element_type=F32)
    o_ref[...] = _layer_norm(ALPHA * x_ref[...] + y, g_ref[...], b_ref[...])


def _outproj(x, ya, yb, w_out, ln_g, ln_b, layer):
    m = x.shape[0]
    tm = _row_tile(m)
    ln_row = layer * 3 + 1
    return pl.pallas_call(
        _outproj_kernel,
        grid=(m // tm,),
        in_specs=[
            pl.BlockSpec((tm, D_MODEL), lambda i: (i, 0)),
            pl.BlockSpec((tm, HALF), lambda i: (i, 0)),
            pl.BlockSpec((tm, HALF), lambda i: (i, 0)),
            pl.BlockSpec((None, HALF, D_MODEL), lambda i: (layer, 0, 0)),
            pl.BlockSpec((None, HALF, D_MODEL), lambda i: (layer, 1, 0)),
            pl.BlockSpec((None, 1, D_MODEL), lambda i: (ln_row, 0, 0)),
            pl.BlockSpec((None, 1, D_MODEL), lambda i: (ln_row, 0, 0)),
        ],
        out_specs=pl.BlockSpec((tm, D_MODEL), lambda i: (i, 0)),
        out_shape=jax.ShapeDtypeStruct((m, D_MODEL), F32),
        compiler_params=_cparams("parallel"),
        name="out_proj",
    )(x, ya, yb, w_out, w_out, ln_g, ln_b)


def _pe_kernel(x_ref, xb_ref, p_ref, wg_ref, wp_ref, o_ref):
    gate = _sigmoid(jnp.dot(xb_ref[...], wg_ref[...], preferred_element_type=F32))
    o_ref[...] = x_ref[...] + gate * jnp.dot(p_ref[...].astype(BF16), wp_ref[...], preferred_element_type=F32)


def _pe(x, xb, p, w_gate, w_proj, layer):
    m = x.shape[0]
    tm = _row_tile(m)
    return pl.pallas_call(
        _pe_kernel,
        grid=(m // tm,),
        in_specs=[
            pl.BlockSpec((tm, D_MODEL), lambda i: (i, 0)),
            pl.BlockSpec((tm, D_MODEL), lambda i: (i, 0)),
            pl.BlockSpec((None, tm, P_DIM), lambda i: (layer, i, 0)),
            pl.BlockSpec((None, D_MODEL, D_MODEL), lambda i: (layer, 0, 0)),
            pl.BlockSpec((None, P_DIM, D_MODEL), lambda i: (layer, 0, 0)),
        ],
        out_specs=pl.BlockSpec((tm, D_MODEL), lambda i: (i, 0)),
        out_shape=jax.ShapeDtypeStruct((m, D_MODEL), F32),
        compiler_params=_cparams("parallel"),
        name="pe_gate",
    )(x, xb, p, w_gate, w_proj)


def _mlstm_prompt_kernel(q_ref, k_ref, v_ref, o_ref, misc_ref, bias_ref, gain_ref,
                         y_ref, c_ref, n_ref, m_ref, m_sc):
    chunk = q_ref.shape[0]
    dh = MLSTM_DH

    @pl.when(pl.program_id(1) == 0)
    def _():
        c_ref[...] = jnp.zeros_like(c_ref)
        n_ref[...] = jnp.zeros_like(n_ref)
        m_sc[...] = jnp.zeros_like(m_sc)

    gates = misc_ref[...] + bias_ref[...]
    ti = _iota((chunk, chunk), 0)
    si = _iota((chunk, chunk), 1)
    eye = ti == si
    causal = si <= ti
    for h in range(MLSTM_HEADS):
        cols = slice(h * dh, (h + 1) * dh)
        ig_col = gates[:, MISC_IG + h:MISC_IG + h + 1]
        lf_col = _log_sigmoid(gates[:, MISC_FG + h:MISC_FG + h + 1])
        lf_row = jnp.sum(jnp.where(eye, lf_col, 0.0), axis=0, keepdims=True)
        ig_row = jnp.sum(jnp.where(eye, ig_col, 0.0), axis=0, keepdims=True)
        b_col = jnp.sum(jnp.where(causal, lf_row, 0.0), axis=1, keepdims=True)
        b_row = jnp.sum(jnp.where(ti <= si, lf_col, 0.0), axis=0, keepdims=True)
        m_prev = m_sc[h][:, :1]
        dlog = jnp.where(causal, b_col - b_row + ig_row, -jnp.inf)
        g_col = b_col + m_prev
        m_t = jnp.maximum(g_col, jnp.max(dlog, axis=1, keepdims=True))
        dw = jnp.exp(dlog - m_t)
        gw = jnp.exp(g_col - m_t)

        q = q_ref[:, cols] * (dh ** -0.5)
        k = k_ref[:, cols]
        v = v_ref[:, cols]
        c_state = c_ref[h]
        n_state = n_ref[h]
        s = _bdot_nt(q, k) * dw
        num = _bdot(s, v) + gw * _bdot(q, c_state)
        den = jnp.sum(s, axis=1, keepdims=True) + gw * jnp.sum(q * n_state, axis=1, keepdims=True)
        hid = num / jnp.maximum(jnp.abs(den), jnp.exp(-m_t))

        b_last = b_col[chunk - 1:chunk, :]
        wlog = b_last - b_col + ig_col
        m_new = jnp.maximum(b_last + m_prev, jnp.max(wlog, axis=0, keepdims=True))
        w_col = jnp.exp(wlog - m_new)
        decay = jnp.exp(b_last + m_prev - m_new)
        kw = k * w_col
        c_ref[h] = decay * c_state + _bdot_tn(kw, v)
        n_ref[h] = decay * n_state + jnp.sum(kw, axis=0, keepdims=True)
        m_row = jnp.broadcast_to(m_new, (1, LANES))
        m_sc[h] = m_row
        m_ref[h] = m_row

        y_ref[:, cols] = (_rms_norm(hid) * gain_ref[:, cols] * _sigmoid(o_ref[:, cols])).astype(BF16)


def _mlstm_prompt(z, bias_row, gain, batch, seq, layer_j):
    chunk = MLSTM_CHUNK
    nc = seq // chunk
    nh, dh = MLSTM_HEADS, MLSTM_DH

    def col(base):
        return lambda b, c: (b * nc + c, base // HALF)

    state = lambda b, c: (b, 0, 0, 0)
    return pl.pallas_call(
        _mlstm_prompt_kernel,
        grid=(batch, nc),
        in_specs=[
            pl.BlockSpec((chunk, HALF), col(EV_QA)),
            pl.BlockSpec((chunk, HALF), col(EV_KA)),
            pl.BlockSpec((chunk, HALF), col(EV_VA)),
            pl.BlockSpec((chunk, HALF), col(EV_OA)),
            pl.BlockSpec((chunk, LANES), lambda b, c: (b * nc + c, EV_MISC // LANES)),
            pl.BlockSpec((1, LANES), lambda b, c: (0, 0)),
            pl.BlockSpec((None, 1, HALF), lambda b, c: (layer_j, 0, 0)),
        ],
        out_specs=[
            pl.BlockSpec((chunk, HALF), lambda b, c: (b * nc + c, 0)),
            pl.BlockSpec((None, nh, dh, dh), state),
            pl.BlockSpec((None, nh, 1, dh), state),
            pl.BlockSpec((None, nh, 1, LANES), state),
        ],
        out_shape=[
            jax.ShapeDtypeStruct((batch * seq, HALF), BF16),
            jax.ShapeDtypeStruct((batch, nh, dh, dh), F32),
            jax.ShapeDtypeStruct((batch, nh, 1, dh), F32),
            jax.ShapeDtypeStruct((batch, nh, 1, LANES), F32),
        ],
        scratch_shapes=[pltpu.VMEM((nh, 1, LANES), F32)],
        compiler_params=_cparams("parallel", "arbitrary"),
        name="mlstm_prompt",
    )(z, z, z, z, z, bias_row, gain)


def _hgrn_lower_bound(lb_ref, layer):
    rows = [lb_ref[j] for j in range(DEPTH)]
    mx = functools.reduce(jnp.maximum, rows)
    e = [jnp.exp(r - mx) for r in rows]
    total = functools.reduce(jnp.add, e)
    acc = jnp.zeros_like(total)
    for j in range(1, layer + 1):
        acc = acc + e[j] / total
    return acc


def _hgrn_prompt_kernel(q_ref, f_ref, i_ref, g_ref, lb_ref, gain_ref, y_ref, s_ref, *, layer):
    chunk = q_ref.shape[0]

    @pl.when(pl.program_id(2) == 0)
    def _():
        s_ref[...] = jnp.zeros_like(s_ref)

    lb = _hgrn_lower_bound(lb_ref, layer)
    f_all = lb + (1.0 - lb) * _sigmoid(f_ref[...])
    lf = jnp.log(f_all)
    tri = (_iota((chunk, chunk), 1) <= _iota((chunk, chunk), 0)).astype(F32)
    b_all = jnp.dot(tri, lf, preferred_element_type=F32, precision=lax.Precision.HIGHEST)
    t_in_blk = _iota((chunk, 1), 0) % HGRN_SUB

    for h in range(HGRN_GROUP):
        cols = slice(h * HGRN_DK, (h + 1) * HGRN_DK)
        b = b_all[:, cols]
        kk = 1.0 - f_all[:, cols]
        q = _silu(q_ref[:, cols])
        v = i_ref[:, cols]
        b_last = b[chunk - 1:chunk, :]
        state = s_ref[h]

        o = _bdot(q * jnp.exp(b), state)
        parts = []
        for blk in range(chunk // HGRN_SUB):
            r = blk * HGRN_SUB
            o_blk = o[r:r + HGRN_SUB]
            if blk > 0:
                b_ref_row = b[r - 1:r]
                qt = q[r:r + HGRN_SUB] * jnp.exp(b[r:r + HGRN_SUB] - b_ref_row)
                kt = kk[:r] * jnp.exp(b_ref_row - b[:r])
                o_blk = o_blk + _bdot(_bdot_nt(qt, kt), v[:r])
            parts.append(o_blk)
        o = jnp.concatenate(parts, axis=0)

        for dlt in range(HGRN_SUB):
            if dlt == 0:
                k_s, b_s, v_s = kk, b, v
            else:
                k_s, b_s, v_s = (pltpu.roll(a, dlt, 0) for a in (kk, b, v))
            e = jnp.exp(jnp.minimum(b - b_s, 0.0))
            a = jnp.sum(q * k_s * e, axis=1, keepdims=True)
            o = o + jnp.where(t_in_blk >= dlt, a, 0.0) * v_s

        s_ref[h] = _row_to_col(jnp.exp(b_last)) * state + _bdot_tn(kk * jnp.exp(b_last - b), v)
        y_ref[:, cols] = (_rms_norm(o) * gain_ref[:, cols] * _silu(g_ref[:, cols])).astype(BF16)


def _hgrn_prompt(z, lb3, gain, batch, seq, layer, layer_j):
    chunk = HGRN_CHUNK
    nc = seq // chunk
    n_groups = HGRN_HEADS // HGRN_GROUP
    dk = HGRN_GROUP * HGRN_DK

    def col(group):
        return lambda b, h, c: (b * nc + c, group * n_groups + h)

    return pl.pallas_call(
        functools.partial(_hgrn_prompt_kernel, layer=layer),
        grid=(batch, n_groups, nc),
        in_specs=[
            pl.BlockSpec((chunk, dk), col(0)),
            pl.BlockSpec((chunk, dk), col(1)),
            pl.BlockSpec((chunk, dk), col(2)),
            pl.BlockSpec((chunk, dk), col(3)),
            pl.BlockSpec((DEPTH, 1, dk), lambda b, h, c: (0, 0, h)),
            pl.BlockSpec((None, 1, dk), lambda b, h, c: (layer_j, 0, h)),
        ],
        out_specs=[
            pl.BlockSpec((chunk, dk), lambda b, h, c: (b * nc + c, h)),
            pl.BlockSpec((None, HGRN_GROUP, HGRN_DK, HGRN_DV), lambda b, h, c: (b, h, 0, 0)),
        ],
        out_shape=[
            jax.ShapeDtypeStruct((batch * seq, HALF), BF16),
            jax.ShapeDtypeStruct((batch, HGRN_HEADS, HGRN_DK, HGRN_DV), F32),
        ],
        compiler_params=_cparams("parallel", "parallel", "arbitrary"),
        name="hgrn_prompt",
    )(z, z, z, z, lb3, gain)


def _ret_log_gamma(h):
    return math.log1p(-2.0 ** (-5 - h))


def _rotate(x, cos, sin):
    half = x.shape[1] // 2
    x1, x2 = x[:, :half], x[:, half:]
    return jnp.concatenate([x1 * cos - x2 * sin, x1 * sin + x2 * cos], axis=1)


def _ret_prompt_kernel(q_ref, k_ref, v_ref, g_ref, cos_ref, sin_ref, gain_ref, y_ref, s_ref):
    chunk = q_ref.shape[0]

    @pl.when(pl.program_id(1) == 0)
    def _():
        s_ref[...] = jnp.zeros_like(s_ref)

    cos, sin = cos_ref[...], sin_ref[...]
    rel = (_iota((chunk, chunk), 0) - _iota((chunk, chunk), 1)).astype(F32)
    pos = _iota((chunk, 1), 0).astype(F32)
    for h in range(RET_HEADS):
        cols = slice(h * RET_DK, (h + 1) * RET_DK)
        lg = _ret_log_gamma(h)
        qr = _rotate(q_ref[:, cols], cos, sin)
        kr = _rotate(k_ref[:, cols], cos, sin) * (RET_DK ** -0.5)
        v = v_ref[:, cols]
        dmask = jnp.where(rel >= 0.0, jnp.exp(lg * jnp.maximum(rel, 0.0)), 0.0)
        state = s_ref[h]
        o = _bdot(_bdot_nt(qr, kr) * dmask, v) + _bdot(qr, state) * jnp.exp(lg * (pos + 1.0))
        tail = jnp.exp(lg * (chunk - 1.0 - pos))
        s_ref[h] = math.exp(lg * chunk) * state + _bdot_tn(kr * tail, v)
        y_ref[:, cols] = (_rms_norm(o) * gain_ref[:, cols] * _silu(g_ref[:, cols])).astype(BF16)


def _ret_prompt(z, cos, sin, gain, batch, seq, layer_j):
    chunk = RET_CHUNK
    nc = seq // chunk
    dk = RET_DK

    def col(group):
        return lambda b, c: (b * nc + c, 4 + group)

    return pl.pallas_call(
        _ret_prompt_kernel,
        grid=(batch, nc),
        in_specs=[
            pl.BlockSpec((chunk, HALF), col(0)),
            pl.BlockSpec((chunk, HALF), col(1)),
            pl.BlockSpec((chunk, HALF), col(2)),
            pl.BlockSpec((chunk, HALF), col(3)),
            pl.BlockSpec((chunk, dk // 2), lambda b, c: (c, 0)),
            pl.BlockSpec((chunk, dk // 2), lambda b, c: (c, 0)),
            pl.BlockSpec((None, 1, HALF), lambda b, c: (layer_j, 0, 0)),
        ],
        out_specs=[
            pl.BlockSpec((chunk, HALF), lambda b, c: (b * nc + c, 0)),
            pl.BlockSpec((None, RET_HEADS, dk, dk), lambda b, c: (b, 0, 0, 0)),
        ],
        out_shape=[
            jax.ShapeDtypeStruct((batch * seq, HALF), BF16),
            jax.ShapeDtypeStruct((batch, RET_HEADS, dk, dk), F32),
        ],
        compiler_params=_cparams("parallel", "arbitrary"),
        name="ret_prompt",
    )(z, z, z, z, cos, sin, gain)


def _score_key(score):
    bits = lax.bitcast_convert_type(score, jnp.int32)
    return jnp.where(bits < 0, -(bits & 0x7FFFFFFF), bits)


def _count(mask):
    return jnp.sum(jnp.where(mask, 1.0, 0.0), axis=1, keepdims=True)


def _kth_largest_key(count_ge, rows, n_sel):
    def body(it, t_off):
        cand = t_off | lax.shift_left(jnp.int32(1), 31 - it)
        return jnp.where(count_ge(cand ^ INT_MIN) >= n_sel, cand, t_off)

    t_off = lax.fori_loop(0, 32, body, jnp.zeros((rows, 1), jnp.int32))
    return t_off ^ INT_MIN


def _tie_cutoff(count_eq_below, rows, budget, n_bits):
    def body(it, cut):
        cand = cut | lax.shift_left(jnp.int32(1), n_bits - 1 - it)
        return jnp.where(count_eq_below(cand) <= budget, cand, cut)

    return lax.fori_loop(0, n_bits, body, jnp.zeros((rows, 1), jnp.int32))


def _select_top(key, valid, kpos, n_sel, n_idx_bits):
    rows = key.shape[0]
    thr = _kth_largest_key(lambda t: _count(key >= t), rows, n_sel)
    above = key > thr
    tied = key == thr
    n_above = _count(above)
    need_cut = jnp.max(n_above + _count(tied & valid)) > n_sel
    cut = lax.cond(
        need_cut,
        lambda: _tie_cutoff(lambda j: _count(tied & (kpos < j)), rows, n_sel - n_above, n_idx_bits),
        lambda: jnp.full((rows, 1), 2 ** 31 - 1, jnp.int32))
    return valid & (above | (tied & (kpos < cut)))


def _dsa_prompt_body(q_ref, qi_ref, qmisc_ref, kb_ref, vb_ref, kib_ref, y_ref, n_sel, width):
    qblk = q_ref.shape[0]
    ki = kib_ref[:width, :]
    qmisc = qmisc_ref[...]
    score = jnp.zeros((qblk, width), F32)
    for h in range(IDX_HEADS):
        w = qmisc[:, MISC_WI + h:MISC_WI + h + 1] * (IDX_HEADS ** -0.5) * (IDX_DIM ** -0.5)
        score = score + jnp.maximum(_bdot_nt(qi_ref[:, h * IDX_DIM:(h + 1) * IDX_DIM], ki), 0.0) * w

    qpos = pl.program_id(1) * qblk + _iota((qblk, width), 0)
    kpos = _iota((qblk, width), 1)
    valid = kpos <= qpos
    key = jnp.where(valid, _score_key(score), INT_MIN)
    sel = _select_top(key, valid, kpos, n_sel, (width + 1).bit_length())
    bias = jnp.where(sel, 0.0, -jnp.inf)

    exp2_scale = (ATT_DH ** -0.5) * math.log2(math.e)
    for g in range(ATT_KV_HEADS):
        kg = kb_ref[:width, g * ATT_DH:(g + 1) * ATT_DH]
        vg = vb_ref[:width, g * ATT_DH:(g + 1) * ATT_DH]
        for hh in range(ATT_GROUP):
            cols = slice((g * ATT_GROUP + hh) * ATT_DH, (g * ATT_GROUP + hh + 1) * ATT_DH)
            logits = _bdot_nt(q_ref[:, cols], kg) + bias
            e = jnp.exp2((logits - jnp.max(logits, axis=1, keepdims=True)) * exp2_scale)
            y_ref[:, cols] = (_bdot(e, vg) / jnp.sum(e, axis=1, keepdims=True)).astype(BF16)


def _dsa_prompt_kernel(q_ref, k_ref, v_ref, qi_ref, qmisc_ref, kmisc_ref, y_ref, kb_ref, vb_ref, kib_ref, *, n_sel):
    tile = q_ref.shape[0]
    i = pl.program_id(1)

    @pl.when(i == 0)
    def _():
        kb_ref[...] = k_ref[...].astype(BF16)
        vb_ref[...] = v_ref[...].astype(BF16)
        kib_ref[...] = kmisc_ref[:, MISC_KI:MISC_KI + IDX_DIM].astype(BF16)

    seq = k_ref.shape[0]
    step = DSA_WIDTH_STEP if seq % DSA_WIDTH_STEP == 0 else seq
    tiles_per_step = step // tile
    for var in range(seq // step):
        @pl.when((i >= var * tiles_per_step) & (i < (var + 1) * tiles_per_step))
        def _(width=(var + 1) * step):
            _dsa_prompt_body(q_ref, qi_ref, qmisc_ref, kb_ref, vb_ref, kib_ref, y_ref, n_sel, width)


def _dsa_prompt(z, batch, seq):
    tile = DSA_Q_TILE if seq % DSA_WIDTH_STEP == 0 else seq
    nt = seq // tile
    n_sel = min(TOPK_MAX, seq // 4)
    kv_w = ATT_KV_HEADS * ATT_DH
    qi_w = IDX_HEADS * IDX_DIM
    return pl.pallas_call(
        functools.partial(_dsa_prompt_kernel, n_sel=n_sel),
        grid=(batch, nt),
        in_specs=[
            pl.BlockSpec((tile, HALF), lambda b, i: (b * nt + i, EV_QB // HALF)),
            pl.BlockSpec((seq, kv_w), lambda b, i: (b, EV_KB // kv_w)),
            pl.BlockSpec((seq, kv_w), lambda b, i: (b, EV_VB // kv_w)),
            pl.BlockSpec((tile, qi_w), lambda b, i: (b * nt + i, EV_QI // qi_w)),
            pl.BlockSpec((tile, LANES), lambda b, i: (b * nt + i, EV_MISC // LANES)),
            pl.BlockSpec((seq, LANES), lambda b, i: (b, EV_MISC // LANES)),
        ],
        out_specs=pl.BlockSpec((tile, HALF), lambda b, i: (b * nt + i, 0)),
        out_shape=jax.ShapeDtypeStruct((batch * seq, HALF), BF16),
        scratch_shapes=[pltpu.VMEM((seq, kv_w), BF16), pltpu.VMEM((seq, kv_w), BF16),
                        pltpu.VMEM((seq, IDX_DIM), BF16)],
        compiler_params=_cparams("parallel", "arbitrary"),
        name="dsa_prompt",
    )(z, z, z, z, z, z)


def _page_specs(block, n_pages, layer_j):
    def spec(g):
        return pl.BlockSpec((None, None) + block,
                            lambda b, s, pt: (layer_j, pt[b * n_pages + s * PAGE_GROUP + g], 0, 0))

    return [spec(g) for g in range(PAGE_GROUP)]


def _dsa_sample_score_kernel(pt_ref, *refs):
    ik_refs = refs[:PAGE_GROUP]
    qi_ref, wi_ref, o_ref = refs[PAGE_GROUP:]
    qi = qi_ref[...]
    w = wi_ref[...] * (IDX_HEADS ** -0.5)
    rows = []
    for ik_ref in ik_refs:
        r = _bdot(qi, ik_ref[...]) * (IDX_DIM ** -0.5)
        rows.append(jnp.sum(jnp.maximum(r, 0.0) * w, axis=0, keepdims=True))
    o_ref[...] = jnp.concatenate(rows, axis=0)


def _dsa_sample_scores(page_table, cache_ik_t, qi, wi, layer_j):
    dec_batch, n_pages = page_table.shape
    assert n_pages % PAGE_GROUP == 0
    grid_spec = pltpu.PrefetchScalarGridSpec(
        num_scalar_prefetch=1,
        grid=(dec_batch, n_pages // PAGE_GROUP),
        in_specs=_page_specs((IDX_DIM, PAGE_SIZE), n_pages, layer_j) + [
            pl.BlockSpec((None, IDX_HEADS, IDX_DIM), lambda b, s, pt: (b, 0, 0)),
            pl.BlockSpec((None, IDX_HEADS, 1), lambda b, s, pt: (b, 0, 0)),
        ],
        out_specs=pl.BlockSpec((None, PAGE_GROUP, PAGE_SIZE), lambda b, s, pt: (b, s, 0)),
    )
    return pl.pallas_call(
        _dsa_sample_score_kernel,
        grid_spec=grid_spec,
        out_shape=jax.ShapeDtypeStruct((dec_batch, n_pages, PAGE_SIZE), F32),
        compiler_params=_cparams("parallel", "arbitrary"),
        name="dsa_sample_scores",
    )(page_table.reshape(-1), *([cache_ik_t] * PAGE_GROUP), qi, wi)


def _dsa_sample_attn_kernel(pt_ref, sc_ref, q_ref, qi_ref, wi_ref, kin_ref, knew_ref, vnew_ref, *refs, n_sel):
    kc_refs = refs[:PAGE_GROUP]
    vc_refs = refs[PAGE_GROUP:2 * PAGE_GROUP]
    y_ref, thr_sc, cut_sc, new_sc, m_sc, l_sc, acc_sc = refs[2 * PAGE_GROUP:]
    step = pl.program_id(1)
    past = sc_ref.shape[0] * PAGE_SIZE
    group_w = PAGE_GROUP * PAGE_SIZE * ATT_KV_HEADS

    @pl.when(step == 0)
    def _():
        r_new = jnp.sum(qi_ref[...] * kin_ref[...], axis=1, keepdims=True) * (IDX_DIM ** -0.5)
        s_new = jnp.sum(jnp.maximum(r_new, 0.0) * (wi_ref[...] * (IDX_HEADS ** -0.5)),
                        axis=0, keepdims=True)
        key_new = _score_key(s_new)
        key = _score_key(sc_ref[...])
        idx = _iota(key.shape, 0) * PAGE_SIZE + _iota(key.shape, 1)

        def total(mask, mask_new):
            return (jnp.sum(_count(mask), axis=0, keepdims=True) + jnp.where(mask_new, 1.0, 0.0))

        thr = _kth_largest_key(lambda t: total(key >= t, key_new >= t), 1, n_sel)
        above, tied = key > thr, key == thr
        budget = n_sel - total(above, key_new > thr)
        cut = _tie_cutoff(lambda j: total(tied & (idx < j), (key_new == thr) & (past < j)),
                          1, budget, (past + 2).bit_length())
        thr_sc[...] = jnp.broadcast_to(thr, thr_sc.shape)
        cut_sc[...] = jnp.broadcast_to(cut, cut_sc.shape)
        new_sc[...] = jnp.broadcast_to(key_new, new_sc.shape)
        m_sc[...] = jnp.full_like(m_sc, NEG_BIG)
        l_sc[...] = jnp.zeros_like(l_sc)
        acc_sc[...] = jnp.zeros_like(acc_sc)

    thr = thr_sc[:, :1]
    cut = cut_sc[:, :1]
    scale = ATT_DH ** -0.5
    q = q_ref[...]
    first_group = _iota((ATT_HEADS, 1), 0) < ATT_GROUP

    def by_group(fn):
        return jnp.where(first_group, fn(0), fn(1))

    def online_update(logits, pv):
        m_old = m_sc[...]
        m_new = jnp.maximum(m_old, jnp.max(logits, axis=1, keepdims=True))
        alpha = jnp.exp(m_old - m_new)
        pr = jnp.exp(logits - m_new)
        l_sc[...] = alpha * l_sc[...] + jnp.sum(pr, axis=1, keepdims=True)
        acc_sc[...] = alpha * acc_sc[...] + pv(pr)
        m_sc[...] = m_new

    first_page = pl.multiple_of(step * PAGE_GROUP, PAGE_GROUP)
    key = _score_key(sc_ref[pl.ds(first_page, PAGE_GROUP), :])
    idx = (first_page + _iota(key.shape, 0)) * PAGE_SIZE + _iota(key.shape, 1)
    sel = (key > thr) | ((key == thr) & (idx < cut))
    exp_shape = (PAGE_SIZE, PAGE_SIZE * ATT_KV_HEADS)
    expand = jnp.where(_iota(exp_shape, 1) // ATT_KV_HEADS == _iota(exp_shape, 0), 1.0, 0.0)
    sel_wide = _bdot(jnp.where(sel, 1.0, 0.0), expand)
    sel_row = jnp.concatenate([sel_wide[g:g + 1, :] for g in range(PAGE_GROUP)], axis=1)
    col_head = _iota((ATT_HEADS, group_w), 1) % ATT_KV_HEADS
    row_head = _iota((ATT_HEADS, group_w), 0) // ATT_GROUP
    mask = (sel_row > 0.5) & (col_head == row_head)
    kc = jnp.concatenate([r[...] for r in kc_refs], axis=0)
    vc = jnp.concatenate([r[...] for r in vc_refs], axis=0)
    logits = jnp.where(mask, _bdot_nt(q, kc) * scale, -jnp.inf)
    online_update(logits, lambda pr: _bdot(pr, vc))

    @pl.when(step == pl.num_programs(1) - 1)
    def _():
        key_new = new_sc[:, :1]
        sel_new = (key_new > thr) | ((key_new == thr) & (past < cut))
        k_new = by_group(lambda g: knew_ref[g:g + 1, :])
        v_new = by_group(lambda g: vnew_ref[g:g + 1, :])
        logit_new = jnp.sum(q * k_new, axis=1, keepdims=True) * scale
        logit_new = jnp.where(sel_new, logit_new, -jnp.inf)
        online_update(logit_new, lambda pr: pr * v_new)
        y_ref[...] = (acc_sc[...] / l_sc[...]).astype(BF16)


def _dsa_sample_attn(page_table, scores, q, qi, wi, ki_new, k_new, v_new, cache_k, cache_v, layer_j):
    dec_batch, n_pages = page_table.shape
    n_sel = min(TOPK_MAX, (n_pages * PAGE_SIZE + 1) // 4)
    per_b = lambda b, s, pt: (b, 0, 0)
    page_specs = _page_specs((PAGE_SIZE * ATT_KV_HEADS, ATT_DH), n_pages, layer_j)
    grid_spec = pltpu.PrefetchScalarGridSpec(
        num_scalar_prefetch=1,
        grid=(dec_batch, n_pages // PAGE_GROUP),
        in_specs=[
            pl.BlockSpec((None, n_pages, PAGE_SIZE), per_b),
            pl.BlockSpec((None, ATT_HEADS, ATT_DH), per_b),
            pl.BlockSpec((None, IDX_HEADS, IDX_DIM), per_b),
            pl.BlockSpec((None, IDX_HEADS, 1), per_b),
            pl.BlockSpec((None, 1, IDX_DIM), per_b),
            pl.BlockSpec((None, ATT_KV_HEADS, ATT_DH), per_b),
            pl.BlockSpec((None, ATT_KV_HEADS, ATT_DH), per_b),
        ] + page_specs + page_specs,
        out_specs=pl.BlockSpec((None, ATT_HEADS, ATT_DH), per_b),
        scratch_shapes=[
            pltpu.VMEM((1, LANES), jnp.int32),
            pltpu.VMEM((1, LANES), jnp.int32),
            pltpu.VMEM((1, LANES), jnp.int32),
            pltpu.VMEM((ATT_HEADS, 1), F32),
            pltpu.VMEM((ATT_HEADS, 1), F32),
            pltpu.VMEM((ATT_HEADS, ATT_DH), F32),
        ],
    )
    return pl.pallas_call(
        functools.partial(_dsa_sample_attn_kernel, n_sel=n_sel),
        grid_spec=grid_spec,
        out_shape=jax.ShapeDtypeStruct((dec_batch, ATT_HEADS, ATT_DH), BF16),
        compiler_params=_cparams("parallel", "arbitrary"),
        name="dsa_sample_attn",
    )(page_table.reshape(-1), scores, q, qi, wi, ki_new, k_new, v_new,
      *([cache_k] * PAGE_GROUP), *([cache_v] * PAGE_GROUP))


def _vec_mat(row, mat):
    return jnp.sum(_row_to_col(row) * mat, axis=0, keepdims=True)


def _mlstm_sample_kernel(x_ref, gate_ref, bias_ref, gain_ref, c_ref, n_ref, m_ref,
                         y_ref, c_out, n_out, m_out):
    gates = gate_ref[...] + bias_ref[...]
    for h in range(MLSTM_HEADS):
        q = x_ref[h:h + 1, :] * (MLSTM_DH ** -0.5)
        k = x_ref[MLSTM_HEADS + h:MLSTM_HEADS + h + 1, :]
        v = x_ref[2 * MLSTM_HEADS + h:2 * MLSTM_HEADS + h + 1, :]
        og = x_ref[3 * MLSTM_HEADS + h:3 * MLSTM_HEADS + h + 1, :]
        ig = gates[h:h + 1, :]
        lf = _log_sigmoid(gates[MLSTM_HEADS + h:MLSTM_HEADS + h + 1, :])
        m_prev = m_ref[h:h + 1, :]
        c_state = c_ref[h]
        n_state = n_ref[h:h + 1, :]
        g = lf + m_prev
        m_t = jnp.maximum(g, ig)
        dw = jnp.exp(ig - m_t)
        gw = jnp.exp(g - m_t)
        s = jnp.sum(q * k, axis=1, keepdims=True) * dw
        num = s * v + gw * _vec_mat(q, c_state)
        den = s + gw * jnp.sum(q * n_state, axis=1, keepdims=True)
        hid = num / jnp.maximum(jnp.abs(den), jnp.exp(-m_t))
        w = jnp.exp(ig - m_t)
        decay = jnp.exp(g - m_t)
        c_out[h] = decay * c_state + _row_to_col(k * w) * v
        n_out[h:h + 1, :] = decay * n_state + w * k
        m_out[h:h + 1, :] = m_t
        y_ref[h:h + 1, :] = (_rms_norm(hid) * gain_ref[h:h + 1, :] * _sigmoid(og)).astype(BF16)


def _mlstm_sample(x, gates, bias, gain, c_state, n_state, m_state, layer_j):
    db = x.shape[0]
    nh, dh = MLSTM_HEADS, MLSTM_DH
    per_b3 = lambda b: (b, 0, 0)
    return pl.pallas_call(
        _mlstm_sample_kernel,
        grid=(db,),
        in_specs=[
            pl.BlockSpec((None, 4 * nh, dh), per_b3),
            pl.BlockSpec((None, 2 * nh, 1), per_b3),
            pl.BlockSpec((2 * nh, 1), lambda b: (0, 0)),
            pl.BlockSpec((None, nh, dh), lambda b: (layer_j, 0, 0)),
            pl.BlockSpec((None, None, nh, dh, dh), lambda b: (layer_j, b, 0, 0, 0)),
            pl.BlockSpec((None, None, nh, dh), lambda b: (layer_j, b, 0, 0)),
            pl.BlockSpec((None, None, nh, 1), lambda b: (layer_j, b, 0, 0)),
        ],
        out_specs=[
            pl.BlockSpec((None, nh, dh), per_b3),
            pl.BlockSpec((None, nh, dh, dh), lambda b: (b, 0, 0, 0)),
            pl.BlockSpec((None, nh, dh), per_b3),
            pl.BlockSpec((None, nh, 1), per_b3),
        ],
        out_shape=[
            jax.ShapeDtypeStruct((db, nh, dh), BF16),
            jax.ShapeDtypeStruct((db, nh, dh, dh), F32),
            jax.ShapeDtypeStruct((db, nh, dh), F32),
            jax.ShapeDtypeStruct((db, nh, 1), F32),
        ],
        compiler_params=_cparams("parallel"),
        name="mlstm_sample",
    )(x, gates, bias, gain, c_state, n_state, m_state)


def _odd_sample_kernel(xh_ref, xr_ref, lb_ref, cos_ref, sin_ref, gh_ref, gr_ref, sh_ref, sr_ref,
                       yh_ref, yr_ref, sh_out, sr_out, *, layer):
    nh = HGRN_HEADS
    lb = _hgrn_lower_bound(lb_ref, layer)
    f = lb + (1.0 - lb) * _sigmoid(xh_ref[nh:2 * nh, :])
    q = _silu(xh_ref[0:nh, :])
    kk = 1.0 - f
    v = xh_ref[2 * nh:3 * nh, :]
    gate = xh_ref[3 * nh:4 * nh, :]
    qk = jnp.sum(q * kk, axis=1, keepdims=True)
    for h in range(nh):
        state = sh_ref[h]
        f_col = _row_to_col(f[h:h + 1, :])
        k_col = _row_to_col(kk[h:h + 1, :])
        v_row = v[h:h + 1, :]
        o = qk[h:h + 1, :] * v_row + _vec_mat(q[h:h + 1, :] * f[h:h + 1, :], state)
        sh_out[h] = f_col * state + k_col * v_row
        yh_ref[h:h + 1, :] = (_rms_norm(o) * gh_ref[h:h + 1, :] * _silu(gate[h:h + 1, :])).astype(BF16)

    nr = RET_HEADS
    cos, sin = cos_ref[...], sin_ref[...]
    qr = _rotate(xr_ref[0:nr, :], cos, sin)
    kr = _rotate(xr_ref[nr:2 * nr, :], cos, sin) * (RET_DK ** -0.5)
    vr = xr_ref[2 * nr:3 * nr, :]
    gr = xr_ref[3 * nr:4 * nr, :]
    qkr = jnp.sum(qr * kr, axis=1, keepdims=True)
    for h in range(nr):
        gamma = math.exp(_ret_log_gamma(h))
        state = sr_ref[h]
        v_row = vr[h:h + 1, :]
        o = qkr[h:h + 1, :] * v_row + _vec_mat(qr[h:h + 1, :], state) * gamma
        sr_out[h] = gamma * state + _row_to_col(kr[h:h + 1, :]) * v_row
        yr_ref[h:h + 1, :] = (_rms_norm(o) * gr_ref[h:h + 1, :] * _silu(gr[h:h + 1, :])).astype(BF16)


def _odd_sample(xh, xr, lb3, cos, sin, gain_h, gain_r, state_h, state_r, layer, layer_j):
    db = xh.shape[0]
    per_b3 = lambda b: (b, 0, 0)
    return pl.pallas_call(
        functools.partial(_odd_sample_kernel, layer=layer),
        grid=(db,),
        in_specs=[
            pl.BlockSpec((None, 4 * HGRN_HEADS, HGRN_DK), per_b3),
            pl.BlockSpec((None, 4 * RET_HEADS, RET_DK), per_b3),
            pl.BlockSpec((DEPTH, HGRN_HEADS, HGRN_DK), lambda b: (0, 0, 0)),
            pl.BlockSpec((1, RET_DK // 2), lambda b: (0, 0)),
            pl.BlockSpec((1, RET_DK // 2), lambda b: (0, 0)),
            pl.BlockSpec((None, HGRN_HEADS, HGRN_DV), lambda b: (layer_j, 0, 0)),
            pl.BlockSpec((None, RET_HEADS, RET_DK), lambda b: (layer_j, 0, 0)),
            pl.BlockSpec((None, None, HGRN_HEADS, HGRN_DK, HGRN_DV), lambda b: (layer_j, b, 0, 0, 0)),
            pl.BlockSpec((None, None, RET_HEADS, RET_DK, RET_DK), lambda b: (layer_j, b, 0, 0, 0)),
        ],
        out_specs=[
            pl.BlockSpec((None, HGRN_HEADS, HGRN_DV), per_b3),
            pl.BlockSpec((None, RET_HEADS, RET_DK), per_b3),
            pl.BlockSpec((None, HGRN_HEADS, HGRN_DK, HGRN_DV), lambda b: (b, 0, 0, 0)),
            pl.BlockSpec((None, RET_HEADS, RET_DK, RET_DK), lambda b: (b, 0, 0, 0)),
        ],
        out_shape=[
            jax.ShapeDtypeStruct((db, HGRN_HEADS, HGRN_DV), BF16),
            jax.ShapeDtypeStruct((db, RET_HEADS, RET_DK), BF16),
            jax.ShapeDtypeStruct((db, HGRN_HEADS, HGRN_DK, HGRN_DV), F32),
            jax.ShapeDtypeStruct((db, RET_HEADS, RET_DK, RET_DK), F32),
        ],
        compiler_params=_cparams("parallel"),
        name="odd_sample",
    )(xh, xr, lb3, cos, sin, gain_h, gain_r, state_h, state_r)


def _repack_even(w):
    gates0 = 4 * HALF
    qb0 = gates0 + 2 * MLSTM_HEADS
    ki0 = qb0 + (EV_MISC - EV_QB)
    wi0 = ki0 + IDX_DIM
    pad = jnp.zeros(w.shape[:-1] + (EV_WIDTH - EV_MISC - IDX_DIM - IDX_HEADS - 2 * MLSTM_HEADS,), w.dtype)
    return jnp.concatenate(
        [w[..., :gates0], w[..., qb0:ki0], w[..., ki0:wi0], w[..., wi0:wi0 + IDX_HEADS],
         w[..., gates0:qb0], pad], axis=-1)


def _rot_tables(pos):
    inv = 1.0 / (RET_THETA ** jnp.linspace(0.0, 1.0, RET_DK // 2, dtype=F32))
    ang = pos.astype(F32)[:, None] * inv[None, :]
    return jnp.cos(ang), jnp.sin(ang)


def kernel(x_prompt, x_sample, state_mlstm_C, state_mlstm_n, state_mlstm_m, cache_k, cache_v, cache_idx_k,
           state_hgrn, state_ret, page_table, p_prompt, p_sample, ln_g, ln_b, w_ffn_up, w_ffn_down,
           w_in_even, b_gate_mlstm, g_mlstm, w_in_odd, hgrn_lb, g_hgrn, g_ret, w_out, w_pe_gate, w_pe_proj):
    batch, seq, _ = x_prompt.shape
    db, dec_seq, _ = x_sample.shape
    assert dec_seq == 1 and all(seq % c == 0 for c in (MLSTM_CHUNK, RET_CHUNK, HGRN_CHUNK))
    n_even = w_in_even.shape[0]
    n_odd = w_in_odd.shape[0]
    n_pages = page_table.shape[1]
    past = n_pages * PAGE_SIZE
    mp = batch * seq

    w_up = w_ffn_up.astype(BF16)
    w_down = w_ffn_down.astype(BF16)
    w_even = _repack_even(w_in_even).astype(BF16)
    w_o = w_out.astype(BF16)
    w_pg = w_pe_gate.astype(BF16)
    w_pp = w_pe_proj.astype(BF16)
    ln_g3 = ln_g.reshape(DEPTH * 3, 1, D_MODEL)
    ln_b3 = ln_b.reshape(DEPTH * 3, 1, D_MODEL)
    pp = p_prompt.reshape(DEPTH, mp, P_DIM)
    ps = p_sample.reshape(DEPTH, db, P_DIM)
    cos_p, sin_p = _rot_tables(jnp.arange(seq))
    cos_s, sin_s = _rot_tables(past + jnp.arange(dec_seq))
    lb3 = hgrn_lb.reshape(DEPTH, 1, HGRN_HEADS * HGRN_DK)
    lb_heads = hgrn_lb.reshape(DEPTH, HGRN_HEADS, HGRN_DK)
    bias_row = jnp.zeros((n_even, 1, LANES), F32).at[:, 0, MISC_IG:MISC_IG + 2 * MLSTM_HEADS].set(b_gate_mlstm)
    bias_col = b_gate_mlstm.reshape(n_even, 2 * MLSTM_HEADS, 1)
    gain_m3 = g_mlstm.reshape(n_even, 1, HALF)
    gain_h3 = g_hgrn.reshape(n_odd, 1, HALF)
    gain_r3 = g_ret.reshape(n_odd, 1, HALF)
    cache_k3 = cache_k.reshape(cache_k.shape[:2] + (PAGE_SIZE * ATT_KV_HEADS, ATT_DH))
    cache_v3 = cache_v.reshape(cache_v.shape[:2] + (PAGE_SIZE * ATT_KV_HEADS, ATT_DH))
    cache_ik_t = jnp.swapaxes(cache_idx_k, 2, 3)
    m_state = state_mlstm_m.reshape(n_even, db, MLSTM_HEADS, 1)

    xp = x_prompt.reshape(mp, D_MODEL)
    xs = x_sample.reshape(db, D_MODEL)
    out_even = {k: [] for k in ("C_p", "C_s", "n_p", "n_s", "m_p", "m_s", "k_p", "k_s", "v_p", "v_s", "ik_p", "ik_s")}
    out_odd = {k: [] for k in ("h_p", "h_s", "r_p", "r_s")}

    for layer in range(DEPTH):
        j = layer // 2
        xp, xp_b = _ffn(xp, w_up, w_down, ln_g3, ln_b3, layer, 0, 0)
        xs, xs_b = _ffn(xs, w_up, w_down, ln_g3, ln_b3, layer, 0, 0)
        if layer % 2 == 0:
            zp = _proj(xp_b, w_even, j)
            zs = _proj(xs_b, w_even, j)
            ya_p, c_p, n_p, m_p = _mlstm_prompt(zp, bias_row[j], gain_m3, batch, seq, j)
            yb_p = _dsa_prompt(zp, batch, seq)

            x4 = zs[:, :4 * HALF].reshape(db, 4 * MLSTM_HEADS, MLSTM_DH)
            gates_s = zs[:, EV_MISC + MISC_IG:EV_MISC + MISC_IG + 2 * MLSTM_HEADS].reshape(db, 2 * MLSTM_HEADS, 1)
            ya_s, c_s, n_s, m_s = _mlstm_sample(x4, gates_s, bias_col[j], g_mlstm.reshape(n_even, MLSTM_HEADS, MLSTM_DH),
                                                state_mlstm_C, state_mlstm_n, m_state, j)
            q_s = zs[:, EV_QB:EV_KB].reshape(db, ATT_HEADS, ATT_DH)
            k_s = zs[:, EV_KB:EV_VB].reshape(db, ATT_KV_HEADS, ATT_DH)
            v_s = zs[:, EV_VB:EV_QI].reshape(db, ATT_KV_HEADS, ATT_DH)
            qi_s = zs[:, EV_QI:EV_MISC].reshape(db, IDX_HEADS, IDX_DIM)
            ki_s = zs[:, EV_MISC + MISC_KI:EV_MISC + MISC_KI + IDX_DIM].reshape(db, 1, IDX_DIM)
            wi_s = zs[:, EV_MISC + MISC_WI:EV_MISC + MISC_WI + IDX_HEADS].reshape(db, IDX_HEADS, 1)
            scores = _dsa_sample_scores(page_table, cache_ik_t, qi_s, wi_s, j)
            yb_s = _dsa_sample_attn(page_table, scores, q_s, qi_s, wi_s, ki_s, k_s, v_s, cache_k3, cache_v3, j)
            ya_s = ya_s.reshape(db, HALF)
            yb_s = yb_s.reshape(db, HALF)

            out_even["C_p"].append(c_p)
            out_even["C_s"].append(c_s)
            out_even["n_p"].append(n_p.reshape(batch, MLSTM_HEADS, MLSTM_DH))
            out_even["n_s"].append(n_s)
            out_even["m_p"].append(m_p[:, :, 0, 0])
            out_even["m_s"].append(m_s[:, :, 0])
            out_even["k_p"].append(zp[:, EV_KB:EV_VB].reshape(batch, seq, ATT_KV_HEADS, ATT_DH))
            out_even["k_s"].append(k_s.reshape(db, dec_seq, ATT_KV_HEADS, ATT_DH))
            out_even["v_p"].append(zp[:, EV_VB:EV_QI].reshape(batch, seq, ATT_KV_HEADS, ATT_DH))
            out_even["v_s"].append(v_s.reshape(db, dec_seq, ATT_KV_HEADS, ATT_DH))
            out_even["ik_p"].append(zp[:, EV_MISC + MISC_KI:EV_MISC + MISC_KI + IDX_DIM].reshape(batch, seq, IDX_DIM))
            out_even["ik_s"].append(ki_s.reshape(db, dec_seq, IDX_DIM))
        else:
            zp = _proj(xp_b, w_in_odd, j)
            zs = _proj(xs_b, w_in_odd, j)
            ya_p, h_p = _hgrn_prompt(zp, lb3, gain_h3, batch, seq, layer, j)
            yb_p, r_p = _ret_prompt(zp, cos_p, sin_p, gain_r3, batch, seq, j)
            xh = zs[:, :4 * HALF].reshape(db, 4 * HGRN_HEADS, HGRN_DK)
            xr = zs[:, 4 * HALF:].reshape(db, 4 * RET_HEADS, RET_DK)
            ya_s, yb_s, h_s, r_s = _odd_sample(
                xh, xr, lb_heads, cos_s, sin_s, g_hgrn.reshape(n_odd, HGRN_HEADS, HGRN_DV),
                g_ret.reshape(n_odd, RET_HEADS, RET_DK), state_hgrn, state_ret, layer, j)
            ya_s = ya_s.reshape(db, HALF)
            yb_s = yb_s.reshape(db, HALF)
            out_odd["h_p"].append(h_p)
            out_odd["h_s"].append(h_s)
            out_odd["r_p"].append(r_p)
            out_odd["r_s"].append(r_s)

        xp = _outproj(xp, ya_p, yb_p, w_o, ln_g3, ln_b3, layer)
        xs = _outproj(xs, ya_s, yb_s, w_o, ln_g3, ln_b3, layer)
        xp, xp_b = _ffn(xp, w_up, w_down, ln_g3, ln_b3, layer, 1, 2)
        xs, xs_b = _ffn(xs, w_up, w_down, ln_g3, ln_b3, layer, 1, 2)
        xp = _pe(xp, xp_b, pp, w_pg, w_pp, layer)
        xs = _pe(xs, xs_b, ps, w_pg, w_pp, layer)

    def stk(name, table, like):
        return jnp.stack(table[name]).astype(like.dtype)

    return (
        xp.reshape(batch, seq, D_MODEL), xs.reshape(db, dec_seq, D_MODEL),
        stk("C_p", out_even, state_mlstm_C), stk("C_s", out_even, state_mlstm_C),
        stk("n_p", out_even, state_mlstm_n), stk("n_s", out_even, state_mlstm_n),
        stk("m_p", out_even, state_mlstm_m), stk("m_s", out_even, state_mlstm_m),
        stk("k_p", out_even, cache_k), stk("k_s", out_even, cache_k),
        stk("v_p", out_even, cache_v), stk("v_s", out_even, cache_v),
        stk("ik_p", out_even, cache_idx_k), stk("ik_s", out_even, cache_idx_k),
        stk("h_p", out_odd, state_hgrn), stk("h_s", out_odd, state_hgrn),
        stk("r_p", out_odd, state_ret), stk("r_s", out_odd, state_ret),
    )
```

```python
import functools
import math

import jax
import jax.numpy as jnp
from jax import lax
from jax.experimental import pallas as pl
from jax.experimental.pallas import tpu as pltpu

F32 = jnp.float32
BF16 = jnp.bfloat16

D_MODEL = 2048
DEPTH = 2
PAGE_SIZE = 128
HALF = D_MODEL // 2
MLSTM_HEADS = 4
MLSTM_DH = HALF // MLSTM_HEADS
ATT_HEADS = 8
ATT_DH = HALF // ATT_HEADS
ATT_KV_HEADS = 2
ATT_GROUP = ATT_HEADS // ATT_KV_HEADS
IDX_HEADS = 8
IDX_DIM = 64
TOPK_MAX = 256
HGRN_HEADS = 8
HGRN_DK = 128
HGRN_DV = HALF // HGRN_HEADS
RET_HEADS = 4
RET_DK = HALF // RET_HEADS
RET_THETA = 10000.0
D_FF = 5632
P_DIM = 256
ALPHA = (2 * DEPTH) ** 0.25
LN_EPS = 1e-5

EV_QA, EV_KA, EV_VA, EV_OA = 0, HALF, 2 * HALF, 3 * HALF
EV_QB = 4 * HALF
EV_KB = EV_QB + ATT_HEADS * ATT_DH
EV_VB = EV_KB + ATT_KV_HEADS * ATT_DH
EV_QI = EV_VB + ATT_KV_HEADS * ATT_DH
EV_MISC = EV_QI + IDX_HEADS * IDX_DIM
LANES = 128
SUBLANES = 8
EV_WIDTH = EV_MISC + LANES
MISC_KI, MISC_WI = 0, IDX_DIM
MISC_IG = MISC_WI + IDX_HEADS
MISC_FG = MISC_IG + MLSTM_HEADS

MLSTM_CHUNK = 128
RET_CHUNK = 128
HGRN_CHUNK = 64
HGRN_GROUP = 8
DSA_Q_TILE = 256
DSA_WIDTH_STEP = 512
PAGE_GROUP = 16
VMEM_LIMIT = 56 * 1024 * 1024
INT_MIN = -2 ** 31
INT_MAX = 2 ** 31 - 1
NEG_BIG = -1e30

NT_DIMS = (((1,), (1,)), ((), ()))
TN_DIMS = (((0,), (0,)), ((), ()))


def _cparams(*sem):
    return pltpu.CompilerParams(dimension_semantics=sem, vmem_limit_bytes=VMEM_LIMIT)


def _row_tile(m, largest=512):
    for t in (1024, 512, 256, 128, 64, 32, 16, 8):
        if t <= largest and m % t == 0:
            return t
    raise ValueError(f"row count {m} is not a multiple of 8")


def _col_tile(n):
    for t in (1024, 896, 512, 256, 128):
        if n % t == 0:
            return t
    raise ValueError(f"column count {n} is not a multiple of 128")


def _layer_norm(y, g, b):
    mu = jnp.mean(y, -1, keepdims=True)
    d = y - mu
    var = jnp.mean(d * d, -1, keepdims=True)
    return d * lax.rsqrt(var + LN_EPS) * g + b


def _rms_norm(h):
    return h * lax.rsqrt(jnp.mean(h * h, -1, keepdims=True) + LN_EPS)


def _sigmoid(x):
    return jax.nn.sigmoid(x)


def _silu(x):
    return x * jax.nn.sigmoid(x)


def _log_sigmoid(x):
    return jnp.minimum(x, 0.0) - jnp.log1p(jnp.exp(-jnp.abs(x)))


def _iota(shape, dim):
    return lax.broadcasted_iota(jnp.int32, shape, dim)


def _row_to_col(row):
    n = row.shape[1]
    eye = _iota((n, n), 0) == _iota((n, n), 1)
    return jnp.sum(jnp.where(eye, row, 0.0), axis=1, keepdims=True)


def _bdot(a, b):
    return jnp.dot(a.astype(BF16), b.astype(BF16), preferred_element_type=F32)


def _bdot_nt(a, b):
    return lax.dot_general(a.astype(BF16), b.astype(BF16), NT_DIMS, preferred_element_type=F32)


def _bdot_tn(a, b):
    return lax.dot_general(a.astype(BF16), b.astype(BF16), TN_DIMS, preferred_element_type=F32)


def _ffn_kernel(x_ref, wg_ref, wu_ref, wd_ref, g_ref, b_ref, o_ref, ob_ref, xb_ref, acc_ref):
    j = pl.program_id(1)

    @pl.when(j == 0)
    def _():
        xb_ref[...] = x_ref[...].astype(BF16)
        acc_ref[...] = jnp.zeros_like(acc_ref)

    xb = xb_ref[...]
    hg = jnp.dot(xb, wg_ref[...], preferred_element_type=F32)
    hu = jnp.dot(xb, wu_ref[...], preferred_element_type=F32)
    act = _silu(hg) * hu
    acc_ref[...] += jnp.dot(act.astype(BF16), wd_ref[...], preferred_element_type=F32)

    @pl.when(j == pl.num_programs(1) - 1)
    def _():
        y = _layer_norm(ALPHA * x_ref[...] + 0.5 * acc_ref[...], g_ref[...], b_ref[...])
        o_ref[...] = y
        ob_ref[...] = y.astype(BF16)


def _ffn(x, w_up, w_down, ln_g, ln_b, layer, which, ln_idx):
    m = x.shape[0]
    tm, tf = _row_tile(m), 512
    nf = D_FF // tf
    ln_row = layer * 3 + ln_idx
    return pl.pallas_call(
        _ffn_kernel,
        grid=(m // tm, nf),
        in_specs=[
            pl.BlockSpec((tm, D_MODEL), lambda i, j: (i, 0)),
            pl.BlockSpec((None, None, D_MODEL, tf), lambda i, j: (layer, which, 0, j)),
            pl.BlockSpec((None, None, D_MODEL, tf), lambda i, j: (layer, which, 0, j + nf)),
            pl.BlockSpec((None, None, tf, D_MODEL), lambda i, j: (layer, which, j, 0)),
            pl.BlockSpec((None, 1, D_MODEL), lambda i, j: (ln_row, 0, 0)),
            pl.BlockSpec((None, 1, D_MODEL), lambda i, j: (ln_row, 0, 0)),
        ],
        out_specs=[pl.BlockSpec((tm, D_MODEL), lambda i, j: (i, 0))] * 2,
        out_shape=[jax.ShapeDtypeStruct((m, D_MODEL), F32), jax.ShapeDtypeStruct((m, D_MODEL), BF16)],
        scratch_shapes=[pltpu.VMEM((tm, D_MODEL), BF16), pltpu.VMEM((tm, D_MODEL), F32)],
        compiler_params=_cparams("parallel", "arbitrary"),
        name="ffn",
    )(x, w_up, w_up, w_down, ln_g, ln_b)


def _proj_kernel(x_ref, w_ref, o_ref, wb_ref):
    @pl.when(pl.program_id(1) == 0)
    def _():
        wb_ref[...] = w_ref[...].astype(BF16)

    o_ref[...] = jnp.dot(x_ref[...], wb_ref[...], preferred_element_type=F32)


def _proj(xb, w, idx):
    m, n = xb.shape[0], w.shape[2]
    tm, tn = _row_tile(m, 1024), _col_tile(n)
    return pl.pallas_call(
        _proj_kernel,
        grid=(n // tn, m // tm),
        in_specs=[
            pl.BlockSpec((tm, D_MODEL), lambda j, i: (i, 0)),
            pl.BlockSpec((None, D_MODEL, tn), lambda j, i: (idx, 0, j)),
        ],
        out_specs=pl.BlockSpec((tm, tn), lambda j, i: (i, j)),
        out_shape=jax.ShapeDtypeStruct((m, n), F32),
        scratch_shapes=[pltpu.VMEM((D_MODEL, tn), BF16)],
        compiler_params=_cparams("parallel", "arbitrary"),
        name="in_proj",
    )(xb, w)


def _outproj_kernel(x_ref, ya_ref, yb_ref, wa_ref, wb_ref, g_ref, b_ref, o_ref):
    y = jnp.dot(ya_ref[...], wa_ref[...], preferred_element_type=F32)
    y += jnp.dot(yb_ref[...], wb_ref[...], preferred_element_type=F32)
    o_ref[...] = _layer_norm(ALPHA * x_ref[...] + y, g_ref[...], b_ref[...])


def _outproj(x, ya, yb, w_out, ln_g, ln_b, layer):
    m = x.shape[0]
    tm = _row_tile(m)
    ln_row = layer * 3 + 1
    return pl.pallas_call(
        _outproj_kernel,
        grid=(m // tm,),
        in_specs=[
            pl.BlockSpec((tm, D_MODEL), lambda i: (i, 0)),
            pl.BlockSpec((tm, HALF), lambda i: (i, 0)),
            pl.BlockSpec((tm, HALF), lambda i: (i, 0)),
            pl.BlockSpec((None, HALF, D_MODEL), lambda i: (layer, 0, 0)),
            pl.BlockSpec((None, HALF, D_MODEL), lambda i: (layer, 1, 0)),
            pl.BlockSpec((None, 1, D_MODEL), lambda i: (ln_row, 0, 0)),
            pl.BlockSpec((None, 1, D_MODEL), lambda i: (ln_row, 0, 0)),
        ],
        out_specs=pl.BlockSpec((tm, D_MODEL), lambda i: (i, 0)),
        out_shape=jax.ShapeDtypeStruct((m, D_MODEL), F32),
        compiler_params=_cparams("parallel"),
        name="out_proj",
    )(x, ya, yb, w_out, w_out, ln_g, ln_b)


def _pe_kernel(x_ref, xb_ref, p_ref, wg_ref, wp_ref, o_ref):
    gate = _sigmoid(jnp.dot(xb_ref[...], wg_ref[...], preferred_element_type=F32))
    o_ref[...] = x_ref[...] + gate * jnp.dot(p_ref[...].astype(BF16), wp_ref[...], preferred_element_type=F32)


def _pe(x, xb, p, w_gate, w_proj, layer):
    m = x.shape[0]
    tm = _row_tile(m)
    return pl.pallas_call(
        _pe_kernel,
        grid=(m // tm,),
        in_specs=[
            pl.BlockSpec((tm, D_MODEL), lambda i: (i, 0)),
            pl.BlockSpec((tm, D_MODEL), lambda i: (i, 0)),
            pl.BlockSpec((None, tm, P_DIM), lambda i: (layer, i, 0)),
            pl.BlockSpec((None, D_MODEL, D_MODEL), lambda i: (layer, 0, 0)),
            pl.BlockSpec((None, P_DIM, D_MODEL), lambda i: (layer, 0, 0)),
        ],
        out_specs=pl.BlockSpec((tm, D_MODEL), lambda i: (i, 0)),
        out_shape=jax.ShapeDtypeStruct((m, D_MODEL), F32),
        compiler_params=_cparams("parallel"),
        name="pe_gate",
    )(x, xb, p, w_gate, w_proj)


def _mlstm_prompt_kernel(q_ref, k_ref, v_ref, o_ref, misc_ref, bias_ref, gain_ref,
                         y_ref, c_ref, n_ref, m_ref, m_sc):
    chunk = q_ref.shape[0]
    dh = MLSTM_DH

    @pl.when(pl.program_id(1) == 0)
    def _():
        c_ref[...] = jnp.zeros_like(c_ref)
        n_ref[...] = jnp.zeros_like(n_ref)
        m_sc[...] = jnp.zeros_like(m_sc)

    gates = misc_ref[...] + bias_ref[...]
    ti = _iota((chunk, chunk), 0)
    si = _iota((chunk, chunk), 1)
    eye = ti == si
    causal = si <= ti
    for h in range(MLSTM_HEADS):
        cols = slice(h * dh, (h + 1) * dh)
        ig_col = gates[:, MISC_IG + h:MISC_IG + h + 1]
        lf_col = _log_sigmoid(gates[:, MISC_FG + h:MISC_FG + h + 1])
        lf_row = jnp.sum(jnp.where(eye, lf_col, 0.0), axis=0, keepdims=True)
        ig_row = jnp.sum(jnp.where(eye, ig_col, 0.0), axis=0, keepdims=True)
        b_col = jnp.sum(jnp.where(causal, lf_row, 0.0), axis=1, keepdims=True)
        b_row = jnp.sum(jnp.where(ti <= si, lf_col, 0.0), axis=0, keepdims=True)
        m_prev = m_sc[h][:, :1]
        dlog = jnp.where(causal, b_col - b_row + ig_row, -jnp.inf)
        g_col = b_col + m_prev
        m_t = jnp.maximum(g_col, jnp.max(dlog, axis=1, keepdims=True))
        dw = jnp.exp(dlog - m_t)
        gw = jnp.exp(g_col - m_t)

        q = q_ref[:, cols] * (dh ** -0.5)
        k = k_ref[:, cols]
        v = v_ref[:, cols]
        c_state = c_ref[h]
        n_state = n_ref[h]
        s = _bdot_nt(q, k) * dw
        num = _bdot(s, v) + gw * _bdot(q, c_state)
        den = jnp.sum(s, axis=1, keepdims=True) + gw * jnp.sum(q * n_state, axis=1, keepdims=True)
        hid = num / jnp.maximum(jnp.abs(den), jnp.exp(-m_t))

        b_last = b_col[chunk - 1:chunk, :]
        wlog = b_last - b_col + ig_col
        m_new = jnp.maximum(b_last + m_prev, jnp.max(wlog, axis=0, keepdims=True))
        w_col = jnp.exp(wlog - m_new)
        decay = jnp.exp(b_last + m_prev - m_new)
        kw = k * w_col
        c_ref[h] = decay * c_state + _bdot_tn(kw, v)
        n_ref[h] = decay * n_state + jnp.sum(kw, axis=0, keepdims=True)
        m_row = jnp.broadcast_to(m_new, (1, LANES))
        m_sc[h] = m_row
        m_ref[h] = m_row

        y_ref[:, cols] = (_rms_norm(hid) * gain_ref[:, cols] * _sigmoid(o_ref[:, cols])).astype(BF16)


def _mlstm_prompt(z, bias_row, gain, batch, seq, layer_j):
    chunk = MLSTM_CHUNK
    nc = seq // chunk
    nh, dh = MLSTM_HEADS, MLSTM_DH

    def col(base):
        return lambda b, c: (b * nc + c, base // HALF)

    state = lambda b, c: (b, 0, 0, 0)
    return pl.pallas_call(
        _mlstm_prompt_kernel,
        grid=(batch, nc),
        in_specs=[
            pl.BlockSpec((chunk, HALF), col(EV_QA)),
            pl.BlockSpec((chunk, HALF), col(EV_KA)),
            pl.BlockSpec((chunk, HALF), col(EV_VA)),
            pl.BlockSpec((chunk, HALF), col(EV_OA)),
            pl.BlockSpec((chunk, LANES), lambda b, c: (b * nc + c, EV_MISC // LANES)),
            pl.BlockSpec((1, LANES), lambda b, c: (0, 0)),
            pl.BlockSpec((None, 1, HALF), lambda b, c: (layer_j, 0, 0)),
        ],
        out_specs=[
            pl.BlockSpec((chunk, HALF), lambda b, c: (b * nc + c, 0)),
            pl.BlockSpec((None, nh, dh, dh), state),
            pl.BlockSpec((None, nh, 1, dh), state),
            pl.BlockSpec((None, nh, 1, LANES), state),
        ],
        out_shape=[
            jax.ShapeDtypeStruct((batch * seq, HALF), BF16),
            jax.ShapeDtypeStruct((batch, nh, dh, dh), F32),
            jax.ShapeDtypeStruct((batch, nh, 1, dh), F32),
            jax.ShapeDtypeStruct((batch, nh, 1, LANES), F32),
        ],
        scratch_shapes=[pltpu.VMEM((nh, 1, LANES), F32)],
        compiler_params=_cparams("parallel", "arbitrary"),
        name="mlstm_prompt",
    )(z, z, z, z, z, bias_row, gain)


def _hgrn_lower_bound(lb_ref, layer):
    rows = [lb_ref[j] for j in range(DEPTH)]
    mx = functools.reduce(jnp.maximum, rows)
    e = [jnp.exp(r - mx) for r in rows]
    total = functools.reduce(jnp.add, e)
    acc = jnp.zeros_like(total)
    for j in range(1, layer + 1):
        acc = acc + e[j] / total
    return acc


def _block_mid_rows(b, half):
    rows, width = b.shape
    block = 2 * half
    if block >= SUBLANES:
        parts = [jnp.broadcast_to(b[r + half - 1:r + half, :], (block, width)) for r in range(0, rows, block)]
        return jnp.concatenate(parts, axis=0)
    b3 = b.reshape(rows // SUBLANES, SUBLANES, width)
    sub = _iota(b3.shape, 1)
    out = jnp.broadcast_to(b3[:, half - 1:half, :], b3.shape)
    for r in range(block, SUBLANES, block):
        out = jnp.where(sub >= r, jnp.broadcast_to(b3[:, r + half - 1:r + half, :], b3.shape), out)
    return out.reshape(rows, width)


def _hgrn_prompt_kernel(q_ref, f_ref, i_ref, g_ref, lb_ref, gain_ref, y_ref, s_ref, *, layer):
    chunk = q_ref.shape[0]

    @pl.when(pl.program_id(2) == 0)
    def _():
        s_ref[...] = jnp.zeros_like(s_ref)

    lb = _hgrn_lower_bound(lb_ref, layer)
    f_all = lb + (1.0 - lb) * _sigmoid(f_ref[...])
    lf = jnp.log(f_all)
    halves = [chunk >> (lvl + 1) for lvl in range(chunk.bit_length() - 1)]
    ti = _iota((chunk, chunk), 0)
    si = _iota((chunk, chunk), 1)
    row = _iota((chunk, 1), 0)
    level_mask = []
    second_half = []
    for half in halves:
        mid = (ti // (2 * half)) * (2 * half) + half - 1
        level_mask.append((si <= mid) & (ti > mid) & (si // (2 * half) == ti // (2 * half)))
        second_half.append(row % (2 * half) >= half)
    eye = ti == si
    b_all = jnp.dot(jnp.where(si <= ti, 1.0, 0.0), lf, preferred_element_type=F32,
                    precision=lax.Precision.HIGHEST)

    for h in range(HGRN_GROUP):
        cols = slice(h * HGRN_DK, (h + 1) * HGRN_DK)
        b = b_all[:, cols]
        kk = 1.0 - f_all[:, cols]
        q = _silu(q_ref[:, cols])
        v = i_ref[:, cols]
        b_last = b[chunk - 1:chunk, :]
        state = s_ref[h]

        attn = jnp.where(eye, jnp.sum(q * kk, axis=1, keepdims=True), 0.0)
        for lvl, half in enumerate(halves):
            x = jnp.where(second_half[lvl], q, kk) * jnp.exp(-jnp.abs(b - _block_mid_rows(b, half)))
            attn = attn + jnp.where(level_mask[lvl], _bdot_nt(x, x), 0.0)
        o = _bdot(attn, v) + _bdot(q * jnp.exp(b), state)

        s_ref[h] = _row_to_col(jnp.exp(b_last)) * state + _bdot_tn(kk * jnp.exp(b_last - b), v)
        y_ref[:, cols] = (_rms_norm(o) * gain_ref[:, cols] * _silu(g_ref[:, cols])).astype(BF16)


def _hgrn_prompt(z, lb3, gain, batch, seq, layer, layer_j):
    chunk = HGRN_CHUNK
    nc = seq // chunk
    n_groups = HGRN_HEADS // HGRN_GROUP
    dk = HGRN_GROUP * HGRN_DK

    def col(group):
        return lambda b, h, c: (b * nc + c, group * n_groups + h)

    return pl.pallas_call(
        functools.partial(_hgrn_prompt_kernel, layer=layer),
        grid=(batch, n_groups, nc),
        in_specs=[
            pl.BlockSpec((chunk, dk), col(0)),
            pl.BlockSpec((chunk, dk), col(1)),
            pl.BlockSpec((chunk, dk), col(2)),
            pl.BlockSpec((chunk, dk), col(3)),
            pl.BlockSpec((DEPTH, 1, dk), lambda b, h, c: (0, 0, h)),
            pl.BlockSpec((None, 1, dk), lambda b, h, c: (layer_j, 0, h)),
        ],
        out_specs=[
            pl.BlockSpec((chunk, dk), lambda b, h, c: (b * nc + c, h)),
            pl.BlockSpec((None, HGRN_GROUP, HGRN_DK, HGRN_DV), lambda b, h, c: (b, h, 0, 0)),
        ],
        out_shape=[
            jax.ShapeDtypeStruct((batch * seq, HALF), BF16),
            jax.ShapeDtypeStruct((batch, HGRN_HEADS, HGRN_DK, HGRN_DV), F32),
        ],
        compiler_params=_cparams("parallel", "parallel", "arbitrary"),
        name="hgrn_prompt",
    )(z, z, z, z, lb3, gain)


def _ret_log_gamma(h):
    return math.log1p(-2.0 ** (-5 - h))


def _rotate(x, cos, sin):
    half = x.shape[1] // 2
    x1, x2 = x[:, :half], x[:, half:]
    return jnp.concatenate([x1 * cos - x2 * sin, x1 * sin + x2 * cos], axis=1)


def _ret_prompt_kernel(q_ref, k_ref, v_ref, g_ref, cos_ref, sin_ref, gain_ref, y_ref, s_ref):
    chunk = q_ref.shape[0]

    @pl.when(pl.program_id(1) == 0)
    def _():
        s_ref[...] = jnp.zeros_like(s_ref)

    cos, sin = cos_ref[...], sin_ref[...]
    rel = (_iota((chunk, chunk), 0) - _iota((chunk, chunk), 1)).astype(F32)
    pos = _iota((chunk, 1), 0).astype(F32)
    for h in range(RET_HEADS):
        cols = slice(h * RET_DK, (h + 1) * RET_DK)
        lg = _ret_log_gamma(h)
        qr = _rotate(q_ref[:, cols], cos, sin)
        kr = _rotate(k_ref[:, cols], cos, sin) * (RET_DK ** -0.5)
        v = v_ref[:, cols]
        dmask = jnp.where(rel >= 0.0, jnp.exp(lg * jnp.maximum(rel, 0.0)), 0.0)
        state = s_ref[h]
        o = _bdot(_bdot_nt(qr, kr) * dmask, v) + _bdot(qr, state) * jnp.exp(lg * (pos + 1.0))
        tail = jnp.exp(lg * (chunk - 1.0 - pos))
        s_ref[h] = math.exp(lg * chunk) * state + _bdot_tn(kr * tail, v)
        y_ref[:, cols] = (_rms_norm(o) * gain_ref[:, cols] * _silu(g_ref[:, cols])).astype(BF16)


def _ret_prompt(z, cos, sin, gain, batch, seq, layer_j):
    chunk = RET_CHUNK
    nc = seq // chunk
    dk = RET_DK

    def col(group):
        return lambda b, c: (b * nc + c, 4 + group)

    return pl.pallas_call(
        _ret_prompt_kernel,
        grid=(batch, nc),
        in_specs=[
            pl.BlockSpec((chunk, HALF), col(0)),
            pl.BlockSpec((chunk, HALF), col(1)),
            pl.BlockSpec((chunk, HALF), col(2)),
            pl.BlockSpec((chunk, HALF), col(3)),
            pl.BlockSpec((chunk, dk // 2), lambda b, c: (c, 0)),
            pl.BlockSpec((chunk, dk // 2), lambda b, c: (c, 0)),
            pl.BlockSpec((None, 1, HALF), lambda b, c: (layer_j, 0, 0)),
        ],
        out_specs=[
            pl.BlockSpec((chunk, HALF), lambda b, c: (b * nc + c, 0)),
            pl.BlockSpec((None, RET_HEADS, dk, dk), lambda b, c: (b, 0, 0, 0)),
        ],
        out_shape=[
            jax.ShapeDtypeStruct((batch * seq, HALF), BF16),
            jax.ShapeDtypeStruct((batch, RET_HEADS, dk, dk), F32),
        ],
        compiler_params=_cparams("parallel", "arbitrary"),
        name="ret_prompt",
    )(z, z, z, z, cos, sin, gain)


def _score_key(score):
    bits = lax.bitcast_convert_type(score, jnp.int32)
    return jnp.where(bits < 0, -(bits & 0x7FFFFFFF), bits)


def _count(mask):
    return jnp.sum(jnp.where(mask, 1.0, 0.0), axis=1, keepdims=True)


def _kth_largest_key(count_ge, shape, n_sel):
    def body(it, t_off):
        cand = t_off | lax.shift_left(jnp.int32(1), 31 - it)
        return jnp.where(count_ge(cand ^ INT_MIN) >= n_sel, cand, t_off)

    t_off = lax.fori_loop(0, 32, body, jnp.zeros(shape, jnp.int32))
    return t_off ^ INT_MIN


def _tie_cutoff(count_eq_below, shape, budget, n_bits):
    def body(it, cut):
        cand = cut | lax.shift_left(jnp.int32(1), n_bits - 1 - it)
        return jnp.where(count_eq_below(cand) <= budget, cand, cut)

    return lax.fori_loop(0, n_bits, body, jnp.zeros(shape, jnp.int32))


def _select_top(key, valid, kpos, n_sel, n_idx_bits):
    per_row = (key.shape[0], 1)
    thr = _kth_largest_key(lambda t: _count(key >= t), per_row, n_sel)
    above = key > thr
    tied = key == thr
    n_above = _count(above)
    need_cut = jnp.max(n_above + _count(tied & valid)) > n_sel
    cut = lax.cond(
        need_cut,
        lambda: _tie_cutoff(lambda j: _count(tied & (kpos < j)), per_row, n_sel - n_above, n_idx_bits),
        lambda: jnp.full(per_row, INT_MAX, jnp.int32))
    return valid & (above | (tied & (kpos < cut)))


def _dsa_prompt_body(q_ref, qi_ref, qmisc_ref, kb_ref, vb_ref, kib_ref, y_ref, n_sel, width):
    qblk = q_ref.shape[0]
    ki = kib_ref[:width, :]
    qmisc = qmisc_ref[...]
    score = jnp.zeros((qblk, width), F32)
    for h in range(IDX_HEADS):
        w = qmisc[:, MISC_WI + h:MISC_WI + h + 1] * (IDX_HEADS ** -0.5) * (IDX_DIM ** -0.5)
        score = score + jnp.maximum(_bdot_nt(qi_ref[:, h * IDX_DIM:(h + 1) * IDX_DIM], ki), 0.0) * w

    qpos = pl.program_id(1) * qblk + _iota((qblk, width), 0)
    kpos = _iota((qblk, width), 1)
    valid = kpos <= qpos
    key = jnp.where(valid, _score_key(score), INT_MIN)
    sel = _select_top(key, valid, kpos, n_sel, (width + 1).bit_length())
    bias = jnp.where(sel, 0.0, -jnp.inf)

    exp2_scale = (ATT_DH ** -0.5) * math.log2(math.e)
    for g in range(ATT_KV_HEADS):
        kg = kb_ref[:width, g * ATT_DH:(g + 1) * ATT_DH]
        vg = vb_ref[:width, g * ATT_DH:(g + 1) * ATT_DH]
        for hh in range(ATT_GROUP):
            cols = slice((g * ATT_GROUP + hh) * ATT_DH, (g * ATT_GROUP + hh + 1) * ATT_DH)
            logits = _bdot_nt(q_ref[:, cols], kg) + bias
            e = jnp.exp2((logits - jnp.max(logits, axis=1, keepdims=True)) * exp2_scale)
            y_ref[:, cols] = (_bdot(e, vg) / jnp.sum(e, axis=1, keepdims=True)).astype(BF16)


def _dsa_prompt_kernel(q_ref, k_ref, v_ref, qi_ref, qmisc_ref, kmisc_ref, y_ref, kb_ref, vb_ref, kib_ref, *, n_sel):
    tile = q_ref.shape[0]
    i = pl.program_id(1)

    @pl.when(i == 0)
    def _():
        kb_ref[...] = k_ref[...].astype(BF16)
        vb_ref[...] = v_ref[...].astype(BF16)
        kib_ref[...] = kmisc_ref[:, MISC_KI:MISC_KI + IDX_DIM].astype(BF16)

    seq = k_ref.shape[0]
    step = DSA_WIDTH_STEP if seq % DSA_WIDTH_STEP == 0 else seq
    tiles_per_step = step // tile
    for var in range(seq // step):
        @pl.when((i >= var * tiles_per_step) & (i < (var + 1) * tiles_per_step))
        def _(width=(var + 1) * step):
            _dsa_prompt_body(q_ref, qi_ref, qmisc_ref, kb_ref, vb_ref, kib_ref, y_ref, n_sel, width)


def _dsa_prompt(z, batch, seq):
    tile = DSA_Q_TILE if seq % DSA_WIDTH_STEP == 0 else seq
    nt = seq // tile
    n_sel = min(TOPK_MAX, seq // 4)
    kv_w = ATT_KV_HEADS * ATT_DH
    qi_w = IDX_HEADS * IDX_DIM
    return pl.pallas_call(
        functools.partial(_dsa_prompt_kernel, n_sel=n_sel),
        grid=(batch, nt),
        in_specs=[
            pl.BlockSpec((tile, HALF), lambda b, i: (b * nt + i, EV_QB // HALF)),
            pl.BlockSpec((seq, kv_w), lambda b, i: (b, EV_KB // kv_w)),
            pl.BlockSpec((seq, kv_w), lambda b, i: (b, EV_VB // kv_w)),
            pl.BlockSpec((tile, qi_w), lambda b, i: (b * nt + i, EV_QI // qi_w)),
            pl.BlockSpec((tile, LANES), lambda b, i: (b * nt + i, EV_MISC // LANES)),
            pl.BlockSpec((seq, LANES), lambda b, i: (b, EV_MISC // LANES)),
        ],
        out_specs=pl.BlockSpec((tile, HALF), lambda b, i: (b * nt + i, 0)),
        out_shape=jax.ShapeDtypeStruct((batch * seq, HALF), BF16),
        scratch_shapes=[pltpu.VMEM((seq, kv_w), BF16), pltpu.VMEM((seq, kv_w), BF16),
                        pltpu.VMEM((seq, IDX_DIM), BF16)],
        compiler_params=_cparams("parallel", "arbitrary"),
        name="dsa_prompt",
    )(z, z, z, z, z, z)


def _page_specs(block, n_pages, layer_j):
    def spec(g):
        return pl.BlockSpec((None, None) + block,
                            lambda b, s, pt: (layer_j, pt[b * n_pages + s * PAGE_GROUP + g], 0, 0))

    return [spec(g) for g in range(PAGE_GROUP)]


def _dsa_sample_score_kernel(pt_ref, *refs):
    ik_refs = refs[:PAGE_GROUP]
    qi_ref, wi_ref, o_ref = refs[PAGE_GROUP:]
    qi = qi_ref[...]
    w = wi_ref[...] * (IDX_HEADS ** -0.5)
    rows = []
    for ik_ref in ik_refs:
        r = _bdot(qi, ik_ref[...]) * (IDX_DIM ** -0.5)
        rows.append(jnp.sum(jnp.maximum(r, 0.0) * w, axis=0, keepdims=True))
    o_ref[...] = jnp.concatenate(rows, axis=0)


def _dsa_sample_scores(page_table, cache_ik_t, qi, wi, layer_j):
    dec_batch, n_pages = page_table.shape
    assert n_pages % PAGE_GROUP == 0
    grid_spec = pltpu.PrefetchScalarGridSpec(
        num_scalar_prefetch=1,
        grid=(dec_batch, n_pages // PAGE_GROUP),
        in_specs=_page_specs((IDX_DIM, PAGE_SIZE), n_pages, layer_j) + [
            pl.BlockSpec((None, IDX_HEADS, IDX_DIM), lambda b, s, pt: (b, 0, 0)),
            pl.BlockSpec((None, IDX_HEADS, 1), lambda b, s, pt: (b, 0, 0)),
        ],
        out_specs=pl.BlockSpec((None, PAGE_GROUP, PAGE_SIZE), lambda b, s, pt: (b, s, 0)),
    )
    return pl.pallas_call(
        _dsa_sample_score_kernel,
        grid_spec=grid_spec,
        out_shape=jax.ShapeDtypeStruct((dec_batch, n_pages, PAGE_SIZE), F32),
        compiler_params=_cparams("parallel", "arbitrary"),
        name="dsa_sample_scores",
    )(page_table.reshape(-1), *([cache_ik_t] * PAGE_GROUP), qi, wi)


def _dsa_sample_select_kernel(sc_ref, qi_ref, wi_ref, kin_ref, thr_ref, cut_ref, new_ref, *, n_sel):
    past = sc_ref.shape[1] * PAGE_SIZE
    r_new = jnp.sum(qi_ref[...] * kin_ref[...], axis=2, keepdims=True) * (IDX_DIM ** -0.5)
    s_new = jnp.sum(jnp.maximum(r_new, 0.0) * (wi_ref[...] * (IDX_HEADS ** -0.5)), axis=1, keepdims=True)
    key_new = _score_key(s_new)
    key = _score_key(sc_ref[...])
    idx = _iota(key.shape, 1) * PAGE_SIZE + _iota(key.shape, 2)
    per_req = key_new.shape

    def total(mask, mask_new):
        in_page = jnp.sum(jnp.where(mask, 1.0, 0.0), axis=2, keepdims=True)
        return jnp.sum(in_page, axis=1, keepdims=True) + jnp.where(mask_new, 1.0, 0.0)

    thr = _kth_largest_key(lambda t: total(key >= t, key_new >= t), per_req, n_sel)
    above, tied = key > thr, key == thr
    n_above = total(above, key_new > thr)
    need_cut = jnp.max(n_above + total(tied, key_new == thr)) > n_sel
    cut = lax.cond(
        need_cut,
        lambda: _tie_cutoff(lambda j: total(tied & (idx < j), (key_new == thr) & (past < j)),
                            per_req, n_sel - n_above, (past + 2).bit_length()),
        lambda: jnp.full(per_req, INT_MAX, jnp.int32))
    thr_ref[...] = jnp.broadcast_to(thr, thr_ref.shape)
    cut_ref[...] = jnp.broadcast_to(cut, cut_ref.shape)
    new_ref[...] = jnp.broadcast_to(key_new, new_ref.shape)


def _dsa_sample_select(scores, qi, wi, ki_new):
    dec_batch, n_pages, _ = scores.shape
    n_sel = min(TOPK_MAX, (n_pages * PAGE_SIZE + 1) // 4)
    out = jax.ShapeDtypeStruct((dec_batch, 1, LANES), jnp.int32)
    return pl.pallas_call(
        functools.partial(_dsa_sample_select_kernel, n_sel=n_sel),
        out_shape=[out, out, out],
        compiler_params=pltpu.CompilerParams(vmem_limit_bytes=VMEM_LIMIT),
        name="dsa_sample_select",
    )(scores, qi, wi, ki_new)


def _dsa_sample_attn_kernel(pt_ref, sc_ref, thr_ref, cut_ref, new_ref, q_ref, knew_ref, vnew_ref, *refs):
    kc_refs = refs[:PAGE_GROUP]
    vc_refs = refs[PAGE_GROUP:2 * PAGE_GROUP]
    y_ref, m_sc, l_sc, acc_sc = refs[2 * PAGE_GROUP:]
    step = pl.program_id(1)
    past = sc_ref.shape[0] * PAGE_SIZE
    group_w = PAGE_GROUP * PAGE_SIZE * ATT_KV_HEADS

    @pl.when(step == 0)
    def _():
        m_sc[...] = jnp.full_like(m_sc, NEG_BIG)
        l_sc[...] = jnp.zeros_like(l_sc)
        acc_sc[...] = jnp.zeros_like(acc_sc)

    thr = thr_ref[:, :1]
    cut = cut_ref[:, :1]
    scale = ATT_DH ** -0.5
    q = q_ref[...]
    first_group = _iota((ATT_HEADS, 1), 0) < ATT_GROUP

    def by_group(fn):
        return jnp.where(first_group, fn(0), fn(1))

    def online_update(logits, pv):
        m_old = m_sc[...]
        m_new = jnp.maximum(m_old, jnp.max(logits, axis=1, keepdims=True))
        alpha = jnp.exp(m_old - m_new)
        pr = jnp.exp(logits - m_new)
        l_sc[...] = alpha * l_sc[...] + jnp.sum(pr, axis=1, keepdims=True)
        acc_sc[...] = alpha * acc_sc[...] + pv(pr)
        m_sc[...] = m_new

    first_page = pl.multiple_of(step * PAGE_GROUP, PAGE_GROUP)
    key = _score_key(sc_ref[pl.ds(first_page, PAGE_GROUP), :])
    idx = (first_page + _iota(key.shape, 0)) * PAGE_SIZE + _iota(key.shape, 1)
    sel = (key > thr) | ((key == thr) & (idx < cut))
    exp_shape = (PAGE_SIZE, PAGE_SIZE * ATT_KV_HEADS)
    expand = jnp.where(_iota(exp_shape, 1) // ATT_KV_HEADS == _iota(exp_shape, 0), 1.0, 0.0)
    sel_wide = _bdot(jnp.where(sel, 1.0, 0.0), expand)
    sel_row = jnp.concatenate([sel_wide[g:g + 1, :] for g in range(PAGE_GROUP)], axis=1)
    col_head = _iota((ATT_HEADS, group_w), 1) % ATT_KV_HEADS
    row_head = _iota((ATT_HEADS, group_w), 0) // ATT_GROUP
    mask = (sel_row > 0.5) & (col_head == row_head)
    kc = jnp.concatenate([r[...] for r in kc_refs], axis=0)
    vc = jnp.concatenate([r[...] for r in vc_refs], axis=0)
    logits = jnp.where(mask, _bdot_nt(q, kc) * scale, -jnp.inf)
    online_update(logits, lambda pr: _bdot(pr, vc))

    @pl.when(step == pl.num_programs(1) - 1)
    def _():
        key_new = new_ref[:, :1]
        sel_new = (key_new > thr) | ((key_new == thr) & (past < cut))
        k_new = by_group(lambda g: knew_ref[g:g + 1, :])
        v_new = by_group(lambda g: vnew_ref[g:g + 1, :])
        logit_new = jnp.sum(q * k_new, axis=1, keepdims=True) * scale
        logit_new = jnp.where(sel_new, logit_new, -jnp.inf)
        online_update(logit_new, lambda pr: pr * v_new)
        y_ref[...] = (acc_sc[...] / l_sc[...]).astype(BF16)


def _dsa_sample_attn(page_table, scores, select, q, k_new, v_new, cache_k, cache_v, layer_j):
    dec_batch, n_pages = page_table.shape
    per_b = lambda b, s, pt: (b, 0, 0)
    page_specs = _page_specs((PAGE_SIZE * ATT_KV_HEADS, ATT_DH), n_pages, layer_j)
    grid_spec = pltpu.PrefetchScalarGridSpec(
        num_scalar_prefetch=1,
        grid=(dec_batch, n_pages // PAGE_GROUP),
        in_specs=[
            pl.BlockSpec((None, n_pages, PAGE_SIZE), per_b),
            pl.BlockSpec((None, 1, LANES), per_b),
            pl.BlockSpec((None, 1, LANES), per_b),
            pl.BlockSpec((None, 1, LANES), per_b),
            pl.BlockSpec((None, ATT_HEADS, ATT_DH), per_b),
            pl.BlockSpec((None, ATT_KV_HEADS, ATT_DH), per_b),
            pl.BlockSpec((None, ATT_KV_HEADS, ATT_DH), per_b),
        ] + page_specs + page_specs,
        out_specs=pl.BlockSpec((None, ATT_HEADS, ATT_DH), per_b),
        scratch_shapes=[
            pltpu.VMEM((ATT_HEADS, 1), F32),
            pltpu.VMEM((ATT_HEADS, 1), F32),
            pltpu.VMEM((ATT_HEADS, ATT_DH), F32),
        ],
    )
    return pl.pallas_call(
        _dsa_sample_attn_kernel,
        grid_spec=grid_spec,
        out_shape=jax.ShapeDtypeStruct((dec_batch, ATT_HEADS, ATT_DH), BF16),
        compiler_params=_cparams("parallel", "arbitrary"),
        name="dsa_sample_attn",
    )(page_table.reshape(-1), scores, *select, q, k_new, v_new,
      *([cache_k] * PAGE_GROUP), *([cache_v] * PAGE_GROUP))


def _vec_mat(row, mat):
    return jnp.sum(_row_to_col(row) * mat, axis=0, keepdims=True)


def _mlstm_sample_kernel(x_ref, gate_ref, bias_ref, gain_ref, c_ref, n_ref, m_ref,
                         y_ref, c_out, n_out, m_out):
    gates = gate_ref[...] + bias_ref[...]
    for h in range(MLSTM_HEADS):
        q = x_ref[h:h + 1, :] * (MLSTM_DH ** -0.5)
        k = x_ref[MLSTM_HEADS + h:MLSTM_HEADS + h + 1, :]
        v = x_ref[2 * MLSTM_HEADS + h:2 * MLSTM_HEADS + h + 1, :]
        og = x_ref[3 * MLSTM_HEADS + h:3 * MLSTM_HEADS + h + 1, :]
        ig = gates[h:h + 1, :]
        lf = _log_sigmoid(gates[MLSTM_HEADS + h:MLSTM_HEADS + h + 1, :])
        m_prev = m_ref[h:h + 1, :]
        c_state = c_ref[h]
        n_state = n_ref[h:h + 1, :]
        g = lf + m_prev
        m_t = jnp.maximum(g, ig)
        dw = jnp.exp(ig - m_t)
        gw = jnp.exp(g - m_t)
        s = jnp.sum(q * k, axis=1, keepdims=True) * dw
        num = s * v + gw * _vec_mat(q, c_state)
        den = s + gw * jnp.sum(q * n_state, axis=1, keepdims=True)
        hid = num / jnp.maximum(jnp.abs(den), jnp.exp(-m_t))
        w = jnp.exp(ig - m_t)
        decay = jnp.exp(g - m_t)
        c_out[h] = decay * c_state + _row_to_col(k * w) * v
        n_out[h:h + 1, :] = decay * n_state + w * k
        m_out[h:h + 1, :] = m_t
        y_ref[h:h + 1, :] = (_rms_norm(hid) * gain_ref[h:h + 1, :] * _sigmoid(og)).astype(BF16)


def _mlstm_sample(x, gates, bias, gain, c_state, n_state, m_state, layer_j):
    db = x.shape[0]
    nh, dh = MLSTM_HEADS, MLSTM_DH
    per_b3 = lambda b: (b, 0, 0)
    return pl.pallas_call(
        _mlstm_sample_kernel,
        grid=(db,),
        in_specs=[
            pl.BlockSpec((None, 4 * nh, dh), per_b3),
            pl.BlockSpec((None, 2 * nh, 1), per_b3),
            pl.BlockSpec((2 * nh, 1), lambda b: (0, 0)),
            pl.BlockSpec((None, nh, dh), lambda b: (layer_j, 0, 0)),
            pl.BlockSpec((None, None, nh, dh, dh), lambda b: (layer_j, b, 0, 0, 0)),
            pl.BlockSpec((None, None, nh, dh), lambda b: (layer_j, b, 0, 0)),
            pl.BlockSpec((None, None, nh, 1), lambda b: (layer_j, b, 0, 0)),
        ],
        out_specs=[
            pl.BlockSpec((None, nh, dh), per_b3),
            pl.BlockSpec((None, nh, dh, dh), lambda b: (b, 0, 0, 0)),
            pl.BlockSpec((None, nh, dh), per_b3),
            pl.BlockSpec((None, nh, 1), per_b3),
        ],
        out_shape=[
            jax.ShapeDtypeStruct((db, nh, dh), BF16),
            jax.ShapeDtypeStruct((db, nh, dh, dh), F32),
            jax.ShapeDtypeStruct((db, nh, dh), F32),
            jax.ShapeDtypeStruct((db, nh, 1), F32),
        ],
        compiler_params=_cparams("parallel"),
        name="mlstm_sample",
    )(x, gates, bias, gain, c_state, n_state, m_state)


def _odd_sample_kernel(xh_ref, xr_ref, lb_ref, cos_ref, sin_ref, gh_ref, gr_ref, sh_ref, sr_ref,
                       yh_ref, yr_ref, sh_out, sr_out, *, layer):
    nh = HGRN_HEADS
    lb = _hgrn_lower_bound(lb_ref, layer)
    f = lb + (1.0 - lb) * _sigmoid(xh_ref[nh:2 * nh, :])
    q = _silu(xh_ref[0:nh, :])
    kk = 1.0 - f
    v = xh_ref[2 * nh:3 * nh, :]
    gate = xh_ref[3 * nh:4 * nh, :]
    qk = jnp.sum(q * kk, axis=1, keepdims=True)
    for h in range(nh):
        state = sh_ref[h]
        f_col = _row_to_col(f[h:h + 1, :])
        k_col = _row_to_col(kk[h:h + 1, :])
        v_row = v[h:h + 1, :]
        o = qk[h:h + 1, :] * v_row + _vec_mat(q[h:h + 1, :] * f[h:h + 1, :], state)
        sh_out[h] = f_col * state + k_col * v_row
        yh_ref[h:h + 1, :] = (_rms_norm(o) * gh_ref[h:h + 1, :] * _silu(gate[h:h + 1, :])).astype(BF16)

    nr = RET_HEADS
    cos, sin = cos_ref[...], sin_ref[...]
    qr = _rotate(xr_ref[0:nr, :], cos, sin)
    kr = _rotate(xr_ref[nr:2 * nr, :], cos, sin) * (RET_DK ** -0.5)
    vr = xr_ref[2 * nr:3 * nr, :]
    gr = xr_ref[3 * nr:4 * nr, :]
    qkr = jnp.sum(qr * kr, axis=1, keepdims=True)
    for h in range(nr):
        gamma = math.exp(_ret_log_gamma(h))
        state = sr_ref[h]
        v_row = vr[h:h + 1, :]
        o = qkr[h:h + 1, :] * v_row + _vec_mat(qr[h:h + 1, :], state) * gamma
        sr_out[h] = gamma * state + _row_to_col(kr[h:h + 1, :]) * v_row
        yr_ref[h:h + 1, :] = (_rms_norm(o) * gr_ref[h:h + 1, :] * _silu(gr[h:h + 1, :])).astype(BF16)


def _odd_sample(xh, xr, lb3, cos, sin, gain_h, gain_r, state_h, state_r, layer, layer_j):
    db = xh.shape[0]
    per_b3 = lambda b: (b, 0, 0)
    return pl.pallas_call(
        functools.partial(_odd_sample_kernel, layer=layer),
        grid=(db,),
        in_specs=[
            pl.BlockSpec((None, 4 * HGRN_HEADS, HGRN_DK), per_b3),
            pl.BlockSpec((None, 4 * RET_HEADS, RET_DK), per_b3),
            pl.BlockSpec((DEPTH, HGRN_HEADS, HGRN_DK), lambda b: (0, 0, 0)),
            pl.BlockSpec((1, RET_DK // 2), lambda b: (0, 0)),
            pl.BlockSpec((1, RET_DK // 2), lambda b: (0, 0)),
            pl.BlockSpec((None, HGRN_HEADS, HGRN_DV), lambda b: (layer_j, 0, 0)),
            pl.BlockSpec((None, RET_HEADS, RET_DK), lambda b: (layer_j, 0, 0)),
            pl.BlockSpec((None, None, HGRN_HEADS, HGRN_DK, HGRN_DV), lambda b: (layer_j, b, 0, 0, 0)),
            pl.BlockSpec((None, None, RET_HEADS, RET_DK, RET_DK), lambda b: (layer_j, b, 0, 0, 0)),
        ],
        out_specs=[
            pl.BlockSpec((None, HGRN_HEADS, HGRN_DV), per_b3),
            pl.BlockSpec((None, RET_HEADS, RET_DK), per_b3),
            pl.BlockSpec((None, HGRN_HEADS, HGRN_DK, HGRN_DV), lambda b: (b, 0, 0, 0)),
            pl.BlockSpec((None, RET_HEADS, RET_DK, RET_DK), lambda b: (b, 0, 0, 0)),
        ],
        out_shape=[
            jax.ShapeDtypeStruct((db, HGRN_HEADS, HGRN_DV), BF16),
            jax.ShapeDtypeStruct((db, RET_HEADS, RET_DK), BF16),
            jax.ShapeDtypeStruct((db, HGRN_HEADS, HGRN_DK, HGRN_DV), F32),
            jax.ShapeDtypeStruct((db, RET_HEADS, RET_DK, RET_DK), F32),
        ],
        compiler_params=_cparams("parallel"),
        name="odd_sample",
    )(xh, xr, lb3, cos, sin, gain_h, gain_r, state_h, state_r)


def _repack_even(w):
    gates0 = 4 * HALF
    qb0 = gates0 + 2 * MLSTM_HEADS
    ki0 = qb0 + (EV_MISC - EV_QB)
    wi0 = ki0 + IDX_DIM
    pad = jnp.zeros(w.shape[:-1] + (EV_WIDTH - EV_MISC - IDX_DIM - IDX_HEADS - 2 * MLSTM_HEADS,), w.dtype)
    return jnp.concatenate(
        [w[..., :gates0], w[..., qb0:ki0], w[..., ki0:wi0], w[..., wi0:wi0 + IDX_HEADS],
         w[..., gates0:qb0], pad], axis=-1)


def _rot_tables(pos):
    inv = 1.0 / (RET_THETA ** jnp.linspace(0.0, 1.0, RET_DK // 2, dtype=F32))
    ang = pos.astype(F32)[:, None] * inv[None, :]
    return jnp.cos(ang), jnp.sin(ang)


def kernel(x_prompt, x_sample, state_mlstm_C, state_mlstm_n, state_mlstm_m, cache_k, cache_v, cache_idx_k,
           state_hgrn, state_ret, page_table, p_prompt, p_sample, ln_g, ln_b, w_ffn_up, w_ffn_down,
           w_in_even, b_gate_mlstm, g_mlstm, w_in_odd, hgrn_lb, g_hgrn, g_ret, w_out, w_pe_gate, w_pe_proj):
    batch, seq, _ = x_prompt.shape
    db, dec_seq, _ = x_sample.shape
    assert dec_seq == 1 and all(seq % c == 0 for c in (MLSTM_CHUNK, RET_CHUNK, HGRN_CHUNK))
    n_even = w_in_even.shape[0]
    n_odd = w_in_odd.shape[0]
    n_pages = page_table.shape[1]
    past = n_pages * PAGE_SIZE
    mp = batch * seq

    w_up = w_ffn_up.astype(BF16)
    w_down = w_ffn_down.astype(BF16)
    w_even = _repack_even(w_in_even).astype(BF16)
    w_o = w_out.astype(BF16)
    w_pg = w_pe_gate.astype(BF16)
    w_pp = w_pe_proj.astype(BF16)
    ln_g3 = ln_g.reshape(DEPTH * 3, 1, D_MODEL)
    ln_b3 = ln_b.reshape(DEPTH * 3, 1, D_MODEL)
    pp = p_prompt.reshape(DEPTH, mp, P_DIM)
    ps = p_sample.reshape(DEPTH, db, P_DIM)
    cos_p, sin_p = _rot_tables(jnp.arange(seq))
    cos_s, sin_s = _rot_tables(past + jnp.arange(dec_seq))
    lb3 = hgrn_lb.reshape(DEPTH, 1, HGRN_HEADS * HGRN_DK)
    lb_heads = hgrn_lb.reshape(DEPTH, HGRN_HEADS, HGRN_DK)
    bias_row = jnp.zeros((n_even, 1, LANES), F32).at[:, 0, MISC_IG:MISC_IG + 2 * MLSTM_HEADS].set(b_gate_mlstm)
    bias_col = b_gate_mlstm.reshape(n_even, 2 * MLSTM_HEADS, 1)
    gain_m3 = g_mlstm.reshape(n_even, 1, HALF)
    gain_h3 = g_hgrn.reshape(n_odd, 1, HALF)
    gain_r3 = g_ret.reshape(n_odd, 1, HALF)
    cache_k3 = cache_k.reshape(cache_k.shape[:2] + (PAGE_SIZE * ATT_KV_HEADS, ATT_DH))
    cache_v3 = cache_v.reshape(cache_v.shape[:2] + (PAGE_SIZE * ATT_KV_HEADS, ATT_DH))
    cache_ik_t = jnp.swapaxes(cache_idx_k, 2, 3)
    m_state = state_mlstm_m.reshape(n_even, db, MLSTM_HEADS, 1)

    xp = x_prompt.reshape(mp, D_MODEL)
    xs = x_sample.reshape(db, D_MODEL)
    out_even = {k: [] for k in ("C_p", "C_s", "n_p", "n_s", "m_p", "m_s", "k_p", "k_s", "v_p", "v_s", "ik_p", "ik_s")}
    out_odd = {k: [] for k in ("h_p", "h_s", "r_p", "r_s")}

    for layer in range(DEPTH):
        j = layer // 2
        xp, xp_b = _ffn(xp, w_up, w_down, ln_g3, ln_b3, layer, 0, 0)
        xs, xs_b = _ffn(xs, w_up, w_down, ln_g3, ln_b3, layer, 0, 0)
        if layer % 2 == 0:
            zp = _proj(xp_b, w_even, j)
            zs = _proj(xs_b, w_even, j)
            ya_p, c_p, n_p, m_p = _mlstm_prompt(zp, bias_row[j], gain_m3, batch, seq, j)
            yb_p = _dsa_prompt(zp, batch, seq)

            x4 = zs[:, :4 * HALF].reshape(db, 4 * MLSTM_HEADS, MLSTM_DH)
            gates_s = zs[:, EV_MISC + MISC_IG:EV_MISC + MISC_IG + 2 * MLSTM_HEADS].reshape(db, 2 * MLSTM_HEADS, 1)
            ya_s, c_s, n_s, m_s = _mlstm_sample(x4, gates_s, bias_col[j], g_mlstm.reshape(n_even, MLSTM_HEADS, MLSTM_DH),
                                                state_mlstm_C, state_mlstm_n, m_state, j)
            q_s = zs[:, EV_QB:EV_KB].reshape(db, ATT_HEADS, ATT_DH)
            k_s = zs[:, EV_KB:EV_VB].reshape(db, ATT_KV_HEADS, ATT_DH)
            v_s = zs[:, EV_VB:EV_QI].reshape(db, ATT_KV_HEADS, ATT_DH)
            qi_s = zs[:, EV_QI:EV_MISC].reshape(db, IDX_HEADS, IDX_DIM)
            ki_s = zs[:, EV_MISC + MISC_KI:EV_MISC + MISC_KI + IDX_DIM].reshape(db, 1, IDX_DIM)
            wi_s = zs[:, EV_MISC + MISC_WI:EV_MISC + MISC_WI + IDX_HEADS].reshape(db, IDX_HEADS, 1)
            scores = _dsa_sample_scores(page_table, cache_ik_t, qi_s, wi_s, j)
            select = _dsa_sample_select(scores, qi_s, wi_s, ki_s)
            yb_s = _dsa_sample_attn(page_table, scores, select, q_s, k_s, v_s, cache_k3, cache_v3, j)
            ya_s = ya_s.reshape(db, HALF)
            yb_s = yb_s.reshape(db, HALF)

            out_even["C_p"].append(c_p)
            out_even["C_s"].append(c_s)
            out_even["n_p"].append(n_p.reshape(batch, MLSTM_HEADS, MLSTM_DH))
            out_even["n_s"].append(n_s)
            out_even["m_p"].append(m_p[:, :, 0, 0])
            out_even["m_s"].append(m_s[:, :, 0])
            out_even["k_p"].append(zp[:, EV_KB:EV_VB].reshape(batch, seq, ATT_KV_HEADS, ATT_DH))
            out_even["k_s"].append(k_s.reshape(db, dec_seq, ATT_KV_HEADS, ATT_DH))
            out_even["v_p"].append(zp[:, EV_VB:EV_QI].reshape(batch, seq, ATT_KV_HEADS, ATT_DH))
            out_even["v_s"].append(v_s.reshape(db, dec_seq, ATT_KV_HEADS, ATT_DH))
            out_even["ik_p"].append(zp[:, EV_MISC + MISC_KI:EV_MISC + MISC_KI + IDX_DIM].reshape(batch, seq, IDX_DIM))
            out_even["ik_s"].append(ki_s.reshape(db, dec_seq, IDX_DIM))
        else:
            zp = _proj(xp_b, w_in_odd, j)
            zs = _proj(xs_b, w_in_odd, j)
            ya_p, h_p = _hgrn_prompt(zp, lb3, gain_h3, batch, seq, layer, j)
            yb_p, r_p = _ret_prompt(zp, cos_p, sin_p, gain_r3, batch, seq, j)
            xh = zs[:, :4 * HALF].reshape(db, 4 * HGRN_HEADS, HGRN_DK)
            xr = zs[:, 4 * HALF:].reshape(db, 4 * RET_HEADS, RET_DK)
            ya_s, yb_s, h_s, r_s = _odd_sample(
                xh, xr, lb_heads, cos_s, sin_s, g_hgrn.reshape(n_odd, HGRN_HEADS, HGRN_DV),
                g_ret.reshape(n_odd, RET_HEADS, RET_DK), state_hgrn, state_ret, layer, j)
            ya_s = ya_s.reshape(db, HALF)
            yb_s = yb_s.reshape(db, HALF)
            out_odd["h_p"].append(h_p)
            out_odd["h_s"].append(h_s)
            out_odd["r_p"].append(r_p)
            out_odd["r_s"].append(r_s)

        xp = _outproj(xp, ya_p, yb_p, w_o, ln_g3, ln_b3, layer)
        xs = _outproj(xs, ya_s, yb_s, w_o, ln_g3, ln_b3, layer)
        xp, xp_b = _ffn(xp, w_up, w_down, ln_g3, ln_b3, layer, 1, 2)
        xs, xs_b = _ffn(xs, w_up, w_down, ln_g3, ln_b3, layer, 1, 2)
        xp = _pe(xp, xp_b, pp, w_pg, w_pp, layer)
        xs = _pe(xs, xs_b, ps, w_pg, w_pp, layer)

    def stk(name, table, like):
        return jnp.stack(table[name]).astype(like.dtype)

    return (
        xp.reshape(batch, seq, D_MODEL), xs.reshape(db, dec_seq, D_MODEL),
        stk("C_p", out_even, state_mlstm_C), stk("C_s", out_even, state_mlstm_C),
        stk("n_p", out_even, state_mlstm_n), stk("n_s", out_even, state_mlstm_n),
        stk("m_p", out_even, state_mlstm_m), stk("m_s", out_even, state_mlstm_m),
        stk("k_p", out_even, cache_k), stk("k_s", out_even, cache_k),
        stk("v_p", out_even, cache_v), stk("v_s", out_even, cache_v),
        stk("ik_p", out_even, cache_idx_k), stk("ik_s", out_even, cache_idx_k),
        stk("h_p", out_odd, state_hgrn), stk("h_s", out_odd, state_hgrn),
        stk("r_p", out_odd, state_ret), stk("r_s", out_odd, state_ret),
    )
```

```python
import functools
import math

import jax
import jax.numpy as jnp
from jax import lax
from jax.experimental import pallas as pl
from jax.experimental.pallas import tpu as pltpu

F32 = jnp.float32
BF16 = jnp.bfloat16

D_MODEL = 2048
DEPTH = 2
PAGE_SIZE = 128
HALF = D_MODEL // 2
MLSTM_HEADS = 4
MLSTM_DH = HALF // MLSTM_HEADS
ATT_HEADS = 8
ATT_DH = HALF // ATT_HEADS
ATT_KV_HEADS = 2
ATT_GROUP = ATT_HEADS // ATT_KV_HEADS
IDX_HEADS = 8
IDX_DIM = 64
TOPK_MAX = 256
HGRN_HEADS = 8
HGRN_DK = 128
HGRN_DV = HALF // HGRN_HEADS
RET_HEADS = 4
RET_DK = HALF // RET_HEADS
RET_THETA = 10000.0
D_FF = 5632
P_DIM = 256
ALPHA = (2 * DEPTH) ** 0.25
LN_EPS = 1e-5

LANES = 128
SUBLANES = 8

ZA_Q, ZA_K, ZA_V, ZA_O = 0, HALF, 2 * HALF, 3 * HALF
ZA_WIDTH = 4 * HALF
ZB_Q = 0
ZB_K = ZB_Q + ATT_HEADS * ATT_DH
ZB_V = ZB_K + ATT_KV_HEADS * ATT_DH
ZB_QI = ZB_V + ATT_KV_HEADS * ATT_DH
ZB_WIDTH = ZB_QI + IDX_HEADS * IDX_DIM
MISC_KI, MISC_WI = 0, IDX_DIM
MISC_IG = MISC_WI + IDX_HEADS
MISC_FG = MISC_IG + MLSTM_HEADS

MLSTM_CHUNK = 128
RET_CHUNK = 128
HGRN_CHUNK = 64
HGRN_GROUP = 8
DSA_Q_TILE = 256
DSA_WIDTH_STEP = 512
PAGE_GROUP = 16
VMEM_LIMIT = 56 * 1024 * 1024
INT_MIN = -2 ** 31
INT_MAX = 2 ** 31 - 1
NEG_BIG = -1e30

NT_DIMS = (((1,), (1,)), ((), ()))
TN_DIMS = (((0,), (0,)), ((), ()))


def _cparams(*sem):
    return pltpu.CompilerParams(dimension_semantics=sem, vmem_limit_bytes=VMEM_LIMIT)


def _row_tile(m, largest=512):
    for t in (1024, 512, 256, 128, 64, 32, 16, 8):
        if t <= largest and m % t == 0:
            return t
    raise ValueError(f"row count {m} is not a multiple of 8")


def _col_tile(n):
    for t in (1024, 896, 512, 256, 128):
        if n % t == 0:
            return t
    raise ValueError(f"column count {n} is not a multiple of 128")


def _layer_norm(y, g, b):
    mu = jnp.mean(y, -1, keepdims=True)
    d = y - mu
    var = jnp.mean(d * d, -1, keepdims=True)
    return d * lax.rsqrt(var + LN_EPS) * g + b


def _rms_norm(h):
    return h * lax.rsqrt(jnp.mean(h * h, -1, keepdims=True) + LN_EPS)


def _sigmoid(x):
    return jax.nn.sigmoid(x)


def _silu(x):
    return x * jax.nn.sigmoid(x)


def _log_sigmoid(x):
    return jnp.minimum(x, 0.0) - jnp.log1p(jnp.exp(-jnp.abs(x)))


def _iota(shape, dim):
    return lax.broadcasted_iota(jnp.int32, shape, dim)


def _row_to_col(row):
    n = row.shape[1]
    eye = _iota((n, n), 0) == _iota((n, n), 1)
    return jnp.sum(jnp.where(eye, row, 0.0), axis=1, keepdims=True)


def _bdot(a, b):
    return jnp.dot(a.astype(BF16), b.astype(BF16), preferred_element_type=F32)


def _bdot_nt(a, b):
    return lax.dot_general(a.astype(BF16), b.astype(BF16), NT_DIMS, preferred_element_type=F32)


def _bdot_tn(a, b):
    return lax.dot_general(a.astype(BF16), b.astype(BF16), TN_DIMS, preferred_element_type=F32)


def _ffn_kernel(x_ref, wg_ref, wu_ref, wd_ref, g_ref, b_ref, o_ref, ob_ref, xb_ref, acc_ref):
    j = pl.program_id(1)

    @pl.when(j == 0)
    def _():
        xb_ref[...] = x_ref[...].astype(BF16)
        acc_ref[...] = jnp.zeros_like(acc_ref)

    xb = xb_ref[...]
    hg = jnp.dot(xb, wg_ref[...].astype(BF16), preferred_element_type=F32)
    hu = jnp.dot(xb, wu_ref[...], preferred_element_type=F32)
    act = _silu(hg) * hu
    acc_ref[...] += jnp.dot(act.astype(BF16), wd_ref[...].astype(BF16), preferred_element_type=F32)

    @pl.when(j == pl.num_programs(1) - 1)
    def _():
        y = _layer_norm(ALPHA * x_ref[...] + 0.5 * acc_ref[...], g_ref[...], b_ref[...])
        o_ref[...] = y
        ob_ref[...] = y.astype(BF16)


def _ffn(x, w_gate_up, w_up_b, w_down, ln_g, ln_b, layer, which, ln_idx):
    m = x.shape[0]
    tm, tf = _row_tile(m), 512
    nf = D_FF // tf
    ln_row = layer * 3 + ln_idx
    return pl.pallas_call(
        _ffn_kernel,
        grid=(m // tm, nf),
        in_specs=[
            pl.BlockSpec((tm, D_MODEL), lambda i, j: (i, 0)),
            pl.BlockSpec((None, None, D_MODEL, tf), lambda i, j: (layer, which, 0, j)),
            pl.BlockSpec((None, None, D_MODEL, tf), lambda i, j: (layer, which, 0, j)),
            pl.BlockSpec((None, None, tf, D_MODEL), lambda i, j: (layer, which, j, 0)),
            pl.BlockSpec((None, 1, D_MODEL), lambda i, j: (ln_row, 0, 0)),
            pl.BlockSpec((None, 1, D_MODEL), lambda i, j: (ln_row, 0, 0)),
        ],
        out_specs=[pl.BlockSpec((tm, D_MODEL), lambda i, j: (i, 0))] * 2,
        out_shape=[jax.ShapeDtypeStruct((m, D_MODEL), F32), jax.ShapeDtypeStruct((m, D_MODEL), BF16)],
        scratch_shapes=[pltpu.VMEM((tm, D_MODEL), BF16), pltpu.VMEM((tm, D_MODEL), F32)],
        compiler_params=_cparams("parallel", "arbitrary"),
        name="ffn",
    )(x, w_gate_up, w_up_b, w_down, ln_g, ln_b)


def _proj_kernel(x_ref, w_ref, o_ref, wb_ref):
    @pl.when(pl.program_id(1) == 0)
    def _():
        wb_ref[...] = w_ref[...].astype(BF16)

    o_ref[...] = jnp.dot(x_ref[...], wb_ref[...], preferred_element_type=F32)


def _proj(xb, w, idx):
    m, n = xb.shape[0], w.shape[2]
    tm, tn = _row_tile(m, 1024), _col_tile(n)
    return pl.pallas_call(
        _proj_kernel,
        grid=(n // tn, m // tm),
        in_specs=[
            pl.BlockSpec((tm, D_MODEL), lambda j, i: (i, 0)),
            pl.BlockSpec((None, D_MODEL, tn), lambda j, i: (idx, 0, j)),
        ],
        out_specs=pl.BlockSpec((tm, tn), lambda j, i: (i, j)),
        out_shape=jax.ShapeDtypeStruct((m, n), F32),
        scratch_shapes=[pltpu.VMEM((D_MODEL, tn), BF16)],
        compiler_params=_cparams("parallel", "arbitrary"),
        name="in_proj",
    )(xb, w)


def _proj_t_kernel(x_ref, w_ref, o_ref, wb_ref):
    @pl.when(pl.program_id(1) == 0)
    def _():
        wb_ref[...] = w_ref[...].astype(BF16)

    o_ref[...] = lax.dot_general(x_ref[...], wb_ref[...], NT_DIMS, preferred_element_type=F32)


def _proj_t(xb, w_t, idx, n_cols):
    m = xb.shape[0]
    tm, tn = _row_tile(m, 1024), _col_tile(n_cols)
    return pl.pallas_call(
        _proj_t_kernel,
        grid=(n_cols // tn, m // tm),
        in_specs=[
            pl.BlockSpec((tm, D_MODEL), lambda j, i: (i, 0)),
            pl.BlockSpec((None, tn, D_MODEL), lambda j, i: (idx, j, 0)),
        ],
        out_specs=pl.BlockSpec((tm, tn), lambda j, i: (i, j)),
        out_shape=jax.ShapeDtypeStruct((m, n_cols), F32),
        scratch_shapes=[pltpu.VMEM((tn, D_MODEL), BF16)],
        compiler_params=_cparams("parallel", "arbitrary"),
        name="in_proj_t",
    )(xb, w_t)


def _outproj_kernel(x_ref, ya_ref, yb_ref, wa_ref, wb_ref, g_ref, b_ref, o_ref):
    y = jnp.dot(ya_ref[...], wa_ref[...], preferred_element_type=F32)
    y += jnp.dot(yb_ref[...], wb_ref[...], preferred_element_type=F32)
    o_ref[...] = _layer_norm(ALPHA * x_ref[...] + y, g_ref[...], b_ref[...])


def _outproj(x, ya, yb, w_out, ln_g, ln_b, layer):
    m = x.shape[0]
    tm = _row_tile(m)
    ln_row = layer * 3 + 1
    return pl.pallas_call(
        _outproj_kernel,
        grid=(m // tm,),
        in_specs=[
            pl.BlockSpec((tm, D_MODEL), lambda i: (i, 0)),
            pl.BlockSpec((tm, HALF), lambda i: (i, 0)),
            pl.BlockSpec((tm, HALF), lambda i: (i, 0)),
            pl.BlockSpec((None, HALF, D_MODEL), lambda i: (layer, 0, 0)),
            pl.BlockSpec((None, HALF, D_MODEL), lambda i: (layer, 1, 0)),
            pl.BlockSpec((None, 1, D_MODEL), lambda i: (ln_row, 0, 0)),
            pl.BlockSpec((None, 1, D_MODEL), lambda i: (ln_row, 0, 0)),
        ],
        out_specs=pl.BlockSpec((tm, D_MODEL), lambda i: (i, 0)),
        out_shape=jax.ShapeDtypeStruct((m, D_MODEL), F32),
        compiler_params=_cparams("parallel"),
        name="out_proj",
    )(x, ya, yb, w_out, w_out, ln_g, ln_b)


def _pe_kernel(x_ref, xb_ref, p_ref, wg_ref, wp_ref, o_ref):
    gate = _sigmoid(jnp.dot(xb_ref[...], wg_ref[...], preferred_element_type=F32))
    o_ref[...] = x_ref[...] + gate * jnp.dot(p_ref[...].astype(BF16), wp_ref[...], preferred_element_type=F32)


def _pe(x, xb, p, w_gate, w_proj, layer):
    m = x.shape[0]
    tm = _row_tile(m)
    return pl.pallas_call(
        _pe_kernel,
        grid=(m // tm,),
        in_specs=[
            pl.BlockSpec((tm, D_MODEL), lambda i: (i, 0)),
            pl.BlockSpec((tm, D_MODEL), lambda i: (i, 0)),
            pl.BlockSpec((None, tm, P_DIM), lambda i: (layer, i, 0)),
            pl.BlockSpec((None, D_MODEL, D_MODEL), lambda i: (layer, 0, 0)),
            pl.BlockSpec((None, P_DIM, D_MODEL), lambda i: (layer, 0, 0)),
        ],
        out_specs=pl.BlockSpec((tm, D_MODEL), lambda i: (i, 0)),
        out_shape=jax.ShapeDtypeStruct((m, D_MODEL), F32),
        compiler_params=_cparams("parallel"),
        name="pe_gate",
    )(x, xb, p, w_gate, w_proj)


def _mlstm_prompt_kernel(q_ref, k_ref, v_ref, o_ref, misc_ref, bias_ref, gain_ref,
                         y_ref, c_ref, n_ref, m_ref, m_sc):
    chunk = q_ref.shape[0]
    dh = MLSTM_DH

    @pl.when(pl.program_id(1) == 0)
    def _():
        c_ref[...] = jnp.zeros_like(c_ref)
        n_ref[...] = jnp.zeros_like(n_ref)
        m_sc[...] = jnp.zeros_like(m_sc)

    gates = misc_ref[...] + bias_ref[...]
    ti = _iota((chunk, chunk), 0)
    si = _iota((chunk, chunk), 1)
    eye = ti == si
    causal = si <= ti
    for h in range(MLSTM_HEADS):
        cols = slice(h * dh, (h + 1) * dh)
        ig_col = gates[:, MISC_IG + h:MISC_IG + h + 1]
        lf_col = _log_sigmoid(gates[:, MISC_FG + h:MISC_FG + h + 1])
        lf_row = jnp.sum(jnp.where(eye, lf_col, 0.0), axis=0, keepdims=True)
        ig_row = jnp.sum(jnp.where(eye, ig_col, 0.0), axis=0, keepdims=True)
        b_col = jnp.sum(jnp.where(causal, lf_row, 0.0), axis=1, keepdims=True)
        b_row = jnp.sum(jnp.where(ti <= si, lf_col, 0.0), axis=0, keepdims=True)
        m_prev = m_sc[h][:, :1]
        dlog = jnp.where(causal, b_col - b_row + ig_row, -jnp.inf)
        g_col = b_col + m_prev
        m_t = jnp.maximum(g_col, jnp.max(dlog, axis=1, keepdims=True))
        dw = jnp.exp(dlog - m_t)
        gw = jnp.exp(g_col - m_t)

        q = q_ref[:, cols] * (dh ** -0.5)
        k = k_ref[:, cols]
        v = v_ref[:, cols]
        c_state = c_ref[h]
        n_state = n_ref[h]
        s = _bdot_nt(q, k) * dw
        num = _bdot(s, v) + gw * _bdot(q, c_state)
        den = jnp.sum(s, axis=1, keepdims=True) + gw * jnp.sum(q * n_state, axis=1, keepdims=True)
        hid = num / jnp.maximum(jnp.abs(den), jnp.exp(-m_t))

        b_last = b_col[chunk - 1:chunk, :]
        wlog = b_last - b_col + ig_col
        m_new = jnp.maximum(b_last + m_prev, jnp.max(wlog, axis=0, keepdims=True))
        w_col = jnp.exp(wlog - m_new)
        decay = jnp.exp(b_last + m_prev - m_new)
        kw = k * w_col
        c_ref[h] = decay * c_state + _bdot_tn(kw, v)
        n_ref[h] = decay * n_state + jnp.sum(kw, axis=0, keepdims=True)
        m_row = jnp.broadcast_to(m_new, (1, LANES))
        m_sc[h] = m_row
        m_ref[h] = m_row

        y_ref[:, cols] = (_rms_norm(hid) * gain_ref[:, cols] * _sigmoid(o_ref[:, cols])).astype(BF16)


def _mlstm_prompt(za, zc, bias_row, gain, batch, seq, layer_j):
    chunk = MLSTM_CHUNK
    nc = seq // chunk
    nh, dh = MLSTM_HEADS, MLSTM_DH

    def col(base):
        return lambda b, c: (b * nc + c, base // HALF)

    state = lambda b, c: (b, 0, 0, 0)
    return pl.pallas_call(
        _mlstm_prompt_kernel,
        grid=(batch, nc),
        in_specs=[
            pl.BlockSpec((chunk, HALF), col(ZA_Q)),
            pl.BlockSpec((chunk, HALF), col(ZA_K)),
            pl.BlockSpec((chunk, HALF), col(ZA_V)),
            pl.BlockSpec((chunk, HALF), col(ZA_O)),
            pl.BlockSpec((chunk, LANES), lambda b, c: (b * nc + c, 0)),
            pl.BlockSpec((1, LANES), lambda b, c: (0, 0)),
            pl.BlockSpec((None, 1, HALF), lambda b, c: (layer_j, 0, 0)),
        ],
        out_specs=[
            pl.BlockSpec((chunk, HALF), lambda b, c: (b * nc + c, 0)),
            pl.BlockSpec((None, nh, dh, dh), state),
            pl.BlockSpec((None, nh, 1, dh), state),
            pl.BlockSpec((None, nh, 1, LANES), state),
        ],
        out_shape=[
            jax.ShapeDtypeStruct((batch * seq, HALF), BF16),
            jax.ShapeDtypeStruct((batch, nh, dh, dh), F32),
            jax.ShapeDtypeStruct((batch, nh, 1, dh), F32),
            jax.ShapeDtypeStruct((batch, nh, 1, LANES), F32),
        ],
        scratch_shapes=[pltpu.VMEM((nh, 1, LANES), F32)],
        compiler_params=_cparams("parallel", "arbitrary"),
        name="mlstm_prompt",
    )(za, za, za, za, zc, bias_row, gain)


def _hgrn_lower_bound(lb_ref, layer):
    rows = [lb_ref[j] for j in range(DEPTH)]
    mx = functools.reduce(jnp.maximum, rows)
    e = [jnp.exp(r - mx) for r in rows]
    total = functools.reduce(jnp.add, e)
    acc = jnp.zeros_like(total)
    for j in range(1, layer + 1):
        acc = acc + e[j] / total
    return acc


def _block_mid_rows(b, half):
    rows, width = b.shape
    block = 2 * half
    if block >= SUBLANES:
        parts = [jnp.broadcast_to(b[r + half - 1:r + half, :], (block, width)) for r in range(0, rows, block)]
        return jnp.concatenate(parts, axis=0)
    b3 = b.reshape(rows // SUBLANES, SUBLANES, width)
    sub = _iota(b3.shape, 1)
    out = jnp.broadcast_to(b3[:, half - 1:half, :], b3.shape)
    for r in range(block, SUBLANES, block):
        out = jnp.where(sub >= r, jnp.broadcast_to(b3[:, r + half - 1:r + half, :], b3.shape), out)
    return out.reshape(rows, width)


def _hgrn_prompt_kernel(q_ref, f_ref, i_ref, g_ref, lb_ref, gain_ref, y_ref, s_ref, *, layer):
    chunk = q_ref.shape[0]

    @pl.when(pl.program_id(2) == 0)
    def _():
        s_ref[...] = jnp.zeros_like(s_ref)

    lb = _hgrn_lower_bound(lb_ref, layer)
    f_all = lb + (1.0 - lb) * _sigmoid(f_ref[...])
    lf = jnp.log(f_all)
    halves = [chunk >> (lvl + 1) for lvl in range(chunk.bit_length() - 1)]
    ti = _iota((chunk, chunk), 0)
    si = _iota((chunk, chunk), 1)
    row = _iota((chunk, 1), 0)
    level_mask = []
    second_half = []
    for half in halves:
        mid = (ti // (2 * half)) * (2 * half) + half - 1
        level_mask.append((si <= mid) & (ti > mid) & (si // (2 * half) == ti // (2 * half)))
        second_half.append(row % (2 * half) >= half)
    eye = ti == si
    b_all = jnp.dot(jnp.where(si <= ti, 1.0, 0.0), lf, preferred_element_type=F32,
                    precision=lax.Precision.HIGHEST)

    for h in range(HGRN_GROUP):
        cols = slice(h * HGRN_DK, (h + 1) * HGRN_DK)
        b = b_all[:, cols]
        kk = 1.0 - f_all[:, cols]
        q = _silu(q_ref[:, cols])
        v = i_ref[:, cols]
        b_last = b[chunk - 1:chunk, :]
        state = s_ref[h]

        attn = jnp.where(eye, jnp.sum(q * kk, axis=1, keepdims=True), 0.0)
        for lvl, half in enumerate(halves):
            x = jnp.where(second_half[lvl], q, kk) * jnp.exp(-jnp.abs(b - _block_mid_rows(b, half)))
            attn = attn + jnp.where(level_mask[lvl], _bdot_nt(x, x), 0.0)
        o = _bdot(attn, v) + _bdot(q * jnp.exp(b), state)

        s_ref[h] = _row_to_col(jnp.exp(b_last)) * state + _bdot_tn(kk * jnp.exp(b_last - b), v)
        y_ref[:, cols] = (_rms_norm(o) * gain_ref[:, cols] * _silu(g_ref[:, cols])).astype(BF16)


def _hgrn_prompt(z, lb3, gain, batch, seq, layer, layer_j):
    chunk = HGRN_CHUNK
    nc = seq // chunk
    n_groups = HGRN_HEADS // HGRN_GROUP
    dk = HGRN_GROUP * HGRN_DK

    def col(group):
        return lambda b, h, c: (b * nc + c, group * n_groups + h)

    return pl.pallas_call(
        functools.partial(_hgrn_prompt_kernel, layer=layer),
        grid=(batch, n_groups, nc),
        in_specs=[
            pl.BlockSpec((chunk, dk), col(0)),
            pl.BlockSpec((chunk, dk), col(1)),
            pl.BlockSpec((chunk, dk), col(2)),
            pl.BlockSpec((chunk, dk), col(3)),
            pl.BlockSpec((DEPTH, 1, dk), lambda b, h, c: (0, 0, h)),
            pl.BlockSpec((None, 1, dk), lambda b, h, c: (layer_j, 0, h)),
        ],
        out_specs=[
            pl.BlockSpec((chunk, dk), lambda b, h, c: (b * nc + c, h)),
            pl.BlockSpec((None, HGRN_GROUP, HGRN_DK, HGRN_DV), lambda b, h, c: (b, h, 0, 0)),
        ],
        out_shape=[
            jax.ShapeDtypeStruct((batch * seq, HALF), BF16),
            jax.ShapeDtypeStruct((batch, HGRN_HEADS, HGRN_DK, HGRN_DV), F32),
        ],
        compiler_params=_cparams("parallel", "parallel", "arbitrary"),
        name="hgrn_prompt",
    )(z, z, z, z, lb3, gain)


def _ret_log_gamma(h):
    return math.log1p(-2.0 ** (-5 - h))


def _rotate(x, cos, sin):
    half = x.shape[1] // 2
    x1, x2 = x[:, :half], x[:, half:]
    return jnp.concatenate([x1 * cos - x2 * sin, x1 * sin + x2 * cos], axis=1)


def _ret_prompt_kernel(q_ref, k_ref, v_ref, g_ref, cos_ref, sin_ref, gain_ref, y_ref, s_ref):
    chunk = q_ref.shape[0]

    @pl.when(pl.program_id(1) == 0)
    def _():
        s_ref[...] = jnp.zeros_like(s_ref)

    cos, sin = cos_ref[...], sin_ref[...]
    rel = (_iota((chunk, chunk), 0) - _iota((chunk, chunk), 1)).astype(F32)
    pos = _iota((chunk, 1), 0).astype(F32)
    for h in range(RET_HEADS):
        cols = slice(h * RET_DK, (h + 1) * RET_DK)
        lg = _ret_log_gamma(h)
        qr = _rotate(q_ref[:, cols], cos, sin)
        kr = _rotate(k_ref[:, cols], cos, sin) * (RET_DK ** -0.5)
        v = v_ref[:, cols]
        dmask = jnp.where(rel >= 0.0, jnp.exp(lg * jnp.maximum(rel, 0.0)), 0.0)
        state = s_ref[h]
        o = _bdot(_bdot_nt(qr, kr) * dmask, v) + _bdot(qr, state) * jnp.exp(lg * (pos + 1.0))
        tail = jnp.exp(lg * (chunk - 1.0 - pos))
        s_ref[h] = math.exp(lg * chunk) * state + _bdot_tn(kr * tail, v)
        y_ref[:, cols] = (_rms_norm(o) * gain_ref[:, cols] * _silu(g_ref[:, cols])).astype(BF16)


def _ret_prompt(z, cos, sin, gain, batch, seq, layer_j):
    chunk = RET_CHUNK
    nc = seq // chunk
    dk = RET_DK

    def col(group):
        return lambda b, c: (b * nc + c, 4 + group)

    return pl.pallas_call(
        _ret_prompt_kernel,
        grid=(batch, nc),
        in_specs=[
            pl.BlockSpec((chunk, HALF), col(0)),
            pl.BlockSpec((chunk, HALF), col(1)),
            pl.BlockSpec((chunk, HALF), col(2)),
            pl.BlockSpec((chunk, HALF), col(3)),
            pl.BlockSpec((chunk, dk // 2), lambda b, c: (c, 0)),
            pl.BlockSpec((chunk, dk // 2), lambda b, c: (c, 0)),
            pl.BlockSpec((None, 1, HALF), lambda b, c: (layer_j, 0, 0)),
        ],
        out_specs=[
            pl.BlockSpec((chunk, HALF), lambda b, c: (b * nc + c, 0)),
            pl.BlockSpec((None, RET_HEADS, dk, dk), lambda b, c: (b, 0, 0, 0)),
        ],
        out_shape=[
            jax.ShapeDtypeStruct((batch * seq, HALF), BF16),
            jax.ShapeDtypeStruct((batch, RET_HEADS, dk, dk), F32),
        ],
        compiler_params=_cparams("parallel", "arbitrary"),
        name="ret_prompt",
    )(z, z, z, z, cos, sin, gain)


def _score_key(score):
    bits = lax.bitcast_convert_type(score, jnp.int32)
    return jnp.where(bits < 0, -(bits & 0x7FFFFFFF), bits)


def _count(mask):
    return jnp.sum(jnp.where(mask, 1.0, 0.0), axis=1, keepdims=True)


def _kth_largest_key(count_ge, shape, n_sel):
    def body(it, t_off):
        cand = t_off | lax.shift_left(jnp.int32(1), 31 - it)
        return jnp.where(count_ge(cand ^ INT_MIN) >= n_sel, cand, t_off)

    t_off = lax.fori_loop(0, 32, body, jnp.zeros(shape, jnp.int32))
    return t_off ^ INT_MIN


def _tie_cutoff(count_eq_below, shape, budget, n_bits):
    def body(it, cut):
        cand = cut | lax.shift_left(jnp.int32(1), n_bits - 1 - it)
        return jnp.where(count_eq_below(cand) <= budget, cand, cut)

    return lax.fori_loop(0, n_bits, body, jnp.zeros(shape, jnp.int32))


def _select_top(key, valid, kpos, n_sel, n_idx_bits):
    per_row = (key.shape[0], 1)
    thr = _kth_largest_key(lambda t: _count(key >= t), per_row, n_sel)
    above = key > thr
    tied = key == thr
    n_above = _count(above)
    need_cut = jnp.max(n_above + _count(tied & valid)) > n_sel
    cut = lax.cond(
        need_cut,
        lambda: _tie_cutoff(lambda j: _count(tied & (kpos < j)), per_row, n_sel - n_above, n_idx_bits),
        lambda: jnp.full(per_row, INT_MAX, jnp.int32))
    return valid & (above | (tied & (kpos < cut)))


def _dsa_prompt_body(q_ref, qi_ref, qmisc_ref, kb_ref, vb_ref, kib_ref, y_ref, n_sel, width):
    qblk = q_ref.shape[0]
    ki = kib_ref[:width, :]
    qmisc = qmisc_ref[...]
    score = jnp.zeros((qblk, width), F32)
    for h in range(IDX_HEADS):
        w = qmisc[:, MISC_WI + h:MISC_WI + h + 1] * (IDX_HEADS ** -0.5) * (IDX_DIM ** -0.5)
        score = score + jnp.maximum(_bdot_nt(qi_ref[:, h * IDX_DIM:(h + 1) * IDX_DIM], ki), 0.0) * w

    qpos = pl.program_id(1) * qblk + _iota((qblk, width), 0)
    kpos = _iota((qblk, width), 1)
    valid = kpos <= qpos
    key = jnp.where(valid, _score_key(score), INT_MIN)
    sel = _select_top(key, valid, kpos, n_sel, (width + 1).bit_length())
    bias = jnp.where(sel, 0.0, -jnp.inf)

    exp2_scale = (ATT_DH ** -0.5) * math.log2(math.e)
    for g in range(ATT_KV_HEADS):
        kg = kb_ref[:width, g * ATT_DH:(g + 1) * ATT_DH]
        vg = vb_ref[:width, g * ATT_DH:(g + 1) * ATT_DH]
        for hh in range(ATT_GROUP):
            cols = slice((g * ATT_GROUP + hh) * ATT_DH, (g * ATT_GROUP + hh + 1) * ATT_DH)
            logits = _bdot_nt(q_ref[:, cols], kg) + bias
            e = jnp.exp2((logits - jnp.max(logits, axis=1, keepdims=True)) * exp2_scale)
            y_ref[:, cols] = (_bdot(e, vg) / jnp.sum(e, axis=1, keepdims=True)).astype(BF16)


def _dsa_prompt_kernel(q_ref, k_ref, v_ref, qi_ref, qmisc_ref, kmisc_ref, y_ref, kb_ref, vb_ref, kib_ref, *, n_sel):
    tile = q_ref.shape[0]
    i = pl.program_id(1)

    @pl.when(i == 0)
    def _():
        kb_ref[...] = k_ref[...].astype(BF16)
        vb_ref[...] = v_ref[...].astype(BF16)
        kib_ref[...] = kmisc_ref[:, MISC_KI:MISC_KI + IDX_DIM].astype(BF16)

    seq = k_ref.shape[0]
    step = DSA_WIDTH_STEP if seq % DSA_WIDTH_STEP == 0 else seq
    tiles_per_step = step // tile
    for var in range(seq // step):
        @pl.when((i >= var * tiles_per_step) & (i < (var + 1) * tiles_per_step))
        def _(width=(var + 1) * step):
            _dsa_prompt_body(q_ref, qi_ref, qmisc_ref, kb_ref, vb_ref, kib_ref, y_ref, n_sel, width)


def _dsa_prompt(zb, zc, batch, seq):
    tile = DSA_Q_TILE if seq % DSA_WIDTH_STEP == 0 else seq
    nt = seq // tile
    n_sel = min(TOPK_MAX, seq // 4)
    kv_w = ATT_KV_HEADS * ATT_DH
    qi_w = IDX_HEADS * IDX_DIM
    return pl.pallas_call(
        functools.partial(_dsa_prompt_kernel, n_sel=n_sel),
        grid=(batch, nt),
        in_specs=[
            pl.BlockSpec((tile, HALF), lambda b, i: (b * nt + i, ZB_Q // HALF)),
            pl.BlockSpec((seq, kv_w), lambda b, i: (b, ZB_K // kv_w)),
            pl.BlockSpec((seq, kv_w), lambda b, i: (b, ZB_V // kv_w)),
            pl.BlockSpec((tile, qi_w), lambda b, i: (b * nt + i, ZB_QI // qi_w)),
            pl.BlockSpec((tile, LANES), lambda b, i: (b * nt + i, 0)),
            pl.BlockSpec((seq, LANES), lambda b, i: (b, 0)),
        ],
        out_specs=pl.BlockSpec((tile, HALF), lambda b, i: (b * nt + i, 0)),
        out_shape=jax.ShapeDtypeStruct((batch * seq, HALF), BF16),
        scratch_shapes=[pltpu.VMEM((seq, kv_w), BF16), pltpu.VMEM((seq, kv_w), BF16),
                        pltpu.VMEM((seq, IDX_DIM), BF16)],
        compiler_params=_cparams("parallel", "arbitrary"),
        name="dsa_prompt",
    )(zb, zb, zb, zb, zc, zc)


def _page_specs(block, n_pages, layer_j):
    def spec(g):
        return pl.BlockSpec((None, None) + block,
                            lambda b, s, pt: (layer_j, pt[b * n_pages + s * PAGE_GROUP + g], 0, 0))

    return [spec(g) for g in range(PAGE_GROUP)]


def _dsa_sample_score_kernel(pt_ref, *refs):
    ik_refs = refs[:PAGE_GROUP]
    qi_ref, wi_ref, o_ref = refs[PAGE_GROUP:]
    qi = qi_ref[...]
    w = wi_ref[...] * (IDX_HEADS ** -0.5)
    rows = []
    for ik_ref in ik_refs:
        r = _bdot(qi, ik_ref[...]) * (IDX_DIM ** -0.5)
        rows.append(jnp.sum(jnp.maximum(r, 0.0) * w, axis=0, keepdims=True))
    o_ref[...] = jnp.concatenate(rows, axis=0)


def _dsa_sample_scores(page_table, cache_ik_t, qi, wi, layer_j):
    dec_batch, n_pages = page_table.shape
    assert n_pages % PAGE_GROUP == 0
    grid_spec = pltpu.PrefetchScalarGridSpec(
        num_scalar_prefetch=1,
        grid=(dec_batch, n_pages // PAGE_GROUP),
        in_specs=_page_specs((IDX_DIM, PAGE_SIZE), n_pages, layer_j) + [
            pl.BlockSpec((None, IDX_HEADS, IDX_DIM), lambda b, s, pt: (b, 0, 0)),
            pl.BlockSpec((None, IDX_HEADS, 1), lambda b, s, pt: (b, 0, 0)),
        ],
        out_specs=pl.BlockSpec((None, PAGE_GROUP, PAGE_SIZE), lambda b, s, pt: (b, s, 0)),
    )
    return pl.pallas_call(
        _dsa_sample_score_kernel,
        grid_spec=grid_spec,
        out_shape=jax.ShapeDtypeStruct((dec_batch, n_pages, PAGE_SIZE), F32),
        compiler_params=_cparams("parallel", "arbitrary"),
        name="dsa_sample_scores",
    )(page_table.reshape(-1), *([cache_ik_t] * PAGE_GROUP), qi, wi)


def _dsa_sample_select_kernel(sc_ref, qi_ref, wi_ref, kin_ref, thr_ref, cut_ref, new_ref, *, n_sel):
    past = sc_ref.shape[1] * PAGE_SIZE
    r_new = jnp.sum(qi_ref[...] * kin_ref[...], axis=2, keepdims=True) * (IDX_DIM ** -0.5)
    s_new = jnp.sum(jnp.maximum(r_new, 0.0) * (wi_ref[...] * (IDX_HEADS ** -0.5)), axis=1, keepdims=True)
    key_new = _score_key(s_new)
    key = _score_key(sc_ref[...])
    idx = _iota(key.shape, 1) * PAGE_SIZE + _iota(key.shape, 2)
    per_req = key_new.shape

    def total(mask, mask_new):
        in_page = jnp.sum(jnp.where(mask, 1.0, 0.0), axis=2, keepdims=True)
        return jnp.sum(in_page, axis=1, keepdims=True) + jnp.where(mask_new, 1.0, 0.0)

    thr = _kth_largest_key(lambda t: total(key >= t, key_new >= t), per_req, n_sel)
    above, tied = key > thr, key == thr
    n_above = total(above, key_new > thr)
    need_cut = jnp.max(n_above + total(tied, key_new == thr)) > n_sel
    cut = lax.cond(
        need_cut,
        lambda: _tie_cutoff(lambda j: total(tied & (idx < j), (key_new == thr) & (past < j)),
                            per_req, n_sel - n_above, (past + 2).bit_length()),
        lambda: jnp.full(per_req, INT_MAX, jnp.int32))
    thr_ref[...] = jnp.broadcast_to(thr, thr_ref.shape)
    cut_ref[...] = jnp.broadcast_to(cut, cut_ref.shape)
    new_ref[...] = jnp.broadcast_to(key_new, new_ref.shape)


def _dsa_sample_select(scores, qi, wi, ki_new):
    dec_batch, n_pages, _ = scores.shape
    n_sel = min(TOPK_MAX, (n_pages * PAGE_SIZE + 1) // 4)
    out = jax.ShapeDtypeStruct((dec_batch, 1, LANES), jnp.int32)
    return pl.pallas_call(
        functools.partial(_dsa_sample_select_kernel, n_sel=n_sel),
        out_shape=[out, out, out],
        compiler_params=pltpu.CompilerParams(vmem_limit_bytes=VMEM_LIMIT),
        name="dsa_sample_select",
    )(scores, qi, wi, ki_new)


def _dsa_sample_attn_kernel(pt_ref, sc_ref, thr_ref, cut_ref, new_ref, q_ref, knew_ref, vnew_ref, *refs):
    kc_refs = refs[:PAGE_GROUP]
    vc_refs = refs[PAGE_GROUP:2 * PAGE_GROUP]
    y_ref, m_sc, l_sc, acc_sc = refs[2 * PAGE_GROUP:]
    step = pl.program_id(1)
    past = sc_ref.shape[0] * PAGE_SIZE
    group_w = PAGE_GROUP * PAGE_SIZE * ATT_KV_HEADS

    @pl.when(step == 0)
    def _():
        m_sc[...] = jnp.full_like(m_sc, NEG_BIG)
        l_sc[...] = jnp.zeros_like(l_sc)
        acc_sc[...] = jnp.zeros_like(acc_sc)

    thr = thr_ref[:, :1]
    cut = cut_ref[:, :1]
    scale = ATT_DH ** -0.5
    q = q_ref[...]
    first_group = _iota((ATT_HEADS, 1), 0) < ATT_GROUP

    def by_group(fn):
        return jnp.where(first_group, fn(0), fn(1))

    def online_update(logits, pv):
        m_old = m_sc[...]
        m_new = jnp.maximum(m_old, jnp.max(logits, axis=1, keepdims=True))
        alpha = jnp.exp(m_old - m_new)
        pr = jnp.exp(logits - m_new)
        l_sc[...] = alpha * l_sc[...] + jnp.sum(pr, axis=1, keepdims=True)
        acc_sc[...] = alpha * acc_sc[...] + pv(pr)
        m_sc[...] = m_new

    first_page = pl.multiple_of(step * PAGE_GROUP, PAGE_GROUP)
    key = _score_key(sc_ref[pl.ds(first_page, PAGE_GROUP), :])
    idx = (first_page + _iota(key.shape, 0)) * PAGE_SIZE + _iota(key.shape, 1)
    sel = (key > thr) | ((key == thr) & (idx < cut))
    exp_shape = (PAGE_SIZE, PAGE_SIZE * ATT_KV_HEADS)
    expand = jnp.where(_iota(exp_shape, 1) // ATT_KV_HEADS == _iota(exp_shape, 0), 1.0, 0.0)
    sel_wide = _bdot(jnp.where(sel, 1.0, 0.0), expand)
    sel_row = jnp.concatenate([sel_wide[g:g + 1, :] for g in range(PAGE_GROUP)], axis=1)
    col_head = _iota((ATT_HEADS, group_w), 1) % ATT_KV_HEADS
    row_head = _iota((ATT_HEADS, group_w), 0) // ATT_GROUP
    mask = (sel_row > 0.5) & (col_head == row_head)
    kc = jnp.concatenate([r[...] for r in kc_refs], axis=0)
    vc = jnp.concatenate([r[...] for r in vc_refs], axis=0)
    logits = jnp.where(mask, _bdot_nt(q, kc) * scale, -jnp.inf)
    online_update(logits, lambda pr: _bdot(pr, vc))

    @pl.when(step == pl.num_programs(1) - 1)
    def _():
        key_new = new_ref[:, :1]
        sel_new = (key_new > thr) | ((key_new == thr) & (past < cut))
        k_new = by_group(lambda g: knew_ref[g:g + 1, :])
        v_new = by_group(lambda g: vnew_ref[g:g + 1, :])
        logit_new = jnp.sum(q * k_new, axis=1, keepdims=True) * scale
        logit_new = jnp.where(sel_new, logit_new, -jnp.inf)
        online_update(logit_new, lambda pr: pr * v_new)
        y_ref[...] = (acc_sc[...] / l_sc[...]).astype(BF16)


def _dsa_sample_attn(page_table, scores, select, q, k_new, v_new, cache_k, cache_v, layer_j):
    dec_batch, n_pages = page_table.shape
    per_b = lambda b, s, pt: (b, 0, 0)
    page_specs = _page_specs((PAGE_SIZE * ATT_KV_HEADS, ATT_DH), n_pages, layer_j)
    grid_spec = pltpu.PrefetchScalarGridSpec(
        num_scalar_prefetch=1,
        grid=(dec_batch, n_pages // PAGE_GROUP),
        in_specs=[
            pl.BlockSpec((None, n_pages, PAGE_SIZE), per_b),
            pl.BlockSpec((None, 1, LANES), per_b),
            pl.BlockSpec((None, 1, LANES), per_b),
            pl.BlockSpec((None, 1, LANES), per_b),
            pl.BlockSpec((None, ATT_HEADS, ATT_DH), per_b),
            pl.BlockSpec((None, ATT_KV_HEADS, ATT_DH), per_b),
            pl.BlockSpec((None, ATT_KV_HEADS, ATT_DH), per_b),
        ] + page_specs + page_specs,
        out_specs=pl.BlockSpec((None, ATT_HEADS, ATT_DH), per_b),
        scratch_shapes=[
            pltpu.VMEM((ATT_HEADS, 1), F32),
            pltpu.VMEM((ATT_HEADS, 1), F32),
            pltpu.VMEM((ATT_HEADS, ATT_DH), F32),
        ],
    )
    return pl.pallas_call(
        _dsa_sample_attn_kernel,
        grid_spec=grid_spec,
        out_shape=jax.ShapeDtypeStruct((dec_batch, ATT_HEADS, ATT_DH), BF16),
        compiler_params=_cparams("parallel", "arbitrary"),
        name="dsa_sample_attn",
    )(page_table.reshape(-1), scores, *select, q, k_new, v_new,
      *([cache_k] * PAGE_GROUP), *([cache_v] * PAGE_GROUP))


def _vec_mat(row, mat):
    return jnp.sum(_row_to_col(row) * mat, axis=0, keepdims=True)


def _mlstm_sample_kernel(x_ref, gate_ref, bias_ref, gain_ref, c_ref, n_ref, m_ref,
                         y_ref, c_out, n_out, m_out):
    gates = gate_ref[...] + bias_ref[...]
    for h in range(MLSTM_HEADS):
        q = x_ref[h:h + 1, :] * (MLSTM_DH ** -0.5)
        k = x_ref[MLSTM_HEADS + h:MLSTM_HEADS + h + 1, :]
        v = x_ref[2 * MLSTM_HEADS + h:2 * MLSTM_HEADS + h + 1, :]
        og = x_ref[3 * MLSTM_HEADS + h:3 * MLSTM_HEADS + h + 1, :]
        ig = gates[h:h + 1, :]
        lf = _log_sigmoid(gates[MLSTM_HEADS + h:MLSTM_HEADS + h + 1, :])
        m_prev = m_ref[h:h + 1, :]
        c_state = c_ref[h]
        n_state = n_ref[h:h + 1, :]
        g = lf + m_prev
        m_t = jnp.maximum(g, ig)
        dw = jnp.exp(ig - m_t)
        gw = jnp.exp(g - m_t)
        s = jnp.sum(q * k, axis=1, keepdims=True) * dw
        num = s * v + gw * _vec_mat(q, c_state)
        den = s + gw * jnp.sum(q * n_state, axis=1, keepdims=True)
        hid = num / jnp.maximum(jnp.abs(den), jnp.exp(-m_t))
        w = jnp.exp(ig - m_t)
        decay = jnp.exp(g - m_t)
        c_out[h] = decay * c_state + _row_to_col(k * w) * v
        n_out[h:h + 1, :] = decay * n_state + w * k
        m_out[h:h + 1, :] = m_t
        y_ref[h:h + 1, :] = (_rms_norm(hid) * gain_ref[h:h + 1, :] * _sigmoid(og)).astype(BF16)


def _mlstm_sample(x, gates, bias, gain, c_state, n_state, m_state, layer_j):
    db = x.shape[0]
    nh, dh = MLSTM_HEADS, MLSTM_DH
    per_b3 = lambda b: (b, 0, 0)
    return pl.pallas_call(
        _mlstm_sample_kernel,
        grid=(db,),
        in_specs=[
            pl.BlockSpec((None, 4 * nh, dh), per_b3),
            pl.BlockSpec((None, 2 * nh, 1), per_b3),
            pl.BlockSpec((2 * nh, 1), lambda b: (0, 0)),
            pl.BlockSpec((None, nh, dh), lambda b: (layer_j, 0, 0)),
            pl.BlockSpec((None, None, nh, dh, dh), lambda b: (layer_j, b, 0, 0, 0)),
            pl.BlockSpec((None, None, nh, dh), lambda b: (layer_j, b, 0, 0)),
            pl.BlockSpec((None, None, nh, 1), lambda b: (layer_j, b, 0, 0)),
        ],
        out_specs=[
            pl.BlockSpec((None, nh, dh), per_b3),
            pl.BlockSpec((None, nh, dh, dh), lambda b: (b, 0, 0, 0)),
            pl.BlockSpec((None, nh, dh), per_b3),
            pl.BlockSpec((None, nh, 1), per_b3),
        ],
        out_shape=[
            jax.ShapeDtypeStruct((db, nh, dh), BF16),
            jax.ShapeDtypeStruct((db, nh, dh, dh), F32),
            jax.ShapeDtypeStruct((db, nh, dh), F32),
            jax.ShapeDtypeStruct((db, nh, 1), F32),
        ],
        compiler_params=_cparams("parallel"),
        name="mlstm_sample",
    )(x, gates, bias, gain, c_state, n_state, m_state)


def _odd_sample_kernel(xh_ref, xr_ref, lb_ref, cos_ref, sin_ref, gh_ref, gr_ref, sh_ref, sr_ref,
                       yh_ref, yr_ref, sh_out, sr_out, *, layer):
    nh = HGRN_HEADS
    lb = _hgrn_lower_bound(lb_ref, layer)
    f = lb + (1.0 - lb) * _sigmoid(xh_ref[nh:2 * nh, :])
    q = _silu(xh_ref[0:nh, :])
    kk = 1.0 - f
    v = xh_ref[2 * nh:3 * nh, :]
    gate = xh_ref[3 * nh:4 * nh, :]
    qk = jnp.sum(q * kk, axis=1, keepdims=True)
    for h in range(nh):
        state = sh_ref[h]
        f_col = _row_to_col(f[h:h + 1, :])
        k_col = _row_to_col(kk[h:h + 1, :])
        v_row = v[h:h + 1, :]
        o = qk[h:h + 1, :] * v_row + _vec_mat(q[h:h + 1, :] * f[h:h + 1, :], state)
        sh_out[h] = f_col * state + k_col * v_row
        yh_ref[h:h + 1, :] = (_rms_norm(o) * gh_ref[h:h + 1, :] * _silu(gate[h:h + 1, :])).astype(BF16)

    nr = RET_HEADS
    cos, sin = cos_ref[...], sin_ref[...]
    qr = _rotate(xr_ref[0:nr, :], cos, sin)
    kr = _rotate(xr_ref[nr:2 * nr, :], cos, sin) * (RET_DK ** -0.5)
    vr = xr_ref[2 * nr:3 * nr, :]
    gr = xr_ref[3 * nr:4 * nr, :]
    qkr = jnp.sum(qr * kr, axis=1, keepdims=True)
    for h in range(nr):
        gamma = math.exp(_ret_log_gamma(h))
        state = sr_ref[h]
        v_row = vr[h:h + 1, :]
        o = qkr[h:h + 1, :] * v_row + _vec_mat(qr[h:h + 1, :], state) * gamma
        sr_out[h] = gamma * state + _row_to_col(kr[h:h + 1, :]) * v_row
        yr_ref[h:h + 1, :] = (_rms_norm(o) * gr_ref[h:h + 1, :] * _silu(gr[h:h + 1, :])).astype(BF16)


def _odd_sample(xh, xr, lb3, cos, sin, gain_h, gain_r, state_h, state_r, layer, layer_j):
    db = xh.shape[0]
    per_b3 = lambda b: (b, 0, 0)
    return pl.pallas_call(
        functools.partial(_odd_sample_kernel, layer=layer),
        grid=(db,),
        in_specs=[
            pl.BlockSpec((None, 4 * HGRN_HEADS, HGRN_DK), per_b3),
            pl.BlockSpec((None, 4 * RET_HEADS, RET_DK), per_b3),
            pl.BlockSpec((DEPTH, HGRN_HEADS, HGRN_DK), lambda b: (0, 0, 0)),
            pl.BlockSpec((1, RET_DK // 2), lambda b: (0, 0)),
            pl.BlockSpec((1, RET_DK // 2), lambda b: (0, 0)),
            pl.BlockSpec((None, HGRN_HEADS, HGRN_DV), lambda b: (layer_j, 0, 0)),
            pl.BlockSpec((None, RET_HEADS, RET_DK), lambda b: (layer_j, 0, 0)),
            pl.BlockSpec((None, None, HGRN_HEADS, HGRN_DK, HGRN_DV), lambda b: (layer_j, b, 0, 0, 0)),
            pl.BlockSpec((None, None, RET_HEADS, RET_DK, RET_DK), lambda b: (layer_j, b, 0, 0, 0)),
        ],
        out_specs=[
            pl.BlockSpec((None, HGRN_HEADS, HGRN_DV), per_b3),
            pl.BlockSpec((None, RET_HEADS, RET_DK), per_b3),
            pl.BlockSpec((None, HGRN_HEADS, HGRN_DK, HGRN_DV), lambda b: (b, 0, 0, 0)),
            pl.BlockSpec((None, RET_HEADS, RET_DK, RET_DK), lambda b: (b, 0, 0, 0)),
        ],
        out_shape=[
            jax.ShapeDtypeStruct((db, HGRN_HEADS, HGRN_DV), BF16),
            jax.ShapeDtypeStruct((db, RET_HEADS, RET_DK), BF16),
            jax.ShapeDtypeStruct((db, HGRN_HEADS, HGRN_DK, HGRN_DV), F32),
            jax.ShapeDtypeStruct((db, RET_HEADS, RET_DK, RET_DK), F32),
        ],
        compiler_params=_cparams("parallel"),
        name="odd_sample",
    )(xh, xr, lb3, cos, sin, gain_h, gain_r, state_h, state_r)


def _even_weight_groups(w):
    w_t = jnp.swapaxes(w, 1, 2)
    gates0 = ZA_WIDTH
    qb0 = gates0 + 2 * MLSTM_HEADS
    ki0 = qb0 + ZB_WIDTH
    end = ki0 + IDX_DIM + IDX_HEADS
    pad = jnp.zeros((w.shape[0], LANES - (end - ki0) - (qb0 - gates0), w.shape[1]), w.dtype)
    w_c = jnp.concatenate([w_t[:, ki0:end], w_t[:, gates0:qb0], pad], axis=1)
    return w_t, w_t[:, qb0:ki0], w_c


def _rot_tables(pos):
    inv = 1.0 / (RET_THETA ** jnp.linspace(0.0, 1.0, RET_DK // 2, dtype=F32))
    ang = pos.astype(F32)[:, None] * inv[None, :]
    return jnp.cos(ang), jnp.sin(ang)


def kernel(x_prompt, x_sample, state_mlstm_C, state_mlstm_n, state_mlstm_m, cache_k, cache_v, cache_idx_k,
           state_hgrn, state_ret, page_table, p_prompt, p_sample, ln_g, ln_b, w_ffn_up, w_ffn_down,
           w_in_even, b_gate_mlstm, g_mlstm, w_in_odd, hgrn_lb, g_hgrn, g_ret, w_out, w_pe_gate, w_pe_proj):
    batch, seq, _ = x_prompt.shape
    db, dec_seq, _ = x_sample.shape
    assert dec_seq == 1 and all(seq % c == 0 for c in (MLSTM_CHUNK, RET_CHUNK, HGRN_CHUNK))
    n_even = w_in_even.shape[0]
    n_odd = w_in_odd.shape[0]
    n_pages = page_table.shape[1]
    past = n_pages * PAGE_SIZE
    mp = batch * seq

    w_up_b = w_ffn_up[..., D_FF:].astype(BF16)
    w_even_a, w_even_b, w_even_c = _even_weight_groups(w_in_even)
    w_o = w_out.astype(BF16)
    w_pg = w_pe_gate.astype(BF16)
    w_pp = w_pe_proj.astype(BF16)
    ln_g3 = ln_g.reshape(DEPTH * 3, 1, D_MODEL)
    ln_b3 = ln_b.reshape(DEPTH * 3, 1, D_MODEL)
    pp = p_prompt.reshape(DEPTH, mp, P_DIM)
    ps = p_sample.reshape(DEPTH, db, P_DIM)
    cos_p, sin_p = _rot_tables(jnp.arange(seq))
    cos_s, sin_s = _rot_tables(past + jnp.arange(dec_seq))
    lb3 = hgrn_lb.reshape(DEPTH, 1, HGRN_HEADS * HGRN_DK)
    lb_heads = hgrn_lb.reshape(DEPTH, HGRN_HEADS, HGRN_DK)
    bias_row = jnp.zeros((n_even, 1, LANES), F32).at[:, 0, MISC_IG:MISC_IG + 2 * MLSTM_HEADS].set(b_gate_mlstm)
    bias_col = b_gate_mlstm.reshape(n_even, 2 * MLSTM_HEADS, 1)
    gain_m3 = g_mlstm.reshape(n_even, 1, HALF)
    gain_h3 = g_hgrn.reshape(n_odd, 1, HALF)
    gain_r3 = g_ret.reshape(n_odd, 1, HALF)
    cache_k3 = cache_k.reshape(cache_k.shape[:2] + (PAGE_SIZE * ATT_KV_HEADS, ATT_DH))
    cache_v3 = cache_v.reshape(cache_v.shape[:2] + (PAGE_SIZE * ATT_KV_HEADS, ATT_DH))
    cache_ik_t = jnp.swapaxes(cache_idx_k, 2, 3)
    m_state = state_mlstm_m.reshape(n_even, db, MLSTM_HEADS, 1)

    xp = x_prompt.reshape(mp, D_MODEL)
    xs = x_sample.reshape(db, D_MODEL)
    out_even = {k: [] for k in ("C_p", "C_s", "n_p", "n_s", "m_p", "m_s", "k_p", "k_s", "v_p", "v_s", "ik_p", "ik_s")}
    out_odd = {k: [] for k in ("h_p", "h_s", "r_p", "r_s")}

    for layer in range(DEPTH):
        j = layer // 2
        xp, xp_b = _ffn(xp, w_ffn_up, w_up_b, w_ffn_down, ln_g3, ln_b3, layer, 0, 0)
        xs, xs_b = _ffn(xs, w_ffn_up, w_up_b, w_ffn_down, ln_g3, ln_b3, layer, 0, 0)
        if layer % 2 == 0:
            za_p, za_s = (_proj_t(xb, w_even_a, j, ZA_WIDTH) for xb in (xp_b, xs_b))
            zb_p, zb_s = (_proj_t(xb, w_even_b, j, ZB_WIDTH) for xb in (xp_b, xs_b))
            zc_p, zc_s = (_proj_t(xb, w_even_c, j, LANES) for xb in (xp_b, xs_b))
            ya_p, c_p, n_p, m_p = _mlstm_prompt(za_p, zc_p, bias_row[j], gain_m3, batch, seq, j)
            yb_p = _dsa_prompt(zb_p, zc_p, batch, seq)

            x4 = za_s.reshape(db, 4 * MLSTM_HEADS, MLSTM_DH)
            gates_s = zc_s[:, MISC_IG:MISC_IG + 2 * MLSTM_HEADS].reshape(db, 2 * MLSTM_HEADS, 1)
            ya_s, c_s, n_s, m_s = _mlstm_sample(x4, gates_s, bias_col[j], g_mlstm.reshape(n_even, MLSTM_HEADS, MLSTM_DH),
                                                state_mlstm_C, state_mlstm_n, m_state, j)
            q_s = zb_s[:, ZB_Q:ZB_K].reshape(db, ATT_HEADS, ATT_DH)
            k_s = zb_s[:, ZB_K:ZB_V].reshape(db, ATT_KV_HEADS, ATT_DH)
            v_s = zb_s[:, ZB_V:ZB_QI].reshape(db, ATT_KV_HEADS, ATT_DH)
            qi_s = zb_s[:, ZB_QI:].reshape(db, IDX_HEADS, IDX_DIM)
            ki_s = zc_s[:, MISC_KI:MISC_KI + IDX_DIM].reshape(db, 1, IDX_DIM)
            wi_s = zc_s[:, MISC_WI:MISC_WI + IDX_HEADS].reshape(db, IDX_HEADS, 1)
            scores = _dsa_sample_scores(page_table, cache_ik_t, qi_s, wi_s, j)
            select = _dsa_sample_select(scores, qi_s, wi_s, ki_s)
            yb_s = _dsa_sample_attn(page_table, scores, select, q_s, k_s, v_s, cache_k3, cache_v3, j)
            ya_s = ya_s.reshape(db, HALF)
            yb_s = yb_s.reshape(db, HALF)

            out_even["C_p"].append(c_p)
            out_even["C_s"].append(c_s)
            out_even["n_p"].append(n_p.reshape(batch, MLSTM_HEADS, MLSTM_DH))
            out_even["n_s"].append(n_s)
            out_even["m_p"].append(m_p[:, :, 0, 0])
            out_even["m_s"].append(m_s[:, :, 0])
            out_even["k_p"].append(zb_p[:, ZB_K:ZB_V].reshape(batch, seq, ATT_KV_HEADS, ATT_DH))
            out_even["k_s"].append(k_s.reshape(db, dec_seq, ATT_KV_HEADS, ATT_DH))
            out_even["v_p"].append(zb_p[:, ZB_V:ZB_QI].reshape(batch, seq, ATT_KV_HEADS, ATT_DH))
            out_even["v_s"].append(v_s.reshape(db, dec_seq, ATT_KV_HEADS, ATT_DH))
            out_even["ik_p"].append(zc_p[:, MISC_KI:MISC_KI + IDX_DIM].reshape(batch, seq, IDX_DIM))
            out_even["ik_s"].append(ki_s.reshape(db, dec_seq, IDX_DIM))
        else:
            zp = _proj(xp_b, w_in_odd, j)
            zs = _proj(xs_b, w_in_odd, j)
            ya_p, h_p = _hgrn_prompt(zp, lb3, gain_h3, batch, seq, layer, j)
            yb_p, r_p = _ret_prompt(zp, cos_p, sin_p, gain_r3, batch, seq, j)
            xh = zs[:, :4 * HALF].reshape(db, 4 * HGRN_HEADS, HGRN_DK)
            xr = zs[:, 4 * HALF:].reshape(db, 4 * RET_HEADS, RET_DK)
            ya_s, yb_s, h_s, r_s = _odd_sample(
                xh, xr, lb_heads, cos_s, sin_s, g_hgrn.reshape(n_odd, HGRN_HEADS, HGRN_DV),
                g_ret.reshape(n_odd, RET_HEADS, RET_DK), state_hgrn, state_ret, layer, j)
            ya_s = ya_s.reshape(db, HALF)
            yb_s = yb_s.reshape(db, HALF)
            out_odd["h_p"].append(h_p)
            out_odd["h_s"].append(h_s)
            out_odd["r_p"].append(r_p)
            out_odd["r_s"].append(r_s)

        xp = _outproj(xp, ya_p, yb_p, w_o, ln_g3, ln_b3, layer)
        xs = _outproj(xs, ya_s, yb_s, w_o, ln_g3, ln_b3, layer)
        xp, xp_b = _ffn(xp, w_ffn_up, w_up_b, w_ffn_down, ln_g3, ln_b3, layer, 1, 2)
        xs, xs_b = _ffn(xs, w_ffn_up, w_up_b, w_ffn_down, ln_g3, ln_b3, layer, 1, 2)
        xp = _pe(xp, xp_b, pp, w_pg, w_pp, layer)
        xs = _pe(xs, xs_b, ps, w_pg, w_pp, layer)

    def stk(name, table, like):
        return jnp.stack(table[name]).astype(like.dtype)

    return (
        xp.reshape(batch, seq, D_MODEL), xs.reshape(db, dec_seq, D_MODEL),
        stk("C_p", out_even, state_mlstm_C), stk("C_s", out_even, state_mlstm_C),
        stk("n_p", out_even, state_mlstm_n), stk("n_s", out_even, state_mlstm_n),
        stk("m_p", out_even, state_mlstm_m), stk("m_s", out_even, state_mlstm_m),
        stk("k_p", out_even, cache_k), stk("k_s", out_even, cache_k),
        stk("v_p", out_even, cache_v), stk("v_s", out_even, cache_v),
        stk("ik_p", out_even, cache_idx_k), stk("ik_s", out_even, cache_idx_k),
        stk("h_p", out_odd, state_hgrn), stk("h_s", out_odd, state_hgrn),
        stk("r_p", out_odd, state_ret), stk("r_s", out_odd, state_ret),
    )
```

```python
import functools
import math

import jax
import jax.numpy as jnp
from jax import lax
from jax.experimental import pallas as pl
from jax.experimental.pallas import tpu as pltpu

F32 = jnp.float32
BF16 = jnp.bfloat16

D_MODEL = 2048
DEPTH = 2
PAGE_SIZE = 128
HALF = D_MODEL // 2
MLSTM_HEADS = 4
MLSTM_DH = HALF // MLSTM_HEADS
ATT_HEADS = 8
ATT_DH = HALF // ATT_HEADS
ATT_KV_HEADS = 2
ATT_GROUP = ATT_HEADS // ATT_KV_HEADS
IDX_HEADS = 8
IDX_DIM = 64
TOPK_MAX = 256
HGRN_HEADS = 8
HGRN_DK = 128
HGRN_DV = HALF // HGRN_HEADS
RET_HEADS = 4
RET_DK = HALF // RET_HEADS
RET_THETA = 10000.0
D_FF = 5632
P_DIM = 256
ALPHA = (2 * DEPTH) ** 0.25
LN_EPS = 1e-5

LANES = 128
SUBLANES = 8

ZA_Q, ZA_K, ZA_V, ZA_O = 0, HALF, 2 * HALF, 3 * HALF
ZA_WIDTH = 4 * HALF
ZB_Q = 0
ZB_K = ZB_Q + ATT_HEADS * ATT_DH
ZB_V = ZB_K + ATT_KV_HEADS * ATT_DH
ZB_QI = ZB_V + ATT_KV_HEADS * ATT_DH
ZB_WIDTH = ZB_QI + IDX_HEADS * IDX_DIM
MISC_KI, MISC_WI = 0, IDX_DIM
MISC_IG = MISC_WI + IDX_HEADS
MISC_FG = MISC_IG + MLSTM_HEADS

MLSTM_CHUNK = 128
RET_CHUNK = 128
HGRN_CHUNK = 64
HGRN_GROUP = 8
DSA_Q_TILE = 256
DSA_WIDTH_STEP = 512
PAGE_GROUP = 16
VMEM_LIMIT = 56 * 1024 * 1024
INT_MIN = -2 ** 31
INT_MAX = 2 ** 31 - 1
F32_INF_BITS = 0x7F800000
NEG_BIG = -1e30

NT_DIMS = (((1,), (1,)), ((), ()))
TN_DIMS = (((0,), (0,)), ((), ()))


def _cparams(*sem):
    return pltpu.CompilerParams(dimension_semantics=sem, vmem_limit_bytes=VMEM_LIMIT)


def _row_tile(m, largest=512):
    for t in (1024, 512, 256, 128, 64, 32, 16, 8):
        if t <= largest and m % t == 0:
            return t
    raise ValueError(f"row count {m} is not a multiple of 8")


def _col_tile(n):
    for t in (1024, 896, 512, 256, 128):
        if n % t == 0:
            return t
    raise ValueError(f"column count {n} is not a multiple of 128")


def _layer_norm(y, g, b):
    mu = jnp.mean(y, -1, keepdims=True)
    d = y - mu
    var = jnp.mean(d * d, -1, keepdims=True)
    return d * lax.rsqrt(var + LN_EPS) * g + b


def _rms_norm(h):
    return h * lax.rsqrt(jnp.mean(h * h, -1, keepdims=True) + LN_EPS)


def _sigmoid(x):
    return jax.nn.sigmoid(x)


def _silu(x):
    return x * jax.nn.sigmoid(x)


def _log_sigmoid(x):
    return jnp.minimum(x, 0.0) - jnp.log1p(jnp.exp(-jnp.abs(x)))


def _iota(shape, dim):
    return lax.broadcasted_iota(jnp.int32, shape, dim)


def _row_to_col(row):
    n = row.shape[1]
    eye = _iota((n, n), 0) == _iota((n, n), 1)
    return jnp.sum(jnp.where(eye, row, 0.0), axis=1, keepdims=True)


def _bdot(a, b):
    return jnp.dot(a.astype(BF16), b.astype(BF16), preferred_element_type=F32)


def _bdot_nt(a, b):
    return lax.dot_general(a.astype(BF16), b.astype(BF16), NT_DIMS, preferred_element_type=F32)


def _bdot_tn(a, b):
    return lax.dot_general(a.astype(BF16), b.astype(BF16), TN_DIMS, preferred_element_type=F32)


def _ffn_kernel(x_ref, wg_ref, wu_ref, wd_ref, g_ref, b_ref, o_ref, ob_ref, xb_ref, acc_ref):
    j = pl.program_id(1)

    @pl.when(j == 0)
    def _():
        xb_ref[...] = x_ref[...].astype(BF16)
        acc_ref[...] = jnp.zeros_like(acc_ref)

    xb = xb_ref[...]
    hg = jnp.dot(xb, wg_ref[...], preferred_element_type=F32)
    hu = jnp.dot(xb, wu_ref[...], preferred_element_type=F32)
    act = _silu(hg) * hu
    acc_ref[...] += jnp.dot(act.astype(BF16), wd_ref[...], preferred_element_type=F32)

    @pl.when(j == pl.num_programs(1) - 1)
    def _():
        y = _layer_norm(ALPHA * x_ref[...] + 0.5 * acc_ref[...], g_ref[...], b_ref[...])
        o_ref[...] = y
        ob_ref[...] = y.astype(BF16)


def _ffn(x, w_up, w_down, ln_g, ln_b, layer, which, ln_idx):
    m = x.shape[0]
    tm, tf = _row_tile(m), 512
    nf = D_FF // tf
    ln_row = layer * 3 + ln_idx
    return pl.pallas_call(
        _ffn_kernel,
        grid=(m // tm, nf),
        in_specs=[
            pl.BlockSpec((tm, D_MODEL), lambda i, j: (i, 0)),
            pl.BlockSpec((None, None, D_MODEL, tf), lambda i, j: (layer, which, 0, j)),
            pl.BlockSpec((None, None, D_MODEL, tf), lambda i, j: (layer, which, 0, j + nf)),
            pl.BlockSpec((None, None, tf, D_MODEL), lambda i, j: (layer, which, j, 0)),
            pl.BlockSpec((None, 1, D_MODEL), lambda i, j: (ln_row, 0, 0)),
            pl.BlockSpec((None, 1, D_MODEL), lambda i, j: (ln_row, 0, 0)),
        ],
        out_specs=[pl.BlockSpec((tm, D_MODEL), lambda i, j: (i, 0))] * 2,
        out_shape=[jax.ShapeDtypeStruct((m, D_MODEL), F32), jax.ShapeDtypeStruct((m, D_MODEL), BF16)],
        scratch_shapes=[pltpu.VMEM((tm, D_MODEL), BF16), pltpu.VMEM((tm, D_MODEL), F32)],
        compiler_params=_cparams("parallel", "arbitrary"),
        name="ffn",
    )(x, w_up, w_up, w_down, ln_g, ln_b)


def _proj_kernel(x_ref, w_ref, o_ref, wb_ref):
    @pl.when(pl.program_id(1) == 0)
    def _():
        wb_ref[...] = w_ref[...].astype(BF16)

    o_ref[...] = jnp.dot(x_ref[...], wb_ref[...], preferred_element_type=F32)


def _proj(xb, w, idx):
    m, n = xb.shape[0], w.shape[2]
    tm, tn = _row_tile(m, 1024), _col_tile(n)
    return pl.pallas_call(
        _proj_kernel,
        grid=(n // tn, m // tm),
        in_specs=[
            pl.BlockSpec((tm, D_MODEL), lambda j, i: (i, 0)),
            pl.BlockSpec((None, D_MODEL, tn), lambda j, i: (idx, 0, j)),
        ],
        out_specs=pl.BlockSpec((tm, tn), lambda j, i: (i, j)),
        out_shape=jax.ShapeDtypeStruct((m, n), F32),
        scratch_shapes=[pltpu.VMEM((D_MODEL, tn), BF16)],
        compiler_params=_cparams("parallel", "arbitrary"),
        name="in_proj",
    )(xb, w)


def _proj_t_kernel(x_ref, w_ref, o_ref, wb_ref):
    @pl.when(pl.program_id(1) == 0)
    def _():
        wb_ref[...] = w_ref[...].astype(BF16)

    o_ref[...] = lax.dot_general(x_ref[...], wb_ref[...], NT_DIMS, preferred_element_type=F32)


def _proj_t(xb, w_t, idx, n_cols):
    m = xb.shape[0]
    tm, tn = _row_tile(m, 1024), _col_tile(n_cols)
    return pl.pallas_call(
        _proj_t_kernel,
        grid=(n_cols // tn, m // tm),
        in_specs=[
            pl.BlockSpec((tm, D_MODEL), lambda j, i: (i, 0)),
            pl.BlockSpec((None, tn, D_MODEL), lambda j, i: (idx, j, 0)),
        ],
        out_specs=pl.BlockSpec((tm, tn), lambda j, i: (i, j)),
        out_shape=jax.ShapeDtypeStruct((m, n_cols), F32),
        scratch_shapes=[pltpu.VMEM((tn, D_MODEL), BF16)],
        compiler_params=_cparams("parallel", "arbitrary"),
        name="in_proj_t",
    )(xb, w_t)


def _outproj_kernel(x_ref, ya_ref, yb_ref, wa_ref, wb_ref, g_ref, b_ref, o_ref):
    y = jnp.dot(ya_ref[...], wa_ref[...], preferred_element_type=F32)
    y += jnp.dot(yb_ref[...], wb_ref[...], preferred_element_type=F32)
    o_ref[...] = _layer_norm(ALPHA * x_ref[...] + y, g_ref[...], b_ref[...])


def _outproj(x, ya, yb, w_out, ln_g, ln_b, layer):
    m = x.shape[0]
    tm = _row_tile(m)
    ln_row = layer * 3 + 1
    return pl.pallas_call(
        _outproj_kernel,
        grid=(m // tm,),
        in_specs=[
            pl.BlockSpec((tm, D_MODEL), lambda i: (i, 0)),
            pl.BlockSpec((tm, HALF), lambda i: (i, 0)),
            pl.BlockSpec((tm, HALF), lambda i: (i, 0)),
            pl.BlockSpec((None, HALF, D_MODEL), lambda i: (layer, 0, 0)),
            pl.BlockSpec((None, HALF, D_MODEL), lambda i: (layer, 1, 0)),
            pl.BlockSpec((None, 1, D_MODEL), lambda i: (ln_row, 0, 0)),
            pl.BlockSpec((None, 1, D_MODEL), lambda i: (ln_row, 0, 0)),
        ],
        out_specs=pl.BlockSpec((tm, D_MODEL), lambda i: (i, 0)),
        out_shape=jax.ShapeDtypeStruct((m, D_MODEL), F32),
        compiler_params=_cparams("parallel"),
        name="out_proj",
    )(x, ya, yb, w_out, w_out, ln_g, ln_b)


def _pe_kernel(x_ref, xb_ref, p_ref, wg_ref, wp_ref, o_ref):
    gate = _sigmoid(jnp.dot(xb_ref[...], wg_ref[...], preferred_element_type=F32))
    o_ref[...] = x_ref[...] + gate * jnp.dot(p_ref[...].astype(BF16), wp_ref[...], preferred_element_type=F32)


def _pe(x, xb, p, w_gate, w_proj, layer):
    m = x.shape[0]
    tm = _row_tile(m)
    return pl.pallas_call(
        _pe_kernel,
        grid=(m // tm,),
        in_specs=[
            pl.BlockSpec((tm, D_MODEL), lambda i: (i, 0)),
            pl.BlockSpec((tm, D_MODEL), lambda i: (i, 0)),
            pl.BlockSpec((None, tm, P_DIM), lambda i: (layer, i, 0)),
            pl.BlockSpec((None, D_MODEL, D_MODEL), lambda i: (layer, 0, 0)),
            pl.BlockSpec((None, P_DIM, D_MODEL), lambda i: (layer, 0, 0)),
        ],
        out_specs=pl.BlockSpec((tm, D_MODEL), lambda i: (i, 0)),
        out_shape=jax.ShapeDtypeStruct((m, D_MODEL), F32),
        compiler_params=_cparams("parallel"),
        name="pe_gate",
    )(x, xb, p, w_gate, w_proj)


def _mlstm_prompt_kernel(q_ref, k_ref, v_ref, o_ref, misc_ref, bias_ref, gain_ref,
                         y_ref, c_ref, n_ref, m_ref, m_sc):
    chunk = q_ref.shape[0]
    dh = MLSTM_DH

    @pl.when(pl.program_id(1) == 0)
    def _():
        c_ref[...] = jnp.zeros_like(c_ref)
        n_ref[...] = jnp.zeros_like(n_ref)
        m_sc[...] = jnp.zeros_like(m_sc)

    gates = misc_ref[...] + bias_ref[...]
    ti = _iota((chunk, chunk), 0)
    si = _iota((chunk, chunk), 1)
    eye = ti == si
    causal = si <= ti
    for h in range(MLSTM_HEADS):
        cols = slice(h * dh, (h + 1) * dh)
        ig_col = gates[:, MISC_IG + h:MISC_IG + h + 1]
        lf_col = _log_sigmoid(gates[:, MISC_FG + h:MISC_FG + h + 1])
        lf_row = jnp.sum(jnp.where(eye, lf_col, 0.0), axis=0, keepdims=True)
        ig_row = jnp.sum(jnp.where(eye, ig_col, 0.0), axis=0, keepdims=True)
        b_col = jnp.sum(jnp.where(causal, lf_row, 0.0), axis=1, keepdims=True)
        b_row = jnp.sum(jnp.where(ti <= si, lf_col, 0.0), axis=0, keepdims=True)
        m_prev = m_sc[h][:, :1]
        dlog = jnp.where(causal, b_col - b_row + ig_row, -jnp.inf)
        g_col = b_col + m_prev
        m_t = jnp.maximum(g_col, jnp.max(dlog, axis=1, keepdims=True))
        dw = jnp.exp(dlog - m_t)
        gw = jnp.exp(g_col - m_t)

        q = q_ref[:, cols] * (dh ** -0.5)
        k = k_ref[:, cols]
        v = v_ref[:, cols]
        c_state = c_ref[h]
        n_state = n_ref[h]
        s = _bdot_nt(q, k) * dw
        num = _bdot(s, v) + gw * _bdot(q, c_state)
        den = jnp.sum(s, axis=1, keepdims=True) + gw * jnp.sum(q * n_state, axis=1, keepdims=True)
        hid = num / jnp.maximum(jnp.abs(den), jnp.exp(-m_t))

        b_last = b_col[chunk - 1:chunk, :]
        wlog = b_last - b_col + ig_col
        m_new = jnp.maximum(b_last + m_prev, jnp.max(wlog, axis=0, keepdims=True))
        w_col = jnp.exp(wlog - m_new)
        decay = jnp.exp(b_last + m_prev - m_new)
        kw = k * w_col
        c_ref[h] = decay * c_state + _bdot_tn(kw, v)
        n_ref[h] = decay * n_state + jnp.sum(kw, axis=0, keepdims=True)
        m_row = jnp.broadcast_to(m_new, (1, LANES))
        m_sc[h] = m_row
        m_ref[h] = m_row

        y_ref[:, cols] = (_rms_norm(hid) * gain_ref[:, cols] * _sigmoid(o_ref[:, cols])).astype(BF16)


def _mlstm_prompt(za, zc, bias_row, gain, batch, seq, layer_j):
    chunk = MLSTM_CHUNK
    nc = seq // chunk
    nh, dh = MLSTM_HEADS, MLSTM_DH

    def col(base):
        return lambda b, c: (b * nc + c, base // HALF)

    state = lambda b, c: (b, 0, 0, 0)
    return pl.pallas_call(
        _mlstm_prompt_kernel,
        grid=(batch, nc),
        in_specs=[
            pl.BlockSpec((chunk, HALF), col(ZA_Q)),
            pl.BlockSpec((chunk, HALF), col(ZA_K)),
            pl.BlockSpec((chunk, HALF), col(ZA_V)),
            pl.BlockSpec((chunk, HALF), col(ZA_O)),
            pl.BlockSpec((chunk, LANES), lambda b, c: (b * nc + c, 0)),
            pl.BlockSpec((1, LANES), lambda b, c: (0, 0)),
            pl.BlockSpec((None, 1, HALF), lambda b, c: (layer_j, 0, 0)),
        ],
        out_specs=[
            pl.BlockSpec((chunk, HALF), lambda b, c: (b * nc + c, 0)),
            pl.BlockSpec((None, nh, dh, dh), state),
            pl.BlockSpec((None, nh, 1, dh), state),
            pl.BlockSpec((None, nh, 1, LANES), state),
        ],
        out_shape=[
            jax.ShapeDtypeStruct((batch * seq, HALF), BF16),
            jax.ShapeDtypeStruct((batch, nh, dh, dh), F32),
            jax.ShapeDtypeStruct((batch, nh, 1, dh), F32),
            jax.ShapeDtypeStruct((batch, nh, 1, LANES), F32),
        ],
        scratch_shapes=[pltpu.VMEM((nh, 1, LANES), F32)],
        compiler_params=_cparams("parallel", "arbitrary"),
        name="mlstm_prompt",
    )(za, za, za, za, zc, bias_row, gain)


def _hgrn_lower_bound(lb_ref, layer):
    rows = [lb_ref[j] for j in range(DEPTH)]
    mx = functools.reduce(jnp.maximum, rows)
    e = [jnp.exp(r - mx) for r in rows]
    total = functools.reduce(jnp.add, e)
    acc = jnp.zeros_like(total)
    for j in range(1, layer + 1):
        acc = acc + e[j] / total
    return acc


def _block_mid_rows(b, half):
    rows, width = b.shape
    block = 2 * half
    if block >= SUBLANES:
        parts = [jnp.broadcast_to(b[r + half - 1:r + half, :], (block, width)) for r in range(0, rows, block)]
        return jnp.concatenate(parts, axis=0)
    b3 = b.reshape(rows // SUBLANES, SUBLANES, width)
    sub = _iota(b3.shape, 1)
    out = jnp.broadcast_to(b3[:, half - 1:half, :], b3.shape)
    for r in range(block, SUBLANES, block):
        out = jnp.where(sub >= r, jnp.broadcast_to(b3[:, r + half - 1:r + half, :], b3.shape), out)
    return out.reshape(rows, width)


def _hgrn_prompt_kernel(q_ref, f_ref, i_ref, g_ref, lb_ref, gain_ref, y_ref, s_ref, *, layer):
    chunk = q_ref.shape[0]

    @pl.when(pl.program_id(2) == 0)
    def _():
        s_ref[...] = jnp.zeros_like(s_ref)

    lb = _hgrn_lower_bound(lb_ref, layer)
    f_all = lb + (1.0 - lb) * _sigmoid(f_ref[...])
    lf = jnp.log(f_all)
    halves = [chunk >> (lvl + 1) for lvl in range(chunk.bit_length() - 1)]
    ti = _iota((chunk, chunk), 0)
    si = _iota((chunk, chunk), 1)
    row = _iota((chunk, 1), 0)
    level_mask = []
    second_half = []
    for half in halves:
        mid = (ti // (2 * half)) * (2 * half) + half - 1
        level_mask.append((si <= mid) & (ti > mid) & (si // (2 * half) == ti // (2 * half)))
        second_half.append(row % (2 * half) >= half)
    eye = ti == si
    b_all = jnp.dot(jnp.where(si <= ti, 1.0, 0.0), lf, preferred_element_type=F32,
                    precision=lax.Precision.HIGHEST)

    for h in range(HGRN_GROUP):
        cols = slice(h * HGRN_DK, (h + 1) * HGRN_DK)
        b = b_all[:, cols]
        kk = 1.0 - f_all[:, cols]
        q = _silu(q_ref[:, cols])
        v = i_ref[:, cols]
        b_last = b[chunk - 1:chunk, :]
        state = s_ref[h]

        attn = jnp.where(eye, jnp.sum(q * kk, axis=1, keepdims=True), 0.0)
        for lvl, half in enumerate(halves):
            x = jnp.where(second_half[lvl], q, kk) * jnp.exp(-jnp.abs(b - _block_mid_rows(b, half)))
            attn = attn + jnp.where(level_mask[lvl], _bdot_nt(x, x), 0.0)
        o = _bdot(attn, v) + _bdot(q * jnp.exp(b), state)

        s_ref[h] = _row_to_col(jnp.exp(b_last)) * state + _bdot_tn(kk * jnp.exp(b_last - b), v)
        y_ref[:, cols] = (_rms_norm(o) * gain_ref[:, cols] * _silu(g_ref[:, cols])).astype(BF16)


def _hgrn_prompt(z, lb3, gain, batch, seq, layer, layer_j):
    chunk = HGRN_CHUNK
    nc = seq // chunk
    n_groups = HGRN_HEADS // HGRN_GROUP
    dk = HGRN_GROUP * HGRN_DK

    def col(group):
        return lambda b, h, c: (b * nc + c, group * n_groups + h)

    return pl.pallas_call(
        functools.partial(_hgrn_prompt_kernel, layer=layer),
        grid=(batch, n_groups, nc),
        in_specs=[
            pl.BlockSpec((chunk, dk), col(0)),
            pl.BlockSpec((chunk, dk), col(1)),
            pl.BlockSpec((chunk, dk), col(2)),
            pl.BlockSpec((chunk, dk), col(3)),
            pl.BlockSpec((DEPTH, 1, dk), lambda b, h, c: (0, 0, h)),
            pl.BlockSpec((None, 1, dk), lambda b, h, c: (layer_j, 0, h)),
        ],
        out_specs=[
            pl.BlockSpec((chunk, dk), lambda b, h, c: (b * nc + c, h)),
            pl.BlockSpec((None, HGRN_GROUP, HGRN_DK, HGRN_DV), lambda b, h, c: (b, h, 0, 0)),
        ],
        out_shape=[
            jax.ShapeDtypeStruct((batch * seq, HALF), BF16),
            jax.ShapeDtypeStruct((batch, HGRN_HEADS, HGRN_DK, HGRN_DV), F32),
        ],
        compiler_params=_cparams("parallel", "parallel", "arbitrary"),
        name="hgrn_prompt",
    )(z, z, z, z, lb3, gain)


def _ret_log_gamma(h):
    return math.log1p(-2.0 ** (-5 - h))


def _rotate(x, cos, sin):
    half = x.shape[1] // 2
    x1, x2 = x[:, :half], x[:, half:]
    return jnp.concatenate([x1 * cos - x2 * sin, x1 * sin + x2 * cos], axis=1)


def _ret_prompt_kernel(q_ref, k_ref, v_ref, g_ref, cos_ref, sin_ref, gain_ref, y_ref, s_ref):
    chunk = q_ref.shape[0]

    @pl.when(pl.program_id(1) == 0)
    def _():
        s_ref[...] = jnp.zeros_like(s_ref)

    cos, sin = cos_ref[...], sin_ref[...]
    rel = (_iota((chunk, chunk), 0) - _iota((chunk, chunk), 1)).astype(F32)
    pos = _iota((chunk, 1), 0).astype(F32)
    for h in range(RET_HEADS):
        cols = slice(h * RET_DK, (h + 1) * RET_DK)
        lg = _ret_log_gamma(h)
        qr = _rotate(q_ref[:, cols], cos, sin)
        kr = _rotate(k_ref[:, cols], cos, sin) * (RET_DK ** -0.5)
        v = v_ref[:, cols]
        dmask = jnp.where(rel >= 0.0, jnp.exp(lg * jnp.maximum(rel, 0.0)), 0.0)
        state = s_ref[h]
        o = _bdot(_bdot_nt(qr, kr) * dmask, v) + _bdot(qr, state) * jnp.exp(lg * (pos + 1.0))
        tail = jnp.exp(lg * (chunk - 1.0 - pos))
        s_ref[h] = math.exp(lg * chunk) * state + _bdot_tn(kr * tail, v)
        y_ref[:, cols] = (_rms_norm(o) * gain_ref[:, cols] * _silu(g_ref[:, cols])).astype(BF16)


def _ret_prompt(z, cos, sin, gain, batch, seq, layer_j):
    chunk = RET_CHUNK
    nc = seq // chunk
    dk = RET_DK

    def col(group):
        return lambda b, c: (b * nc + c, 4 + group)

    return pl.pallas_call(
        _ret_prompt_kernel,
        grid=(batch, nc),
        in_specs=[
            pl.BlockSpec((chunk, HALF), col(0)),
            pl.BlockSpec((chunk, HALF), col(1)),
            pl.BlockSpec((chunk, HALF), col(2)),
            pl.BlockSpec((chunk, HALF), col(3)),
            pl.BlockSpec((chunk, dk // 2), lambda b, c: (c, 0)),
            pl.BlockSpec((chunk, dk // 2), lambda b, c: (c, 0)),
            pl.BlockSpec((None, 1, HALF), lambda b, c: (layer_j, 0, 0)),
        ],
        out_specs=[
            pl.BlockSpec((chunk, HALF), lambda b, c: (b * nc + c, 0)),
            pl.BlockSpec((None, RET_HEADS, dk, dk), lambda b, c: (b, 0, 0, 0)),
        ],
        out_shape=[
            jax.ShapeDtypeStruct((batch * seq, HALF), BF16),
            jax.ShapeDtypeStruct((batch, RET_HEADS, dk, dk), F32),
        ],
        compiler_params=_cparams("parallel", "arbitrary"),
        name="ret_prompt",
    )(z, z, z, z, cos, sin, gain)


def _ordered_float(key):
    key = jnp.clip(key, -F32_INF_BITS, F32_INF_BITS)
    bits = jnp.where(key < 0, (-key) | INT_MIN, key)
    return lax.bitcast_convert_type(bits, F32)


def _count(mask):
    return jnp.sum(jnp.where(mask, 1.0, 0.0), axis=1, keepdims=True)


def _kth_largest(count_ge, shape, n_sel):
    def body(it, t_off):
        cand = t_off | lax.shift_left(jnp.int32(1), 31 - it)
        return jnp.where(count_ge(_ordered_float(cand ^ INT_MIN)) >= n_sel, cand, t_off)

    t_off = lax.fori_loop(0, 32, body, jnp.zeros(shape, jnp.int32))
    return _ordered_float(t_off ^ INT_MIN)


def _tie_cutoff(count_eq_below, shape, budget, n_bits):
    def body(it, cut):
        cand = cut | lax.shift_left(jnp.int32(1), n_bits - 1 - it)
        return jnp.where(count_eq_below(cand) <= budget, cand, cut)

    return lax.fori_loop(0, n_bits, body, jnp.zeros(shape, jnp.int32))


def _select_top(score, valid, kpos, n_sel, n_idx_bits):
    per_row = (score.shape[0], 1)
    thr = _kth_largest(lambda t: _count(score >= t), per_row, n_sel)
    above = score > thr
    tied = score == thr
    n_above = _count(above)
    need_cut = jnp.max(n_above + _count(tied & valid)) > n_sel
    cut = lax.cond(
        need_cut,
        lambda: _tie_cutoff(lambda j: _count(tied & (kpos < j)), per_row, n_sel - n_above, n_idx_bits),
        lambda: jnp.full(per_row, INT_MAX, jnp.int32))
    return valid & (above | (tied & (kpos < cut)))


def _dsa_prompt_body(q_ref, qi_ref, qmisc_ref, kb_ref, vb_ref, kib_ref, y_ref, n_sel, width):
    qblk = q_ref.shape[0]
    ki = kib_ref[:width, :]
    qmisc = qmisc_ref[...]
    score = jnp.zeros((qblk, width), F32)
    for h in range(IDX_HEADS):
        w = qmisc[:, MISC_WI + h:MISC_WI + h + 1] * (IDX_HEADS ** -0.5) * (IDX_DIM ** -0.5)
        score = score + jnp.maximum(_bdot_nt(qi_ref[:, h * IDX_DIM:(h + 1) * IDX_DIM], ki), 0.0) * w

    qpos = pl.program_id(1) * qblk + _iota((qblk, width), 0)
    kpos = _iota((qblk, width), 1)
    valid = kpos <= qpos
    sel = _select_top(jnp.where(valid, score, -jnp.inf), valid, kpos, n_sel, (width + 1).bit_length())
    bias = jnp.where(sel, 0.0, -jnp.inf)

    exp2_scale = (ATT_DH ** -0.5) * math.log2(math.e)
    for g in range(ATT_KV_HEADS):
        kg = kb_ref[:width, g * ATT_DH:(g + 1) * ATT_DH]
        vg = vb_ref[:width, g * ATT_DH:(g + 1) * ATT_DH]
        for hh in range(ATT_GROUP):
            cols = slice((g * ATT_GROUP + hh) * ATT_DH, (g * ATT_GROUP + hh + 1) * ATT_DH)
            logits = _bdot_nt(q_ref[:, cols], kg) + bias
            e = jnp.exp2((logits - jnp.max(logits, axis=1, keepdims=True)) * exp2_scale)
            y_ref[:, cols] = (_bdot(e, vg) / jnp.sum(e, axis=1, keepdims=True)).astype(BF16)


def _dsa_prompt_kernel(q_ref, k_ref, v_ref, qi_ref, qmisc_ref, kmisc_ref, y_ref, kb_ref, vb_ref, kib_ref, *, n_sel):
    tile = q_ref.shape[0]
    i = pl.program_id(1)

    @pl.when(i == 0)
    def _():
        kb_ref[...] = k_ref[...].astype(BF16)
        vb_ref[...] = v_ref[...].astype(BF16)
        kib_ref[...] = kmisc_ref[:, MISC_KI:MISC_KI + IDX_DIM].astype(BF16)

    seq = k_ref.shape[0]
    step = DSA_WIDTH_STEP if seq % DSA_WIDTH_STEP == 0 else seq
    tiles_per_step = step // tile
    for var in range(seq // step):
        @pl.when((i >= var * tiles_per_step) & (i < (var + 1) * tiles_per_step))
        def _(width=(var + 1) * step):
            _dsa_prompt_body(q_ref, qi_ref, qmisc_ref, kb_ref, vb_ref, kib_ref, y_ref, n_sel, width)


def _dsa_prompt(zb, zc, batch, seq):
    tile = DSA_Q_TILE if seq % DSA_WIDTH_STEP == 0 else seq
    nt = seq // tile
    n_sel = min(TOPK_MAX, seq // 4)
    kv_w = ATT_KV_HEADS * ATT_DH
    qi_w = IDX_HEADS * IDX_DIM
    return pl.pallas_call(
        functools.partial(_dsa_prompt_kernel, n_sel=n_sel),
        grid=(batch, nt),
        in_specs=[
            pl.BlockSpec((tile, HALF), lambda b, i: (b * nt + i, ZB_Q // HALF)),
            pl.BlockSpec((seq, kv_w), lambda b, i: (b, ZB_K // kv_w)),
            pl.BlockSpec((seq, kv_w), lambda b, i: (b, ZB_V // kv_w)),
            pl.BlockSpec((tile, qi_w), lambda b, i: (b * nt + i, ZB_QI // qi_w)),
            pl.BlockSpec((tile, LANES), lambda b, i: (b * nt + i, 0)),
            pl.BlockSpec((seq, LANES), lambda b, i: (b, 0)),
        ],
        out_specs=pl.BlockSpec((tile, HALF), lambda b, i: (b * nt + i, 0)),
        out_shape=jax.ShapeDtypeStruct((batch * seq, HALF), BF16),
        scratch_shapes=[pltpu.VMEM((seq, kv_w), BF16), pltpu.VMEM((seq, kv_w), BF16),
                        pltpu.VMEM((seq, IDX_DIM), BF16)],
        compiler_params=_cparams("parallel", "arbitrary"),
        name="dsa_prompt",
    )(zb, zb, zb, zb, zc, zc)


def _page_specs(block, n_pages, layer_j):
    def spec(g):
        return pl.BlockSpec((None, None) + block,
                            lambda b, s, pt: (layer_j, pt[b * n_pages + s * PAGE_GROUP + g], 0, 0))

    return [spec(g) for g in range(PAGE_GROUP)]


def _dsa_sample_score_kernel(pt_ref, *refs):
    ik_refs = refs[:PAGE_GROUP]
    qi_ref, wi_ref, o_ref = refs[PAGE_GROUP:]
    qi = qi_ref[...]
    w = wi_ref[...] * (IDX_HEADS ** -0.5)
    rows = []
    for ik_ref in ik_refs:
        r = _bdot(qi, ik_ref[...]) * (IDX_DIM ** -0.5)
        rows.append(jnp.sum(jnp.maximum(r, 0.0) * w, axis=0, keepdims=True))
    o_ref[...] = jnp.concatenate(rows, axis=0)


def _dsa_sample_scores(page_table, cache_ik_t, qi, wi, layer_j):
    dec_batch, n_pages = page_table.shape
    assert n_pages % PAGE_GROUP == 0
    grid_spec = pltpu.PrefetchScalarGridSpec(
        num_scalar_prefetch=1,
        grid=(dec_batch, n_pages // PAGE_GROUP),
        in_specs=_page_specs((IDX_DIM, PAGE_SIZE), n_pages, layer_j) + [
            pl.BlockSpec((None, IDX_HEADS, IDX_DIM), lambda b, s, pt: (b, 0, 0)),
            pl.BlockSpec((None, IDX_HEADS, 1), lambda b, s, pt: (b, 0, 0)),
        ],
        out_specs=pl.BlockSpec((None, PAGE_GROUP, PAGE_SIZE), lambda b, s, pt: (b, s, 0)),
    )
    return pl.pallas_call(
        _dsa_sample_score_kernel,
        grid_spec=grid_spec,
        out_shape=jax.ShapeDtypeStruct((dec_batch, n_pages, PAGE_SIZE), F32),
        compiler_params=_cparams("parallel", "arbitrary"),
        name="dsa_sample_scores",
    )(page_table.reshape(-1), *([cache_ik_t] * PAGE_GROUP), qi, wi)


def _dsa_sample_select_kernel(sc_ref, qi_ref, wi_ref, kin_ref, thr_ref, cut_ref, new_ref, *, n_sel):
    past = sc_ref.shape[1] * PAGE_SIZE
    r_new = jnp.sum(qi_ref[...] * kin_ref[...], axis=2, keepdims=True) * (IDX_DIM ** -0.5)
    key_new = jnp.sum(jnp.maximum(r_new, 0.0) * (wi_ref[...] * (IDX_HEADS ** -0.5)), axis=1, keepdims=True)
    key = sc_ref[...]
    idx = _iota(key.shape, 1) * PAGE_SIZE + _iota(key.shape, 2)
    per_req = key_new.shape

    def total(mask, mask_new):
        in_page = jnp.sum(jnp.where(mask, 1.0, 0.0), axis=2, keepdims=True)
        return jnp.sum(in_page, axis=1, keepdims=True) + jnp.where(mask_new, 1.0, 0.0)

    thr = _kth_largest(lambda t: total(key >= t, key_new >= t), per_req, n_sel)
    above, tied = key > thr, key == thr
    n_above = total(above, key_new > thr)
    need_cut = jnp.max(n_above + total(tied, key_new == thr)) > n_sel
    cut = lax.cond(
        need_cut,
        lambda: _tie_cutoff(lambda j: total(tied & (idx < j), (key_new == thr) & (past < j)),
                            per_req, n_sel - n_above, (past + 2).bit_length()),
        lambda: jnp.full(per_req, INT_MAX, jnp.int32))
    thr_ref[...] = jnp.broadcast_to(thr, thr_ref.shape)
    cut_ref[...] = jnp.broadcast_to(cut, cut_ref.shape)
    new_ref[...] = jnp.broadcast_to(key_new, new_ref.shape)


def _dsa_sample_select(scores, qi, wi, ki_new):
    dec_batch, n_pages, _ = scores.shape
    n_sel = min(TOPK_MAX, (n_pages * PAGE_SIZE + 1) // 4)
    out = lambda dtype: jax.ShapeDtypeStruct((dec_batch, 1, LANES), dtype)
    return pl.pallas_call(
        functools.partial(_dsa_sample_select_kernel, n_sel=n_sel),
        out_shape=[out(F32), out(jnp.int32), out(F32)],
        compiler_params=pltpu.CompilerParams(vmem_limit_bytes=VMEM_LIMIT),
        name="dsa_sample_select",
    )(scores, qi, wi, ki_new)


def _dsa_sample_attn_kernel(pt_ref, sc_ref, thr_ref, cut_ref, new_ref, q_ref, knew_ref, vnew_ref, *refs):
    kc_refs = refs[:PAGE_GROUP]
    vc_refs = refs[PAGE_GROUP:2 * PAGE_GROUP]
    y_ref, m_sc, l_sc, acc_sc = refs[2 * PAGE_GROUP:]
    step = pl.program_id(1)
    past = sc_ref.shape[0] * PAGE_SIZE
    group_w = PAGE_GROUP * PAGE_SIZE * ATT_KV_HEADS

    @pl.when(step == 0)
    def _():
        m_sc[...] = jnp.full_like(m_sc, NEG_BIG)
        l_sc[...] = jnp.zeros_like(l_sc)
        acc_sc[...] = jnp.zeros_like(acc_sc)

    thr = thr_ref[:, :1]
    cut = cut_ref[:, :1]
    scale = ATT_DH ** -0.5
    q = q_ref[...]
    first_group = _iota((ATT_HEADS, 1), 0) < ATT_GROUP

    def by_group(fn):
        return jnp.where(first_group, fn(0), fn(1))

    def online_update(logits, pv):
        m_old = m_sc[...]
        m_new = jnp.maximum(m_old, jnp.max(logits, axis=1, keepdims=True))
        alpha = jnp.exp(m_old - m_new)
        pr = jnp.exp(logits - m_new)
        l_sc[...] = alpha * l_sc[...] + jnp.sum(pr, axis=1, keepdims=True)
        acc_sc[...] = alpha * acc_sc[...] + pv(pr)
        m_sc[...] = m_new

    first_page = pl.multiple_of(step * PAGE_GROUP, PAGE_GROUP)
    key = sc_ref[pl.ds(first_page, PAGE_GROUP), :]
    idx = (first_page + _iota(key.shape, 0)) * PAGE_SIZE + _iota(key.shape, 1)
    sel = (key > thr) | ((key == thr) & (idx < cut))
    exp_shape = (PAGE_SIZE, PAGE_SIZE * ATT_KV_HEADS)
    expand = jnp.where(_iota(exp_shape, 1) // ATT_KV_HEADS == _iota(exp_shape, 0), 1.0, 0.0)
    sel_wide = _bdot(jnp.where(sel, 1.0, 0.0), expand)
    sel_row = jnp.concatenate([sel_wide[g:g + 1, :] for g in range(PAGE_GROUP)], axis=1)
    col_head = _iota((ATT_HEADS, group_w), 1) % ATT_KV_HEADS
    row_head = _iota((ATT_HEADS, group_w), 0) // ATT_GROUP
    mask = (sel_row > 0.5) & (col_head == row_head)
    kc = jnp.concatenate([r[...] for r in kc_refs], axis=0)
    vc = jnp.concatenate([r[...] for r in vc_refs], axis=0)
    logits = jnp.where(mask, _bdot_nt(q, kc) * scale, -jnp.inf)
    online_update(logits, lambda pr: _bdot(pr, vc))

    @pl.when(step == pl.num_programs(1) - 1)
    def _():
        key_new = new_ref[:, :1]
        sel_new = (key_new > thr) | ((key_new == thr) & (past < cut))
        k_new = by_group(lambda g: knew_ref[g:g + 1, :])
        v_new = by_group(lambda g: vnew_ref[g:g + 1, :])
        logit_new = jnp.sum(q * k_new, axis=1, keepdims=True) * scale
        logit_new = jnp.where(sel_new, logit_new, -jnp.inf)
        online_update(logit_new, lambda pr: pr * v_new)
        y_ref[...] = (acc_sc[...] / l_sc[...]).astype(BF16)


def _dsa_sample_attn(page_table, scores, select, q, k_new, v_new, cache_k, cache_v, layer_j):
    dec_batch, n_pages = page_table.shape
    per_b = lambda b, s, pt: (b, 0, 0)
    page_specs = _page_specs((PAGE_SIZE * ATT_KV_HEADS, ATT_DH), n_pages, layer_j)
    grid_spec = pltpu.PrefetchScalarGridSpec(
        num_scalar_prefetch=1,
        grid=(dec_batch, n_pages // PAGE_GROUP),
        in_specs=[
            pl.BlockSpec((None, n_pages, PAGE_SIZE), per_b),
            pl.BlockSpec((None, 1, LANES), per_b),
            pl.BlockSpec((None, 1, LANES), per_b),
            pl.BlockSpec((None, 1, LANES), per_b),
            pl.BlockSpec((None, ATT_HEADS, ATT_DH), per_b),
            pl.BlockSpec((None, ATT_KV_HEADS, ATT_DH), per_b),
            pl.BlockSpec((None, ATT_KV_HEADS, ATT_DH), per_b),
        ] + page_specs + page_specs,
        out_specs=pl.BlockSpec((None, ATT_HEADS, ATT_DH), per_b),
        scratch_shapes=[
            pltpu.VMEM((ATT_HEADS, 1), F32),
            pltpu.VMEM((ATT_HEADS, 1), F32),
            pltpu.VMEM((ATT_HEADS, ATT_DH), F32),
        ],
    )
    return pl.pallas_call(
        _dsa_sample_attn_kernel,
        grid_spec=grid_spec,
        out_shape=jax.ShapeDtypeStruct((dec_batch, ATT_HEADS, ATT_DH), BF16),
        compiler_params=_cparams("parallel", "arbitrary"),
        name="dsa_sample_attn",
    )(page_table.reshape(-1), scores, *select, q, k_new, v_new,
      *([cache_k] * PAGE_GROUP), *([cache_v] * PAGE_GROUP))


def _vec_mat(row, mat):
    return jnp.sum(_row_to_col(row) * mat, axis=0, keepdims=True)


def _mlstm_sample_kernel(x_ref, gate_ref, bias_ref, gain_ref, c_ref, n_ref, m_ref,
                         y_ref, c_out, n_out, m_out):
    gates = gate_ref[...] + bias_ref[...]
    for h in range(MLSTM_HEADS):
        q = x_ref[h:h + 1, :] * (MLSTM_DH ** -0.5)
        k = x_ref[MLSTM_HEADS + h:MLSTM_HEADS + h + 1, :]
        v = x_ref[2 * MLSTM_HEADS + h:2 * MLSTM_HEADS + h + 1, :]
        og = x_ref[3 * MLSTM_HEADS + h:3 * MLSTM_HEADS + h + 1, :]
        ig = gates[h:h + 1, :]
        lf = _log_sigmoid(gates[MLSTM_HEADS + h:MLSTM_HEADS + h + 1, :])
        m_prev = m_ref[h:h + 1, :]
        c_state = c_ref[h]
        n_state = n_ref[h:h + 1, :]
        g = lf + m_prev
        m_t = jnp.maximum(g, ig)
        dw = jnp.exp(ig - m_t)
        gw = jnp.exp(g - m_t)
        s = jnp.sum(q * k, axis=1, keepdims=True) * dw
        num = s * v + gw * _vec_mat(q, c_state)
        den = s + gw * jnp.sum(q * n_state, axis=1, keepdims=True)
        hid = num / jnp.maximum(jnp.abs(den), jnp.exp(-m_t))
        w = jnp.exp(ig - m_t)
        decay = jnp.exp(g - m_t)
        c_out[h] = decay * c_state + _row_to_col(k * w) * v
        n_out[h:h + 1, :] = decay * n_state + w * k
        m_out[h:h + 1, :] = m_t
        y_ref[h:h + 1, :] = (_rms_norm(hid) * gain_ref[h:h + 1, :] * _sigmoid(og)).astype(BF16)


def _mlstm_sample(x, gates, bias, gain, c_state, n_state, m_state, layer_j):
    db = x.shape[0]
    nh, dh = MLSTM_HEADS, MLSTM_DH
    per_b3 = lambda b: (b, 0, 0)
    return pl.pallas_call(
        _mlstm_sample_kernel,
        grid=(db,),
        in_specs=[
            pl.BlockSpec((None, 4 * nh, dh), per_b3),
            pl.BlockSpec((None, 2 * nh, 1), per_b3),
            pl.BlockSpec((2 * nh, 1), lambda b: (0, 0)),
            pl.BlockSpec((None, nh, dh), lambda b: (layer_j, 0, 0)),
            pl.BlockSpec((None, None, nh, dh, dh), lambda b: (layer_j, b, 0, 0, 0)),
            pl.BlockSpec((None, None, nh, dh), lambda b: (layer_j, b, 0, 0)),
            pl.BlockSpec((None, None, nh, 1), lambda b: (layer_j, b, 0, 0)),
        ],
        out_specs=[
            pl.BlockSpec((None, nh, dh), per_b3),
            pl.BlockSpec((None, nh, dh, dh), lambda b: (b, 0, 0, 0)),
            pl.BlockSpec((None, nh, dh), per_b3),
            pl.BlockSpec((None, nh, 1), per_b3),
        ],
        out_shape=[
            jax.ShapeDtypeStruct((db, nh, dh), BF16),
            jax.ShapeDtypeStruct((db, nh, dh, dh), F32),
            jax.ShapeDtypeStruct((db, nh, dh), F32),
            jax.ShapeDtypeStruct((db, nh, 1), F32),
        ],
        compiler_params=_cparams("parallel"),
        name="mlstm_sample",
    )(x, gates, bias, gain, c_state, n_state, m_state)


def _odd_sample_kernel(xh_ref, xr_ref, lb_ref, cos_ref, sin_ref, gh_ref, gr_ref, sh_ref, sr_ref,
                       yh_ref, yr_ref, sh_out, sr_out, *, layer):
    nh = HGRN_HEADS
    lb = _hgrn_lower_bound(lb_ref, layer)
    f = lb + (1.0 - lb) * _sigmoid(xh_ref[nh:2 * nh, :])
    q = _silu(xh_ref[0:nh, :])
    kk = 1.0 - f
    v = xh_ref[2 * nh:3 * nh, :]
    gate = xh_ref[3 * nh:4 * nh, :]
    qk = jnp.sum(q * kk, axis=1, keepdims=True)
    for h in range(nh):
        state = sh_ref[h]
        f_col = _row_to_col(f[h:h + 1, :])
        k_col = _row_to_col(kk[h:h + 1, :])
        v_row = v[h:h + 1, :]
        o = qk[h:h + 1, :] * v_row + _vec_mat(q[h:h + 1, :] * f[h:h + 1, :], state)
        sh_out[h] = f_col * state + k_col * v_row
        yh_ref[h:h + 1, :] = (_rms_norm(o) * gh_ref[h:h + 1, :] * _silu(gate[h:h + 1, :])).astype(BF16)

    nr = RET_HEADS
    cos, sin = cos_ref[...], sin_ref[...]
    qr = _rotate(xr_ref[0:nr, :], cos, sin)
    kr = _rotate(xr_ref[nr:2 * nr, :], cos, sin) * (RET_DK ** -0.5)
    vr = xr_ref[2 * nr:3 * nr, :]
    gr = xr_ref[3 * nr:4 * nr, :]
    qkr = jnp.sum(qr * kr, axis=1, keepdims=True)
    for h in range(nr):
        gamma = math.exp(_ret_log_gamma(h))
        state = sr_ref[h]
        v_row = vr[h:h + 1, :]
        o = qkr[h:h + 1, :] * v_row + _vec_mat(qr[h:h + 1, :], state) * gamma
        sr_out[h] = gamma * state + _row_to_col(kr[h:h + 1, :]) * v_row
        yr_ref[h:h + 1, :] = (_rms_norm(o) * gr_ref[h:h + 1, :] * _silu(gr[h:h + 1, :])).astype(BF16)


def _odd_sample(xh, xr, lb3, cos, sin, gain_h, gain_r, state_h, state_r, layer, layer_j):
    db = xh.shape[0]
    per_b3 = lambda b: (b, 0, 0)
    return pl.pallas_call(
        functools.partial(_odd_sample_kernel, layer=layer),
        grid=(db,),
        in_specs=[
            pl.BlockSpec((None, 4 * HGRN_HEADS, HGRN_DK), per_b3),
            pl.BlockSpec((None, 4 * RET_HEADS, RET_DK), per_b3),
            pl.BlockSpec((DEPTH, HGRN_HEADS, HGRN_DK), lambda b: (0, 0, 0)),
            pl.BlockSpec((1, RET_DK // 2), lambda b: (0, 0)),
            pl.BlockSpec((1, RET_DK // 2), lambda b: (0, 0)),
            pl.BlockSpec((None, HGRN_HEADS, HGRN_DV), lambda b: (layer_j, 0, 0)),
            pl.BlockSpec((None, RET_HEADS, RET_DK), lambda b: (layer_j, 0, 0)),
            pl.BlockSpec((None, None, HGRN_HEADS, HGRN_DK, HGRN_DV), lambda b: (layer_j, b, 0, 0, 0)),
            pl.BlockSpec((None, None, RET_HEADS, RET_DK, RET_DK), lambda b: (layer_j, b, 0, 0, 0)),
        ],
        out_specs=[
            pl.BlockSpec((None, HGRN_HEADS, HGRN_DV), per_b3),
            pl.BlockSpec((None, RET_HEADS, RET_DK), per_b3),
            pl.BlockSpec((None, HGRN_HEADS, HGRN_DK, HGRN_DV), lambda b: (b, 0, 0, 0)),
            pl.BlockSpec((None, RET_HEADS, RET_DK, RET_DK), lambda b: (b, 0, 0, 0)),
        ],
        out_shape=[
            jax.ShapeDtypeStruct((db, HGRN_HEADS, HGRN_DV), BF16),
            jax.ShapeDtypeStruct((db, RET_HEADS, RET_DK), BF16),
            jax.ShapeDtypeStruct((db, HGRN_HEADS, HGRN_DK, HGRN_DV), F32),
            jax.ShapeDtypeStruct((db, RET_HEADS, RET_DK, RET_DK), F32),
        ],
        compiler_params=_cparams("parallel"),
        name="odd_sample",
    )(xh, xr, lb3, cos, sin, gain_h, gain_r, state_h, state_r)


def _even_weight_groups(w):
    w_t = jnp.swapaxes(w, 1, 2)
    gates0 = ZA_WIDTH
    qb0 = gates0 + 2 * MLSTM_HEADS
    ki0 = qb0 + ZB_WIDTH
    end = ki0 + IDX_DIM + IDX_HEADS
    pad = jnp.zeros((w.shape[0], LANES - (end - ki0) - (qb0 - gates0), w.shape[1]), w.dtype)
    w_c = jnp.concatenate([w_t[:, ki0:end], w_t[:, gates0:qb0], pad], axis=1)
    return w_t, w_t[:, qb0:ki0], w_c


def _rot_tables(pos):
    inv = 1.0 / (RET_THETA ** jnp.linspace(0.0, 1.0, RET_DK // 2, dtype=F32))
    ang = pos.astype(F32)[:, None] * inv[None, :]
    return jnp.cos(ang), jnp.sin(ang)


def kernel(x_prompt, x_sample, state_mlstm_C, state_mlstm_n, state_mlstm_m, cache_k, cache_v, cache_idx_k,
           state_hgrn, state_ret, page_table, p_prompt, p_sample, ln_g, ln_b, w_ffn_up, w_ffn_down,
           w_in_even, b_gate_mlstm, g_mlstm, w_in_odd, hgrn_lb, g_hgrn, g_ret, w_out, w_pe_gate, w_pe_proj):
    batch, seq, _ = x_prompt.shape
    db, dec_seq, _ = x_sample.shape
    assert dec_seq == 1 and all(seq % c == 0 for c in (MLSTM_CHUNK, RET_CHUNK, HGRN_CHUNK))
    n_even = w_in_even.shape[0]
    n_odd = w_in_odd.shape[0]
    n_pages = page_table.shape[1]
    past = n_pages * PAGE_SIZE
    mp = batch * seq

    w_up = w_ffn_up.astype(BF16)
    w_down = w_ffn_down.astype(BF16)
    w_even_a, w_even_b, w_even_c = _even_weight_groups(w_in_even)
    w_o = w_out.astype(BF16)
    w_pg = w_pe_gate.astype(BF16)
    w_pp = w_pe_proj.astype(BF16)
    ln_g3 = ln_g.reshape(DEPTH * 3, 1, D_MODEL)
    ln_b3 = ln_b.reshape(DEPTH * 3, 1, D_MODEL)
    pp = p_prompt.reshape(DEPTH, mp, P_DIM)
    ps = p_sample.reshape(DEPTH, db, P_DIM)
    cos_p, sin_p = _rot_tables(jnp.arange(seq))
    cos_s, sin_s = _rot_tables(past + jnp.arange(dec_seq))
    lb3 = hgrn_lb.reshape(DEPTH, 1, HGRN_HEADS * HGRN_DK)
    lb_heads = hgrn_lb.reshape(DEPTH, HGRN_HEADS, HGRN_DK)
    bias_row = jnp.zeros((n_even, 1, LANES), F32).at[:, 0, MISC_IG:MISC_IG + 2 * MLSTM_HEADS].set(b_gate_mlstm)
    bias_col = b_gate_mlstm.reshape(n_even, 2 * MLSTM_HEADS, 1)
    gain_m3 = g_mlstm.reshape(n_even, 1, HALF)
    gain_h3 = g_hgrn.reshape(n_odd, 1, HALF)
    gain_r3 = g_ret.reshape(n_odd, 1, HALF)
    cache_k3 = cache_k.reshape(cache_k.shape[:2] + (PAGE_SIZE * ATT_KV_HEADS, ATT_DH))
    cache_v3 = cache_v.reshape(cache_v.shape[:2] + (PAGE_SIZE * ATT_KV_HEADS, ATT_DH))
    cache_ik_t = jnp.swapaxes(cache_idx_k, 2, 3)
    m_state = state_mlstm_m.reshape(n_even, db, MLSTM_HEADS, 1)

    xp = x_prompt.reshape(mp, D_MODEL)
    xs = x_sample.reshape(db, D_MODEL)
    out_even = {k: [] for k in ("C_p", "C_s", "n_p", "n_s", "m_p", "m_s", "k_p", "k_s", "v_p", "v_s", "ik_p", "ik_s")}
    out_odd = {k: [] for k in ("h_p", "h_s", "r_p", "r_s")}

    for layer in range(DEPTH):
        j = layer // 2
        xp, xp_b = _ffn(xp, w_up, w_down, ln_g3, ln_b3, layer, 0, 0)
        xs, xs_b = _ffn(xs, w_up, w_down, ln_g3, ln_b3, layer, 0, 0)
        if layer % 2 == 0:
            za_p, za_s = (_proj_t(xb, w_even_a, j, ZA_WIDTH) for xb in (xp_b, xs_b))
            zb_p, zb_s = (_proj_t(xb, w_even_b, j, ZB_WIDTH) for xb in (xp_b, xs_b))
            zc_p, zc_s = (_proj_t(xb, w_even_c, j, LANES) for xb in (xp_b, xs_b))
            ya_p, c_p, n_p, m_p = _mlstm_prompt(za_p, zc_p, bias_row[j], gain_m3, batch, seq, j)
            yb_p = _dsa_prompt(zb_p, zc_p, batch, seq)

            x4 = za_s.reshape(db, 4 * MLSTM_HEADS, MLSTM_DH)
            gates_s = zc_s[:, MISC_IG:MISC_IG + 2 * MLSTM_HEADS].reshape(db, 2 * MLSTM_HEADS, 1)
            ya_s, c_s, n_s, m_s = _mlstm_sample(x4, gates_s, bias_col[j], g_mlstm.reshape(n_even, MLSTM_HEADS, MLSTM_DH),
                                                state_mlstm_C, state_mlstm_n, m_state, j)
            q_s = zb_s[:, ZB_Q:ZB_K].reshape(db, ATT_HEADS, ATT_DH)
            k_s = zb_s[:, ZB_K:ZB_V].reshape(db, ATT_KV_HEADS, ATT_DH)
            v_s = zb_s[:, ZB_V:ZB_QI].reshape(db, ATT_KV_HEADS, ATT_DH)
            qi_s = zb_s[:, ZB_QI:].reshape(db, IDX_HEADS, IDX_DIM)
            ki_s = zc_s[:, MISC_KI:MISC_KI + IDX_DIM].reshape(db, 1, IDX_DIM)
            wi_s = zc_s[:, MISC_WI:MISC_WI + IDX_HEADS].reshape(db, IDX_HEADS, 1)
            scores = _dsa_sample_scores(page_table, cache_ik_t, qi_s, wi_s, j)
            select = _dsa_sample_select(scores, qi_s, wi_s, ki_s)
            yb_s = _dsa_sample_attn(page_table, scores, select, q_s, k_s, v_s, cache_k3, cache_v3, j)
            ya_s = ya_s.reshape(db, HALF)
            yb_s = yb_s.reshape(db, HALF)

            out_even["C_p"].append(c_p)
            out_even["C_s"].append(c_s)
            out_even["n_p"].append(n_p.reshape(batch, MLSTM_HEADS, MLSTM_DH))
            out_even["n_s"].append(n_s)
            out_even["m_p"].append(m_p[:, :, 0, 0])
            out_even["m_s"].append(m_s[:, :, 0])
            out_even["k_p"].append(zb_p[:, ZB_K:ZB_V].reshape(batch, seq, ATT_KV_HEADS, ATT_DH))
            out_even["k_s"].append(k_s.reshape(db, dec_seq, ATT_KV_HEADS, ATT_DH))
            out_even["v_p"].append(zb_p[:, ZB_V:ZB_QI].reshape(batch, seq, ATT_KV_HEADS, ATT_DH))
            out_even["v_s"].append(v_s.reshape(db, dec_seq, ATT_KV_HEADS, ATT_DH))
            out_even["ik_p"].append(zc_p[:, MISC_KI:MISC_KI + IDX_DIM].reshape(batch, seq, IDX_DIM))
            out_even["ik_s"].append(ki_s.reshape(db, dec_seq, IDX_DIM))
        else:
            zp = _proj(xp_b, w_in_odd, j)
            zs = _proj(xs_b, w_in_odd, j)
            ya_p, h_p = _hgrn_prompt(zp, lb3, gain_h3, batch, seq, layer, j)
            yb_p, r_p = _ret_prompt(zp, cos_p, sin_p, gain_r3, batch, seq, j)
            xh = zs[:, :4 * HALF].reshape(db, 4 * HGRN_HEADS, HGRN_DK)
            xr = zs[:, 4 * HALF:].reshape(db, 4 * RET_HEADS, RET_DK)
            ya_s, yb_s, h_s, r_s = _odd_sample(
                xh, xr, lb_heads, cos_s, sin_s, g_hgrn.reshape(n_odd, HGRN_HEADS, HGRN_DV),
                g_ret.reshape(n_odd, RET_HEADS, RET_DK), state_hgrn, state_ret, layer, j)
            ya_s = ya_s.reshape(db, HALF)
            yb_s = yb_s.reshape(db, HALF)
            out_odd["h_p"].append(h_p)
            out_odd["h_s"].append(h_s)
            out_odd["r_p"].append(r_p)
            out_odd["r_s"].append(r_s)

        xp = _outproj(xp, ya_p, yb_p, w_o, ln_g3, ln_b3, layer)
        xs = _outproj(xs, ya_s, yb_s, w_o, ln_g3, ln_b3, layer)
        xp, xp_b = _ffn(xp, w_up, w_down, ln_g3, ln_b3, layer, 1, 2)
        xs, xs_b = _ffn(xs, w_up, w_down, ln_g3, ln_b3, layer, 1, 2)
        xp = _pe(xp, xp_b, pp, w_pg, w_pp, layer)
        xs = _pe(xs, xs_b, ps, w_pg, w_pp, layer)

    def stk(name, table, like):
        return jnp.stack(table[name]).astype(like.dtype)

    return (
        xp.reshape(batch, seq, D_MODEL), xs.reshape(db, dec_seq, D_MODEL),
        stk("C_p", out_even, state_mlstm_C), stk("C_s", out_even, state_mlstm_C),
        stk("n_p", out_even, state_mlstm_n), stk("n_s", out_even, state_mlstm_n),
        stk("m_p", out_even, state_mlstm_m), stk("m_s", out_even, state_mlstm_m),
        stk("k_p", out_even, cache_k), stk("k_s", out_even, cache_k),
        stk("v_p", out_even, cache_v), stk("v_s", out_even, cache_v),
        stk("ik_p", out_even, cache_idx_k), stk("ik_s", out_even, cache_idx_k),
        stk("h_p", out_odd, state_hgrn), stk("h_s", out_odd, state_hgrn),
        stk("r_p", out_odd, state_ret), stk("r_s", out_odd, state_ret),
    )
```

```python
import functools
import math

import jax
import jax.numpy as jnp
from jax import lax
from jax.experimental import pallas as pl
from jax.experimental.pallas import tpu as pltpu

F32 = jnp.float32
BF16 = jnp.bfloat16

D_MODEL = 2048
DEPTH = 2
PAGE_SIZE = 128
HALF = D_MODEL // 2
MLSTM_HEADS = 4
MLSTM_DH = HALF // MLSTM_HEADS
ATT_HEADS = 8
ATT_DH = HALF // ATT_HEADS
ATT_KV_HEADS = 2
ATT_GROUP = ATT_HEADS // ATT_KV_HEADS
IDX_HEADS = 8
IDX_DIM = 64
TOPK_MAX = 256
HGRN_HEADS = 8
HGRN_DK = 128
HGRN_DV = HALF // HGRN_HEADS
RET_HEADS = 4
RET_DK = HALF // RET_HEADS
RET_THETA = 10000.0
D_FF = 5632
P_DIM = 256
ALPHA = (2 * DEPTH) ** 0.25
LN_EPS = 1e-5

LANES = 128
SUBLANES = 8

ZA_Q, ZA_K, ZA_V, ZA_O = 0, HALF, 2 * HALF, 3 * HALF
ZA_WIDTH = 4 * HALF
ZB_Q = 0
ZB_K = ZB_Q + ATT_HEADS * ATT_DH
ZB_V = ZB_K + ATT_KV_HEADS * ATT_DH
ZB_QI = ZB_V + ATT_KV_HEADS * ATT_DH
ZB_WIDTH = ZB_QI + IDX_HEADS * IDX_DIM
MISC_KI, MISC_WI = 0, IDX_DIM
MISC_IG = MISC_WI + IDX_HEADS
MISC_FG = MISC_IG + MLSTM_HEADS

MLSTM_CHUNK = 128
RET_CHUNK = 128
HGRN_CHUNK = 64
HGRN_GROUP = 8
DSA_Q_TILE = 256
DSA_WIDTH_STEP = 512
PAGE_GROUP = 16
VMEM_LIMIT = 56 * 1024 * 1024
INT_MIN = -2 ** 31
INT_MAX = 2 ** 31 - 1
F32_INF_BITS = 0x7F800000
NEG_BIG = -1e30

NT_DIMS = (((1,), (1,)), ((), ()))
TN_DIMS = (((0,), (0,)), ((), ()))


def _cparams(*sem):
    return pltpu.CompilerParams(dimension_semantics=sem, vmem_limit_bytes=VMEM_LIMIT)


def _row_tile(m, largest=512):
    for t in (1024, 512, 256, 128, 64, 32, 16, 8):
        if t <= largest and m % t == 0:
            return t
    raise ValueError(f"row count {m} is not a multiple of 8")


def _col_tile(n):
    for t in (1024, 896, 512, 256, 128):
        if n % t == 0:
            return t
    raise ValueError(f"column count {n} is not a multiple of 128")


def _layer_norm(y, g, b):
    mu = jnp.mean(y, -1, keepdims=True)
    d = y - mu
    var = jnp.mean(d * d, -1, keepdims=True)
    return d * lax.rsqrt(var + LN_EPS) * g + b


def _rms_norm(h):
    return h * lax.rsqrt(jnp.mean(h * h, -1, keepdims=True) + LN_EPS)


def _sigmoid(x):
    return jax.nn.sigmoid(x)


def _silu(x):
    return x * jax.nn.sigmoid(x)


def _log_sigmoid(x):
    return jnp.minimum(x, 0.0) - jnp.log1p(jnp.exp(-jnp.abs(x)))


def _iota(shape, dim):
    return lax.broadcasted_iota(jnp.int32, shape, dim)


def _row_to_col(row):
    n = row.shape[1]
    eye = _iota((n, n), 0) == _iota((n, n), 1)
    return jnp.sum(jnp.where(eye, row, 0.0), axis=1, keepdims=True)


def _bdot(a, b):
    return jnp.dot(a.astype(BF16), b.astype(BF16), preferred_element_type=F32)


def _bdot_nt(a, b):
    return lax.dot_general(a.astype(BF16), b.astype(BF16), NT_DIMS, preferred_element_type=F32)


def _bdot_tn(a, b):
    return lax.dot_general(a.astype(BF16), b.astype(BF16), TN_DIMS, preferred_element_type=F32)


def _ffn_kernel(x_ref, wg_ref, wu_ref, wd_ref, g_ref, b_ref, o_ref, ob_ref, xb_ref, acc_ref):
    j = pl.program_id(1)

    @pl.when(j == 0)
    def _():
        xb_ref[...] = x_ref[...].astype(BF16)
        acc_ref[...] = jnp.zeros_like(acc_ref)

    xb = xb_ref[...]
    hg = jnp.dot(xb, wg_ref[...], preferred_element_type=F32)
    hu = jnp.dot(xb, wu_ref[...], preferred_element_type=F32)
    act = _silu(hg) * hu
    acc_ref[...] += jnp.dot(act.astype(BF16), wd_ref[...], preferred_element_type=F32)

    @pl.when(j == pl.num_programs(1) - 1)
    def _():
        y = _layer_norm(ALPHA * x_ref[...] + 0.5 * acc_ref[...], g_ref[...], b_ref[...])
        o_ref[...] = y
        ob_ref[...] = y.astype(BF16)


def _ffn(x, w_up, w_down, ln_g, ln_b, layer, which, ln_idx):
    m = x.shape[0]
    tm, tf = _row_tile(m), 512
    nf = D_FF // tf
    ln_row = layer * 3 + ln_idx
    return pl.pallas_call(
        _ffn_kernel,
        grid=(m // tm, nf),
        in_specs=[
            pl.BlockSpec((tm, D_MODEL), lambda i, j: (i, 0)),
            pl.BlockSpec((None, None, D_MODEL, tf), lambda i, j: (layer, which, 0, j)),
            pl.BlockSpec((None, None, D_MODEL, tf), lambda i, j: (layer, which, 0, j + nf)),
            pl.BlockSpec((None, None, tf, D_MODEL), lambda i, j: (layer, which, j, 0)),
            pl.BlockSpec((None, 1, D_MODEL), lambda i, j: (ln_row, 0, 0)),
            pl.BlockSpec((None, 1, D_MODEL), lambda i, j: (ln_row, 0, 0)),
        ],
        out_specs=[pl.BlockSpec((tm, D_MODEL), lambda i, j: (i, 0))] * 2,
        out_shape=[jax.ShapeDtypeStruct((m, D_MODEL), F32), jax.ShapeDtypeStruct((m, D_MODEL), BF16)],
        scratch_shapes=[pltpu.VMEM((tm, D_MODEL), BF16), pltpu.VMEM((tm, D_MODEL), F32)],
        compiler_params=_cparams("parallel", "arbitrary"),
        name="ffn",
    )(x, w_up, w_up, w_down, ln_g, ln_b)


def _proj_kernel(x_ref, w_ref, o_ref, wb_ref):
    @pl.when(pl.program_id(1) == 0)
    def _():
        wb_ref[...] = w_ref[...].astype(BF16)

    o_ref[...] = jnp.dot(x_ref[...], wb_ref[...], preferred_element_type=F32)


def _proj(xb, w, idx):
    m, n = xb.shape[0], w.shape[2]
    tm, tn = _row_tile(m, 1024), _col_tile(n)
    return pl.pallas_call(
        _proj_kernel,
        grid=(n // tn, m // tm),
        in_specs=[
            pl.BlockSpec((tm, D_MODEL), lambda j, i: (i, 0)),
            pl.BlockSpec((None, D_MODEL, tn), lambda j, i: (idx, 0, j)),
        ],
        out_specs=pl.BlockSpec((tm, tn), lambda j, i: (i, j)),
        out_shape=jax.ShapeDtypeStruct((m, n), F32),
        scratch_shapes=[pltpu.VMEM((D_MODEL, tn), BF16)],
        compiler_params=_cparams("parallel", "arbitrary"),
        name="in_proj",
    )(xb, w)


def _proj_t_kernel(x_ref, w_ref, o_ref, wb_ref):
    @pl.when(pl.program_id(1) == 0)
    def _():
        wb_ref[...] = w_ref[...].astype(BF16)

    o_ref[...] = lax.dot_general(x_ref[...], wb_ref[...], NT_DIMS, preferred_element_type=F32)


def _proj_t(xb, w_t, idx, n_cols):
    m = xb.shape[0]
    tm, tn = _row_tile(m, 1024), _col_tile(n_cols)
    return pl.pallas_call(
        _proj_t_kernel,
        grid=(n_cols // tn, m // tm),
        in_specs=[
            pl.BlockSpec((tm, D_MODEL), lambda j, i: (i, 0)),
            pl.BlockSpec((None, tn, D_MODEL), lambda j, i: (idx, j, 0)),
        ],
        out_specs=pl.BlockSpec((tm, tn), lambda j, i: (i, j)),
        out_shape=jax.ShapeDtypeStruct((m, n_cols), F32),
        scratch_shapes=[pltpu.VMEM((tn, D_MODEL), BF16)],
        compiler_params=_cparams("parallel", "arbitrary"),
        name="in_proj_t",
    )(xb, w_t)


def _outproj_kernel(x_ref, ya_ref, yb_ref, wa_ref, wb_ref, g_ref, b_ref, o_ref):
    y = jnp.dot(ya_ref[...], wa_ref[...], preferred_element_type=F32)
    y += jnp.dot(yb_ref[...], wb_ref[...], preferred_element_type=F32)
    o_ref[...] = _layer_norm(ALPHA * x_ref[...] + y, g_ref[...], b_ref[...])


def _outproj(x, ya, yb, w_out, ln_g, ln_b, layer):
    m = x.shape[0]
    tm = _row_tile(m)
    ln_row = layer * 3 + 1
    return pl.pallas_call(
        _outproj_kernel,
        grid=(m // tm,),
        in_specs=[
            pl.BlockSpec((tm, D_MODEL), lambda i: (i, 0)),
            pl.BlockSpec((tm, HALF), lambda i: (i, 0)),
            pl.BlockSpec((tm, HALF), lambda i: (i, 0)),
            pl.BlockSpec((None, HALF, D_MODEL), lambda i: (layer, 0, 0)),
            pl.BlockSpec((None, HALF, D_MODEL), lambda i: (layer, 1, 0)),
            pl.BlockSpec((None, 1, D_MODEL), lambda i: (ln_row, 0, 0)),
            pl.BlockSpec((None, 1, D_MODEL), lambda i: (ln_row, 0, 0)),
        ],
        out_specs=pl.BlockSpec((tm, D_MODEL), lambda i: (i, 0)),
        out_shape=jax.ShapeDtypeStruct((m, D_MODEL), F32),
        compiler_params=_cparams("parallel"),
        name="out_proj",
    )(x, ya, yb, w_out, w_out, ln_g, ln_b)


def _pe_kernel(x_ref, xb_ref, p_ref, wg_ref, wp_ref, o_ref):
    gate = _sigmoid(jnp.dot(xb_ref[...], wg_ref[...], preferred_element_type=F32))
    o_ref[...] = x_ref[...] + gate * jnp.dot(p_ref[...].astype(BF16), wp_ref[...], preferred_element_type=F32)


def _pe(x, xb, p, w_gate, w_proj, layer):
    m = x.shape[0]
    tm = _row_tile(m)
    return pl.pallas_call(
        _pe_kernel,
        grid=(m // tm,),
        in_specs=[
            pl.BlockSpec((tm, D_MODEL), lambda i: (i, 0)),
            pl.BlockSpec((tm, D_MODEL), lambda i: (i, 0)),
            pl.BlockSpec((None, tm, P_DIM), lambda i: (layer, i, 0)),
            pl.BlockSpec((None, D_MODEL, D_MODEL), lambda i: (layer, 0, 0)),
            pl.BlockSpec((None, P_DIM, D_MODEL), lambda i: (layer, 0, 0)),
        ],
        out_specs=pl.BlockSpec((tm, D_MODEL), lambda i: (i, 0)),
        out_shape=jax.ShapeDtypeStruct((m, D_MODEL), F32),
        compiler_params=_cparams("parallel"),
        name="pe_gate",
    )(x, xb, p, w_gate, w_proj)


def _mlstm_prompt_kernel(q_ref, k_ref, v_ref, o_ref, misc_ref, bias_ref, gain_ref,
                         y_ref, c_ref, n_ref, m_ref, m_sc):
    chunk = q_ref.shape[0]
    dh = MLSTM_DH

    @pl.when(pl.program_id(1) == 0)
    def _():
        c_ref[...] = jnp.zeros_like(c_ref)
        n_ref[...] = jnp.zeros_like(n_ref)
        m_sc[...] = jnp.zeros_like(m_sc)

    gates = misc_ref[...] + bias_ref[...]
    ti = _iota((chunk, chunk), 0)
    si = _iota((chunk, chunk), 1)
    eye = ti == si
    causal = si <= ti
    for h in range(MLSTM_HEADS):
        cols = slice(h * dh, (h + 1) * dh)
        ig_col = gates[:, MISC_IG + h:MISC_IG + h + 1]
        lf_col = _log_sigmoid(gates[:, MISC_FG + h:MISC_FG + h + 1])
        lf_row = jnp.sum(jnp.where(eye, lf_col, 0.0), axis=0, keepdims=True)
        ig_row = jnp.sum(jnp.where(eye, ig_col, 0.0), axis=0, keepdims=True)
        b_col = jnp.sum(jnp.where(causal, lf_row, 0.0), axis=1, keepdims=True)
        b_row = jnp.sum(jnp.where(ti <= si, lf_col, 0.0), axis=0, keepdims=True)
        m_prev = m_sc[h][:, :1]
        dlog = jnp.where(causal, b_col - b_row + ig_row, -jnp.inf)
        g_col = b_col + m_prev
        m_t = jnp.maximum(g_col, jnp.max(dlog, axis=1, keepdims=True))
        dw = jnp.exp(dlog - m_t)
        gw = jnp.exp(g_col - m_t)

        q = q_ref[:, cols] * (dh ** -0.5)
        k = k_ref[:, cols]
        v = v_ref[:, cols]
        c_state = c_ref[h]
        n_state = n_ref[h]
        s = _bdot_nt(q, k) * dw
        num = _bdot(s, v) + gw * _bdot(q, c_state)
        den = jnp.sum(s, axis=1, keepdims=True) + gw * jnp.sum(q * n_state, axis=1, keepdims=True)
        hid = num / jnp.maximum(jnp.abs(den), jnp.exp(-m_t))

        b_last = b_col[chunk - 1:chunk, :]
        wlog = b_last - b_col + ig_col
        m_new = jnp.maximum(b_last + m_prev, jnp.max(wlog, axis=0, keepdims=True))
        w_col = jnp.exp(wlog - m_new)
        decay = jnp.exp(b_last + m_prev - m_new)
        kw = k * w_col
        c_ref[h] = decay * c_state + _bdot_tn(kw, v)
        n_ref[h] = decay * n_state + jnp.sum(kw, axis=0, keepdims=True)
        m_row = jnp.broadcast_to(m_new, (1, LANES))
        m_sc[h] = m_row
        m_ref[h] = m_row

        y_ref[:, cols] = (_rms_norm(hid) * gain_ref[:, cols] * _sigmoid(o_ref[:, cols])).astype(BF16)


def _mlstm_prompt(za, zc, bias_row, gain, batch, seq, layer_j):
    chunk = MLSTM_CHUNK
    nc = seq // chunk
    nh, dh = MLSTM_HEADS, MLSTM_DH

    def col(base):
        return lambda b, c: (b * nc + c, base // HALF)

    state = lambda b, c: (b, 0, 0, 0)
    return pl.pallas_call(
        _mlstm_prompt_kernel,
        grid=(batch, nc),
        in_specs=[
            pl.BlockSpec((chunk, HALF), col(ZA_Q)),
            pl.BlockSpec((chunk, HALF), col(ZA_K)),
            pl.BlockSpec((chunk, HALF), col(ZA_V)),
            pl.BlockSpec((chunk, HALF), col(ZA_O)),
            pl.BlockSpec((chunk, LANES), lambda b, c: (b * nc + c, 0)),
            pl.BlockSpec((1, LANES), lambda b, c: (0, 0)),
            pl.BlockSpec((None, 1, HALF), lambda b, c: (layer_j, 0, 0)),
        ],
        out_specs=[
            pl.BlockSpec((chunk, HALF), lambda b, c: (b * nc + c, 0)),
            pl.BlockSpec((None, nh, dh, dh), state),
            pl.BlockSpec((None, nh, 1, dh), state),
            pl.BlockSpec((None, nh, 1, LANES), state),
        ],
        out_shape=[
            jax.ShapeDtypeStruct((batch * seq, HALF), BF16),
            jax.ShapeDtypeStruct((batch, nh, dh, dh), F32),
            jax.ShapeDtypeStruct((batch, nh, 1, dh), F32),
            jax.ShapeDtypeStruct((batch, nh, 1, LANES), F32),
        ],
        scratch_shapes=[pltpu.VMEM((nh, 1, LANES), F32)],
        compiler_params=_cparams("parallel", "arbitrary"),
        name="mlstm_prompt",
    )(za, za, za, za, zc, bias_row, gain)


def _hgrn_lower_bound(lb_ref, layer):
    rows = [lb_ref[j] for j in range(DEPTH)]
    mx = functools.reduce(jnp.maximum, rows)
    e = [jnp.exp(r - mx) for r in rows]
    total = functools.reduce(jnp.add, e)
    acc = jnp.zeros_like(total)
    for j in range(1, layer + 1):
        acc = acc + e[j] / total
    return acc


def _block_mid_rows(b, half):
    rows, width = b.shape
    block = 2 * half
    if block >= SUBLANES:
        parts = [jnp.broadcast_to(b[r + half - 1:r + half, :], (block, width)) for r in range(0, rows, block)]
        return jnp.concatenate(parts, axis=0)
    b3 = b.reshape(rows // SUBLANES, SUBLANES, width)
    sub = _iota(b3.shape, 1)
    out = jnp.broadcast_to(b3[:, half - 1:half, :], b3.shape)
    for r in range(block, SUBLANES, block):
        out = jnp.where(sub >= r, jnp.broadcast_to(b3[:, r + half - 1:r + half, :], b3.shape), out)
    return out.reshape(rows, width)


def _hgrn_prompt_kernel(q_ref, f_ref, i_ref, g_ref, lb_ref, gain_ref, y_ref, s_ref, *, layer):
    chunk = q_ref.shape[0]

    @pl.when(pl.program_id(2) == 0)
    def _():
        s_ref[...] = jnp.zeros_like(s_ref)

    lb = _hgrn_lower_bound(lb_ref, layer)
    f_all = lb + (1.0 - lb) * _sigmoid(f_ref[...])
    lf = jnp.log(f_all)
    halves = [chunk >> (lvl + 1) for lvl in range(chunk.bit_length() - 1)]
    ti = _iota((chunk, chunk), 0)
    si = _iota((chunk, chunk), 1)
    row = _iota((chunk, 1), 0)
    level_mask = []
    second_half = []
    for half in halves:
        mid = (ti // (2 * half)) * (2 * half) + half - 1
        level_mask.append((si <= mid) & (ti > mid) & (si // (2 * half) == ti // (2 * half)))
        second_half.append(row % (2 * half) >= half)
    eye = ti == si
    b_all = jnp.dot(jnp.where(si <= ti, 1.0, 0.0), lf, preferred_element_type=F32,
                    precision=lax.Precision.HIGHEST)

    for h in range(HGRN_GROUP):
        cols = slice(h * HGRN_DK, (h + 1) * HGRN_DK)
        b = b_all[:, cols]
        kk = 1.0 - f_all[:, cols]
        q = _silu(q_ref[:, cols])
        v = i_ref[:, cols]
        b_last = b[chunk - 1:chunk, :]
        state = s_ref[h]

        attn = jnp.where(eye, jnp.sum(q * kk, axis=1, keepdims=True), 0.0)
        for lvl, half in enumerate(halves):
            x = jnp.where(second_half[lvl], q, kk) * jnp.exp(-jnp.abs(b - _block_mid_rows(b, half)))
            attn = attn + jnp.where(level_mask[lvl], _bdot_nt(x, x), 0.0)
        o = _bdot(attn, v) + _bdot(q * jnp.exp(b), state)

        s_ref[h] = _row_to_col(jnp.exp(b_last)) * state + _bdot_tn(kk * jnp.exp(b_last - b), v)
        y_ref[:, cols] = (_rms_norm(o) * gain_ref[:, cols] * _silu(g_ref[:, cols])).astype(BF16)


def _hgrn_prompt(z, lb3, gain, batch, seq, layer, layer_j):
    chunk = HGRN_CHUNK
    nc = seq // chunk
    n_groups = HGRN_HEADS // HGRN_GROUP
    dk = HGRN_GROUP * HGRN_DK

    def col(group):
        return lambda b, h, c: (b * nc + c, group * n_groups + h)

    return pl.pallas_call(
        functools.partial(_hgrn_prompt_kernel, layer=layer),
        grid=(batch, n_groups, nc),
        in_specs=[
            pl.BlockSpec((chunk, dk), col(0)),
            pl.BlockSpec((chunk, dk), col(1)),
            pl.BlockSpec((chunk, dk), col(2)),
            pl.BlockSpec((chunk, dk), col(3)),
            pl.BlockSpec((DEPTH, 1, dk), lambda b, h, c: (0, 0, h)),
            pl.BlockSpec((None, 1, dk), lambda b, h, c: (layer_j, 0, h)),
        ],
        out_specs=[
            pl.BlockSpec((chunk, dk), lambda b, h, c: (b * nc + c, h)),
            pl.BlockSpec((None, HGRN_GROUP, HGRN_DK, HGRN_DV), lambda b, h, c: (b, h, 0, 0)),
        ],
        out_shape=[
            jax.ShapeDtypeStruct((batch * seq, HALF), BF16),
            jax.ShapeDtypeStruct((batch, HGRN_HEADS, HGRN_DK, HGRN_DV), F32),
        ],
        compiler_params=_cparams("parallel", "parallel", "arbitrary"),
        name="hgrn_prompt",
    )(z, z, z, z, lb3, gain)


def _ret_log_gamma(h):
    return math.log1p(-2.0 ** (-5 - h))


def _rotate(x, cos, sin):
    half = x.shape[1] // 2
    x1, x2 = x[:, :half], x[:, half:]
    return jnp.concatenate([x1 * cos - x2 * sin, x1 * sin + x2 * cos], axis=1)


def _ret_prompt_kernel(q_ref, k_ref, v_ref, g_ref, cos_ref, sin_ref, gain_ref, y_ref, s_ref):
    chunk = q_ref.shape[0]

    @pl.when(pl.program_id(1) == 0)
    def _():
        s_ref[...] = jnp.zeros_like(s_ref)

    cos, sin = cos_ref[...], sin_ref[...]
    rel = (_iota((chunk, chunk), 0) - _iota((chunk, chunk), 1)).astype(F32)
    pos = _iota((chunk, 1), 0).astype(F32)
    for h in range(RET_HEADS):
        cols = slice(h * RET_DK, (h + 1) * RET_DK)
        lg = _ret_log_gamma(h)
        qr = _rotate(q_ref[:, cols], cos, sin)
        kr = _rotate(k_ref[:, cols], cos, sin) * (RET_DK ** -0.5)
        v = v_ref[:, cols]
        dmask = jnp.where(rel >= 0.0, jnp.exp(lg * jnp.maximum(rel, 0.0)), 0.0)
        state = s_ref[h]
        o = _bdot(_bdot_nt(qr, kr) * dmask, v) + _bdot(qr, state) * jnp.exp(lg * (pos + 1.0))
        tail = jnp.exp(lg * (chunk - 1.0 - pos))
        s_ref[h] = math.exp(lg * chunk) * state + _bdot_tn(kr * tail, v)
        y_ref[:, cols] = (_rms_norm(o) * gain_ref[:, cols] * _silu(g_ref[:, cols])).astype(BF16)


def _ret_prompt(z, cos, sin, gain, batch, seq, layer_j):
    chunk = RET_CHUNK
    nc = seq // chunk
    dk = RET_DK

    def col(group):
        return lambda b, c: (b * nc + c, 4 + group)

    return pl.pallas_call(
        _ret_prompt_kernel,
        grid=(batch, nc),
        in_specs=[
            pl.BlockSpec((chunk, HALF), col(0)),
            pl.BlockSpec((chunk, HALF), col(1)),
            pl.BlockSpec((chunk, HALF), col(2)),
            pl.BlockSpec((chunk, HALF), col(3)),
            pl.BlockSpec((chunk, dk // 2), lambda b, c: (c, 0)),
            pl.BlockSpec((chunk, dk // 2), lambda b, c: (c, 0)),
            pl.BlockSpec((None, 1, HALF), lambda b, c: (layer_j, 0, 0)),
        ],
        out_specs=[
            pl.BlockSpec((chunk, HALF), lambda b, c: (b * nc + c, 0)),
            pl.BlockSpec((None, RET_HEADS, dk, dk), lambda b, c: (b, 0, 0, 0)),
        ],
        out_shape=[
            jax.ShapeDtypeStruct((batch * seq, HALF), BF16),
            jax.ShapeDtypeStruct((batch, RET_HEADS, dk, dk), F32),
        ],
        compiler_params=_cparams("parallel", "arbitrary"),
        name="ret_prompt",
    )(z, z, z, z, cos, sin, gain)


def _ordered_float(key):
    bits = jnp.where(key < 0, (-key) | INT_MIN, key)
    return lax.bitcast_convert_type(bits, F32)


def _count(mask):
    return jnp.sum(jnp.where(mask, 1.0, 0.0), axis=1, keepdims=True)


def _kth_largest(count_ge, shape, n_sel):
    def body(it, t_off):
        cand = t_off | lax.shift_left(jnp.int32(1), 31 - it)
        return jnp.where(count_ge(_ordered_float(cand ^ INT_MIN)) >= n_sel, cand, t_off)

    t_off = lax.fori_loop(0, 32, body, jnp.zeros(shape, jnp.int32))
    return _ordered_float(jnp.maximum(t_off ^ INT_MIN, -F32_INF_BITS))


def _tie_cutoff(count_eq_below, shape, budget, n_bits):
    def body(it, cut):
        cand = cut | lax.shift_left(jnp.int32(1), n_bits - 1 - it)
        return jnp.where(count_eq_below(cand) <= budget, cand, cut)

    return lax.fori_loop(0, n_bits, body, jnp.zeros(shape, jnp.int32))


def _select_top(score, valid, kpos, n_sel, n_idx_bits):
    per_row = (score.shape[0], 1)
    thr = _kth_largest(lambda t: _count(score >= t), per_row, n_sel)
    above = score > thr
    tied = score == thr
    n_above = _count(above)
    need_cut = jnp.max(n_above + _count(tied & valid)) > n_sel
    cut = lax.cond(
        need_cut,
        lambda: _tie_cutoff(lambda j: _count(tied & (kpos < j)), per_row, n_sel - n_above, n_idx_bits),
        lambda: jnp.full(per_row, INT_MAX, jnp.int32))
    return valid & (above | (tied & (kpos < cut)))


def _dsa_prompt_body(q_ref, qi_ref, qmisc_ref, kb_ref, vb_ref, kib_ref, y_ref, n_sel, width):
    qblk = q_ref.shape[0]
    ki = kib_ref[:width, :]
    qmisc = qmisc_ref[...]
    score = jnp.zeros((qblk, width), F32)
    for h in range(IDX_HEADS):
        w = qmisc[:, MISC_WI + h:MISC_WI + h + 1] * (IDX_HEADS ** -0.5) * (IDX_DIM ** -0.5)
        score = score + jnp.maximum(_bdot_nt(qi_ref[:, h * IDX_DIM:(h + 1) * IDX_DIM], ki), 0.0) * w

    qpos = pl.program_id(1) * qblk + _iota((qblk, width), 0)
    kpos = _iota((qblk, width), 1)
    valid = kpos <= qpos
    sel = _select_top(jnp.where(valid, score, -jnp.inf), valid, kpos, n_sel, (width + 1).bit_length())
    bias = jnp.where(sel, 0.0, -jnp.inf)

    exp2_scale = (ATT_DH ** -0.5) * math.log2(math.e)
    for g in range(ATT_KV_HEADS):
        kg = kb_ref[:width, g * ATT_DH:(g + 1) * ATT_DH]
        vg = vb_ref[:width, g * ATT_DH:(g + 1) * ATT_DH]
        for hh in range(ATT_GROUP):
            cols = slice((g * ATT_GROUP + hh) * ATT_DH, (g * ATT_GROUP + hh + 1) * ATT_DH)
            logits = _bdot_nt(q_ref[:, cols], kg) + bias
            e = jnp.exp2((logits - jnp.max(logits, axis=1, keepdims=True)) * exp2_scale)
            y_ref[:, cols] = (_bdot(e, vg) / jnp.sum(e, axis=1, keepdims=True)).astype(BF16)


def _dsa_prompt_kernel(q_ref, k_ref, v_ref, qi_ref, qmisc_ref, kmisc_ref, y_ref, kb_ref, vb_ref, kib_ref, *, n_sel):
    tile = q_ref.shape[0]
    i = pl.program_id(1)

    @pl.when(i == 0)
    def _():
        kb_ref[...] = k_ref[...].astype(BF16)
        vb_ref[...] = v_ref[...].astype(BF16)
        kib_ref[...] = kmisc_ref[:, MISC_KI:MISC_KI + IDX_DIM].astype(BF16)

    seq = k_ref.shape[0]
    step = DSA_WIDTH_STEP if seq % DSA_WIDTH_STEP == 0 else seq
    tiles_per_step = step // tile
    for var in range(seq // step):
        @pl.when((i >= var * tiles_per_step) & (i < (var + 1) * tiles_per_step))
        def _(width=(var + 1) * step):
            _dsa_prompt_body(q_ref, qi_ref, qmisc_ref, kb_ref, vb_ref, kib_ref, y_ref, n_sel, width)


def _dsa_prompt(zb, zc, batch, seq):
    tile = DSA_Q_TILE if seq % DSA_WIDTH_STEP == 0 else seq
    nt = seq // tile
    n_sel = min(TOPK_MAX, seq // 4)
    kv_w = ATT_KV_HEADS * ATT_DH
    qi_w = IDX_HEADS * IDX_DIM
    return pl.pallas_call(
        functools.partial(_dsa_prompt_kernel, n_sel=n_sel),
        grid=(batch, nt),
        in_specs=[
            pl.BlockSpec((tile, HALF), lambda b, i: (b * nt + i, ZB_Q // HALF)),
            pl.BlockSpec((seq, kv_w), lambda b, i: (b, ZB_K // kv_w)),
            pl.BlockSpec((seq, kv_w), lambda b, i: (b, ZB_V // kv_w)),
            pl.BlockSpec((tile, qi_w), lambda b, i: (b * nt + i, ZB_QI // qi_w)),
            pl.BlockSpec((tile, LANES), lambda b, i: (b * nt + i, 0)),
            pl.BlockSpec((seq, LANES), lambda b, i: (b, 0)),
        ],
        out_specs=pl.BlockSpec((tile, HALF), lambda b, i: (b * nt + i, 0)),
        out_shape=jax.ShapeDtypeStruct((batch * seq, HALF), BF16),
        scratch_shapes=[pltpu.VMEM((seq, kv_w), BF16), pltpu.VMEM((seq, kv_w), BF16),
                        pltpu.VMEM((seq, IDX_DIM), BF16)],
        compiler_params=_cparams("parallel", "arbitrary"),
        name="dsa_prompt",
    )(zb, zb, zb, zb, zc, zc)


def _page_specs(block, n_pages, layer_j):
    def spec(g):
        return pl.BlockSpec((None, None) + block,
                            lambda b, s, pt: (layer_j, pt[b * n_pages + s * PAGE_GROUP + g], 0, 0))

    return [spec(g) for g in range(PAGE_GROUP)]


def _dsa_sample_score_kernel(pt_ref, *refs):
    ik_refs = refs[:PAGE_GROUP]
    qi_ref, wi_ref, o_ref = refs[PAGE_GROUP:]
    qi = qi_ref[...]
    w = wi_ref[...] * (IDX_HEADS ** -0.5)
    rows = []
    for ik_ref in ik_refs:
        r = _bdot(qi, ik_ref[...]) * (IDX_DIM ** -0.5)
        rows.append(jnp.sum(jnp.maximum(r, 0.0) * w, axis=0, keepdims=True))
    o_ref[...] = jnp.concatenate(rows, axis=0)


def _dsa_sample_scores(page_table, cache_ik_t, qi, wi, layer_j):
    dec_batch, n_pages = page_table.shape
    assert n_pages % PAGE_GROUP == 0
    grid_spec = pltpu.PrefetchScalarGridSpec(
        num_scalar_prefetch=1,
        grid=(dec_batch, n_pages // PAGE_GROUP),
        in_specs=_page_specs((IDX_DIM, PAGE_SIZE), n_pages, layer_j) + [
            pl.BlockSpec((None, IDX_HEADS, IDX_DIM), lambda b, s, pt: (b, 0, 0)),
            pl.BlockSpec((None, IDX_HEADS, 1), lambda b, s, pt: (b, 0, 0)),
        ],
        out_specs=pl.BlockSpec((None, PAGE_GROUP, PAGE_SIZE), lambda b, s, pt: (b, s, 0)),
    )
    return pl.pallas_call(
        _dsa_sample_score_kernel,
        grid_spec=grid_spec,
        out_shape=jax.ShapeDtypeStruct((dec_batch, n_pages, PAGE_SIZE), F32),
        compiler_params=_cparams("parallel", "arbitrary"),
        name="dsa_sample_scores",
    )(page_table.reshape(-1), *([cache_ik_t] * PAGE_GROUP), qi, wi)


def _dsa_sample_select_kernel(sc_ref, qi_ref, wi_ref, kin_ref, thr_ref, cut_ref, new_ref, *, n_sel):
    past = sc_ref.shape[1] * PAGE_SIZE
    r_new = jnp.sum(qi_ref[...] * kin_ref[...], axis=2, keepdims=True) * (IDX_DIM ** -0.5)
    key_new = jnp.sum(jnp.maximum(r_new, 0.0) * (wi_ref[...] * (IDX_HEADS ** -0.5)), axis=1, keepdims=True)
    key = sc_ref[...]
    idx = _iota(key.shape, 1) * PAGE_SIZE + _iota(key.shape, 2)
    per_req = key_new.shape

    def total(mask, mask_new):
        in_page = jnp.sum(jnp.where(mask, 1.0, 0.0), axis=2, keepdims=True)
        return jnp.sum(in_page, axis=1, keepdims=True) + jnp.where(mask_new, 1.0, 0.0)

    thr = _kth_largest(lambda t: total(key >= t, key_new >= t), per_req, n_sel)
    above, tied = key > thr, key == thr
    n_above = total(above, key_new > thr)
    need_cut = jnp.max(n_above + total(tied, key_new == thr)) > n_sel
    cut = lax.cond(
        need_cut,
        lambda: _tie_cutoff(lambda j: total(tied & (idx < j), (key_new == thr) & (past < j)),
                            per_req, n_sel - n_above, (past + 2).bit_length()),
        lambda: jnp.full(per_req, INT_MAX, jnp.int32))
    thr_ref[...] = jnp.broadcast_to(thr, thr_ref.shape)
    cut_ref[...] = jnp.broadcast_to(cut, cut_ref.shape)
    new_ref[...] = jnp.broadcast_to(key_new, new_ref.shape)


def _dsa_sample_select(scores, qi, wi, ki_new):
    dec_batch, n_pages, _ = scores.shape
    n_sel = min(TOPK_MAX, (n_pages * PAGE_SIZE + 1) // 4)
    out = lambda dtype: jax.ShapeDtypeStruct((dec_batch, 1, LANES), dtype)
    return pl.pallas_call(
        functools.partial(_dsa_sample_select_kernel, n_sel=n_sel),
        out_shape=[out(F32), out(jnp.int32), out(F32)],
        compiler_params=pltpu.CompilerParams(vmem_limit_bytes=VMEM_LIMIT),
        name="dsa_sample_select",
    )(scores, qi, wi, ki_new)


def _dsa_sample_attn_kernel(pt_ref, sc_ref, thr_ref, cut_ref, new_ref, q_ref, knew_ref, vnew_ref, *refs):
    kc_refs = refs[:PAGE_GROUP]
    vc_refs = refs[PAGE_GROUP:2 * PAGE_GROUP]
    y_ref, m_sc, l_sc, acc_sc = refs[2 * PAGE_GROUP:]
    step = pl.program_id(1)
    past = sc_ref.shape[0] * PAGE_SIZE
    group_w = PAGE_GROUP * PAGE_SIZE * ATT_KV_HEADS

    @pl.when(step == 0)
    def _():
        m_sc[...] = jnp.full_like(m_sc, NEG_BIG)
        l_sc[...] = jnp.zeros_like(l_sc)
        acc_sc[...] = jnp.zeros_like(acc_sc)

    thr = thr_ref[:, :1]
    cut = cut_ref[:, :1]
    scale = ATT_DH ** -0.5
    q = q_ref[...]
    first_group = _iota((ATT_HEADS, 1), 0) < ATT_GROUP

    def by_group(fn):
        return jnp.where(first_group, fn(0), fn(1))

    def online_update(logits, pv):
        m_old = m_sc[...]
        m_new = jnp.maximum(m_old, jnp.max(logits, axis=1, keepdims=True))
        alpha = jnp.exp(m_old - m_new)
        pr = jnp.exp(logits - m_new)
        l_sc[...] = alpha * l_sc[...] + jnp.sum(pr, axis=1, keepdims=True)
        acc_sc[...] = alpha * acc_sc[...] + pv(pr)
        m_sc[...] = m_new

    first_page = pl.multiple_of(step * PAGE_GROUP, PAGE_GROUP)
    key = sc_ref[pl.ds(first_page, PAGE_GROUP), :]
    idx = (first_page + _iota(key.shape, 0)) * PAGE_SIZE + _iota(key.shape, 1)
    sel = (key > thr) | ((key == thr) & (idx < cut))
    exp_shape = (PAGE_SIZE, PAGE_SIZE * ATT_KV_HEADS)
    expand = jnp.where(_iota(exp_shape, 1) // ATT_KV_HEADS == _iota(exp_shape, 0), 1.0, 0.0)
    sel_wide = _bdot(jnp.where(sel, 1.0, 0.0), expand)
    sel_row = jnp.concatenate([sel_wide[g:g + 1, :] for g in range(PAGE_GROUP)], axis=1)
    col_head = _iota((ATT_HEADS, group_w), 1) % ATT_KV_HEADS
    row_head = _iota((ATT_HEADS, group_w), 0) // ATT_GROUP
    mask = (sel_row > 0.5) & (col_head == row_head)
    kc = jnp.concatenate([r[...] for r in kc_refs], axis=0)
    vc = jnp.concatenate([r[...] for r in vc_refs], axis=0)
    logits = jnp.where(mask, _bdot_nt(q, kc) * scale, -jnp.inf)
    online_update(logits, lambda pr: _bdot(pr, vc))

    @pl.when(step == pl.num_programs(1) - 1)
    def _():
        key_new = new_ref[:, :1]
        sel_new = (key_new > thr) | ((key_new == thr) & (past < cut))
        k_new = by_group(lambda g: knew_ref[g:g + 1, :])
        v_new = by_group(lambda g: vnew_ref[g:g + 1, :])
        logit_new = jnp.sum(q * k_new, axis=1, keepdims=True) * scale
        logit_new = jnp.where(sel_new, logit_new, -jnp.inf)
        online_update(logit_new, lambda pr: pr * v_new)
        y_ref[...] = (acc_sc[...] / l_sc[...]).astype(BF16)


def _dsa_sample_attn(page_table, scores, select, q, k_new, v_new, cache_k, cache_v, layer_j):
    dec_batch, n_pages = page_table.shape
    per_b = lambda b, s, pt: (b, 0, 0)
    page_specs = _page_specs((PAGE_SIZE * ATT_KV_HEADS, ATT_DH), n_pages, layer_j)
    grid_spec = pltpu.PrefetchScalarGridSpec(
        num_scalar_prefetch=1,
        grid=(dec_batch, n_pages // PAGE_GROUP),
        in_specs=[
            pl.BlockSpec((None, n_pages, PAGE_SIZE), per_b),
            pl.BlockSpec((None, 1, LANES), per_b),
            pl.BlockSpec((None, 1, LANES), per_b),
            pl.BlockSpec((None, 1, LANES), per_b),
            pl.BlockSpec((None, ATT_HEADS, ATT_DH), per_b),
            pl.BlockSpec((None, ATT_KV_HEADS, ATT_DH), per_b),
            pl.BlockSpec((None, ATT_KV_HEADS, ATT_DH), per_b),
        ] + page_specs + page_specs,
        out_specs=pl.BlockSpec((None, ATT_HEADS, ATT_DH), per_b),
        scratch_shapes=[
            pltpu.VMEM((ATT_HEADS, 1), F32),
            pltpu.VMEM((ATT_HEADS, 1), F32),
            pltpu.VMEM((ATT_HEADS, ATT_DH), F32),
        ],
    )
    return pl.pallas_call(
        _dsa_sample_attn_kernel,
        grid_spec=grid_spec,
        out_shape=jax.ShapeDtypeStruct((dec_batch, ATT_HEADS, ATT_DH), BF16),
        compiler_params=_cparams("parallel", "arbitrary"),
        name="dsa_sample_attn",
    )(page_table.reshape(-1), scores, *select, q, k_new, v_new,
      *([cache_k] * PAGE_GROUP), *([cache_v] * PAGE_GROUP))


def _vec_mat(row, mat):
    return jnp.sum(_row_to_col(row) * mat, axis=0, keepdims=True)


def _mlstm_sample_kernel(x_ref, gate_ref, bias_ref, gain_ref, c_ref, n_ref, m_ref,
                         y_ref, c_out, n_out, m_out):
    gates = gate_ref[...] + bias_ref[...]
    for h in range(MLSTM_HEADS):
        q = x_ref[h:h + 1, :] * (MLSTM_DH ** -0.5)
        k = x_ref[MLSTM_HEADS + h:MLSTM_HEADS + h + 1, :]
        v = x_ref[2 * MLSTM_HEADS + h:2 * MLSTM_HEADS + h + 1, :]
        og = x_ref[3 * MLSTM_HEADS + h:3 * MLSTM_HEADS + h + 1, :]
        ig = gates[h:h + 1, :]
        lf = _log_sigmoid(gates[MLSTM_HEADS + h:MLSTM_HEADS + h + 1, :])
        m_prev = m_ref[h:h + 1, :]
        c_state = c_ref[h]
        n_state = n_ref[h:h + 1, :]
        g = lf + m_prev
        m_t = jnp.maximum(g, ig)
        dw = jnp.exp(ig - m_t)
        gw = jnp.exp(g - m_t)
        s = jnp.sum(q * k, axis=1, keepdims=True) * dw
        num = s * v + gw * _vec_mat(q, c_state)
        den = s + gw * jnp.sum(q * n_state, axis=1, keepdims=True)
        hid = num / jnp.maximum(jnp.abs(den), jnp.exp(-m_t))
        w = jnp.exp(ig - m_t)
        decay = jnp.exp(g - m_t)
        c_out[h] = decay * c_state + _row_to_col(k * w) * v
        n_out[h:h + 1, :] = decay * n_state + w * k
        m_out[h:h + 1, :] = m_t
        y_ref[h:h + 1, :] = (_rms_norm(hid) * gain_ref[h:h + 1, :] * _sigmoid(og)).astype(BF16)


def _mlstm_sample(x, gates, bias, gain, c_state, n_state, m_state, layer_j):
    db = x.shape[0]
    nh, dh = MLSTM_HEADS, MLSTM_DH
    per_b3 = lambda b: (b, 0, 0)
    return pl.pallas_call(
        _mlstm_sample_kernel,
        grid=(db,),
        in_specs=[
            pl.BlockSpec((None, 4 * nh, dh), per_b3),
            pl.BlockSpec((None, 2 * nh, 1), per_b3),
            pl.BlockSpec((2 * nh, 1), lambda b: (0, 0)),
            pl.BlockSpec((None, nh, dh), lambda b: (layer_j, 0, 0)),
            pl.BlockSpec((None, None, nh, dh, dh), lambda b: (layer_j, b, 0, 0, 0)),
            pl.BlockSpec((None, None, nh, dh), lambda b: (layer_j, b, 0, 0)),
            pl.BlockSpec((None, None, nh, 1), lambda b: (layer_j, b, 0, 0)),
        ],
        out_specs=[
            pl.BlockSpec((None, nh, dh), per_b3),
            pl.BlockSpec((None, nh, dh, dh), lambda b: (b, 0, 0, 0)),
            pl.BlockSpec((None, nh, dh), per_b3),
            pl.BlockSpec((None, nh, 1), per_b3),
        ],
        out_shape=[
            jax.ShapeDtypeStruct((db, nh, dh), BF16),
            jax.ShapeDtypeStruct((db, nh, dh, dh), F32),
            jax.ShapeDtypeStruct((db, nh, dh), F32),
            jax.ShapeDtypeStruct((db, nh, 1), F32),
        ],
        compiler_params=_cparams("parallel"),
        name="mlstm_sample",
    )(x, gates, bias, gain, c_state, n_state, m_state)


def _odd_sample_kernel(xh_ref, xr_ref, lb_ref, cos_ref, sin_ref, gh_ref, gr_ref, sh_ref, sr_ref,
                       yh_ref, yr_ref, sh_out, sr_out, *, layer):
    nh = HGRN_HEADS
    lb = _hgrn_lower_bound(lb_ref, layer)
    f = lb + (1.0 - lb) * _sigmoid(xh_ref[nh:2 * nh, :])
    q = _silu(xh_ref[0:nh, :])
    kk = 1.0 - f
    v = xh_ref[2 * nh:3 * nh, :]
    gate = xh_ref[3 * nh:4 * nh, :]
    qk = jnp.sum(q * kk, axis=1, keepdims=True)
    for h in range(nh):
        state = sh_ref[h]
        f_col = _row_to_col(f[h:h + 1, :])
        k_col = _row_to_col(kk[h:h + 1, :])
        v_row = v[h:h + 1, :]
        o = qk[h:h + 1, :] * v_row + _vec_mat(q[h:h + 1, :] * f[h:h + 1, :], state)
        sh_out[h] = f_col * state + k_col * v_row
        yh_ref[h:h + 1, :] = (_rms_norm(o) * gh_ref[h:h + 1, :] * _silu(gate[h:h + 1, :])).astype(BF16)

    nr = RET_HEADS
    cos, sin = cos_ref[...], sin_ref[...]
    qr = _rotate(xr_ref[0:nr, :], cos, sin)
    kr = _rotate(xr_ref[nr:2 * nr, :], cos, sin) * (RET_DK ** -0.5)
    vr = xr_ref[2 * nr:3 * nr, :]
    gr = xr_ref[3 * nr:4 * nr, :]
    qkr = jnp.sum(qr * kr, axis=1, keepdims=True)
    for h in range(nr):
        gamma = math.exp(_ret_log_gamma(h))
        state = sr_ref[h]
        v_row = vr[h:h + 1, :]
        o = qkr[h:h + 1, :] * v_row + _vec_mat(qr[h:h + 1, :], state) * gamma
        sr_out[h] = gamma * state + _row_to_col(kr[h:h + 1, :]) * v_row
        yr_ref[h:h + 1, :] = (_rms_norm(o) * gr_ref[h:h + 1, :] * _silu(gr[h:h + 1, :])).astype(BF16)


def _odd_sample(xh, xr, lb3, cos, sin, gain_h, gain_r, state_h, state_r, layer, layer_j):
    db = xh.shape[0]
    per_b3 = lambda b: (b, 0, 0)
    return pl.pallas_call(
        functools.partial(_odd_sample_kernel, layer=layer),
        grid=(db,),
        in_specs=[
            pl.BlockSpec((None, 4 * HGRN_HEADS, HGRN_DK), per_b3),
            pl.BlockSpec((None, 4 * RET_HEADS, RET_DK), per_b3),
            pl.BlockSpec((DEPTH, HGRN_HEADS, HGRN_DK), lambda b: (0, 0, 0)),
            pl.BlockSpec((1, RET_DK // 2), lambda b: (0, 0)),
            pl.BlockSpec((1, RET_DK // 2), lambda b: (0, 0)),
            pl.BlockSpec((None, HGRN_HEADS, HGRN_DV), lambda b: (layer_j, 0, 0)),
            pl.BlockSpec((None, RET_HEADS, RET_DK), lambda b: (layer_j, 0, 0)),
            pl.BlockSpec((None, None, HGRN_HEADS, HGRN_DK, HGRN_DV), lambda b: (layer_j, b, 0, 0, 0)),
            pl.BlockSpec((None, None, RET_HEADS, RET_DK, RET_DK), lambda b: (layer_j, b, 0, 0, 0)),
        ],
        out_specs=[
            pl.BlockSpec((None, HGRN_HEADS, HGRN_DV), per_b3),
            pl.BlockSpec((None, RET_HEADS, RET_DK), per_b3),
            pl.BlockSpec((None, HGRN_HEADS, HGRN_DK, HGRN_DV), lambda b: (b, 0, 0, 0)),
            pl.BlockSpec((None, RET_HEADS, RET_DK, RET_DK), lambda b: (b, 0, 0, 0)),
        ],
        out_shape=[
            jax.ShapeDtypeStruct((db, HGRN_HEADS, HGRN_DV), BF16),
            jax.ShapeDtypeStruct((db, RET_HEADS, RET_DK), BF16),
            jax.ShapeDtypeStruct((db, HGRN_HEADS, HGRN_DK, HGRN_DV), F32),
            jax.ShapeDtypeStruct((db, RET_HEADS, RET_DK, RET_DK), F32),
        ],
        compiler_params=_cparams("parallel"),
        name="odd_sample",
    )(xh, xr, lb3, cos, sin, gain_h, gain_r, state_h, state_r)


def _even_weight_groups(w):
    w_t = jnp.swapaxes(w, 1, 2)
    gates0 = ZA_WIDTH
    qb0 = gates0 + 2 * MLSTM_HEADS
    ki0 = qb0 + ZB_WIDTH
    end = ki0 + IDX_DIM + IDX_HEADS
    pad = jnp.zeros((w.shape[0], LANES - (end - ki0) - (qb0 - gates0), w.shape[1]), w.dtype)
    w_c = jnp.concatenate([w_t[:, ki0:end], w_t[:, gates0:qb0], pad], axis=1)
    return w_t, w_t[:, qb0:ki0], w_c


def _rot_tables(pos):
    inv = 1.0 / (RET_THETA ** jnp.linspace(0.0, 1.0, RET_DK // 2, dtype=F32))
    ang = pos.astype(F32)[:, None] * inv[None, :]
    return jnp.cos(ang), jnp.sin(ang)


def kernel(x_prompt, x_sample, state_mlstm_C, state_mlstm_n, state_mlstm_m, cache_k, cache_v, cache_idx_k,
           state_hgrn, state_ret, page_table, p_prompt, p_sample, ln_g, ln_b, w_ffn_up, w_ffn_down,
           w_in_even, b_gate_mlstm, g_mlstm, w_in_odd, hgrn_lb, g_hgrn, g_ret, w_out, w_pe_gate, w_pe_proj):
    batch, seq, _ = x_prompt.shape
    db, dec_seq, _ = x_sample.shape
    assert dec_seq == 1 and all(seq % c == 0 for c in (MLSTM_CHUNK, RET_CHUNK, HGRN_CHUNK))
    n_even = w_in_even.shape[0]
    n_odd = w_in_odd.shape[0]
    n_pages = page_table.shape[1]
    past = n_pages * PAGE_SIZE
    mp = batch * seq

    w_up = w_ffn_up.astype(BF16)
    w_down = w_ffn_down.astype(BF16)
    w_even_a, w_even_b, w_even_c = _even_weight_groups(w_in_even)
    w_o = w_out.astype(BF16)
    w_pg = w_pe_gate.astype(BF16)
    w_pp = w_pe_proj.astype(BF16)
    ln_g3 = ln_g.reshape(DEPTH * 3, 1, D_MODEL)
    ln_b3 = ln_b.reshape(DEPTH * 3, 1, D_MODEL)
    pp = p_prompt.reshape(DEPTH, mp, P_DIM)
    ps = p_sample.reshape(DEPTH, db, P_DIM)
    cos_p, sin_p = _rot_tables(jnp.arange(seq))
    cos_s, sin_s = _rot_tables(past + jnp.arange(dec_seq))
    lb3 = hgrn_lb.reshape(DEPTH, 1, HGRN_HEADS * HGRN_DK)
    lb_heads = hgrn_lb.reshape(DEPTH, HGRN_HEADS, HGRN_DK)
    bias_row = jnp.zeros((n_even, 1, LANES), F32).at[:, 0, MISC_IG:MISC_IG + 2 * MLSTM_HEADS].set(b_gate_mlstm)
    bias_col = b_gate_mlstm.reshape(n_even, 2 * MLSTM_HEADS, 1)
    gain_m3 = g_mlstm.reshape(n_even, 1, HALF)
    gain_h3 = g_hgrn.reshape(n_odd, 1, HALF)
    gain_r3 = g_ret.reshape(n_odd, 1, HALF)
    cache_k3 = cache_k.reshape(cache_k.shape[:2] + (PAGE_SIZE * ATT_KV_HEADS, ATT_DH))
    cache_v3 = cache_v.reshape(cache_v.shape[:2] + (PAGE_SIZE * ATT_KV_HEADS, ATT_DH))
    cache_ik_t = jnp.swapaxes(cache_idx_k, 2, 3)
    m_state = state_mlstm_m.reshape(n_even, db, MLSTM_HEADS, 1)

    xp = x_prompt.reshape(mp, D_MODEL)
    xs = x_sample.reshape(db, D_MODEL)
    out_even = {k: [] for k in ("C_p", "C_s", "n_p", "n_s", "m_p", "m_s", "k_p", "k_s", "v_p", "v_s", "ik_p", "ik_s")}
    out_odd = {k: [] for k in ("h_p", "h_s", "r_p", "r_s")}

    for layer in range(DEPTH):
        j = layer // 2
        xp, xp_b = _ffn(xp, w_up, w_down, ln_g3, ln_b3, layer, 0, 0)
        xs, xs_b = _ffn(xs, w_up, w_down, ln_g3, ln_b3, layer, 0, 0)
        if layer % 2 == 0:
            za_p, za_s = (_proj_t(xb, w_even_a, j, ZA_WIDTH) for xb in (xp_b, xs_b))
            zb_p, zb_s = (_proj_t(xb, w_even_b, j, ZB_WIDTH) for xb in (xp_b, xs_b))
            zc_p, zc_s = (_proj_t(xb, w_even_c, j, LANES) for xb in (xp_b, xs_b))
            ya_p, c_p, n_p, m_p = _mlstm_prompt(za_p, zc_p, bias_row[j], gain_m3, batch, seq, j)
            yb_p = _dsa_prompt(zb_p, zc_p, batch, seq)

            x4 = za_s.reshape(db, 4 * MLSTM_HEADS, MLSTM_DH)
            gates_s = zc_s[:, MISC_IG:MISC_IG + 2 * MLSTM_HEADS].reshape(db, 2 * MLSTM_HEADS, 1)
            ya_s, c_s, n_s, m_s = _mlstm_sample(x4, gates_s, bias_col[j], g_mlstm.reshape(n_even, MLSTM_HEADS, MLSTM_DH),
                                                state_mlstm_C, state_mlstm_n, m_state, j)
            q_s = zb_s[:, ZB_Q:ZB_K].reshape(db, ATT_HEADS, ATT_DH)
            k_s = zb_s[:, ZB_K:ZB_V].reshape(db, ATT_KV_HEADS, ATT_DH)
            v_s = zb_s[:, ZB_V:ZB_QI].reshape(db, ATT_KV_HEADS, ATT_DH)
            qi_s = zb_s[:, ZB_QI:].reshape(db, IDX_HEADS, IDX_DIM)
            ki_s = zc_s[:, MISC_KI:MISC_KI + IDX_DIM].reshape(db, 1, IDX_DIM)
            wi_s = zc_s[:, MISC_WI:MISC_WI + IDX_HEADS].reshape(db, IDX_HEADS, 1)
            scores = _dsa_sample_scores(page_table, cache_ik_t, qi_s, wi_s, j)
            select = _dsa_sample_select(scores, qi_s, wi_s, ki_s)
            yb_s = _dsa_sample_attn(page_table, scores, select, q_s, k_s, v_s, cache_k3, cache_v3, j)
            ya_s = ya_s.reshape(db, HALF)
            yb_s = yb_s.reshape(db, HALF)

            out_even["C_p"].append(c_p)
            out_even["C_s"].append(c_s)
            out_even["n_p"].append(n_p.reshape(batch, MLSTM_HEADS, MLSTM_DH))
            out_even["n_s"].append(n_s)
            out_even["m_p"].append(m_p[:, :, 0, 0])
            out_even["m_s"].append(m_s[:, :, 0])
            out_even["k_p"].append(zb_p[:, ZB_K:ZB_V].reshape(batch, seq, ATT_KV_HEADS, ATT_DH))
            out_even["k_s"].append(k_s.reshape(db, dec_seq, ATT_KV_HEADS, ATT_DH))
            out_even["v_p"].append(zb_p[:, ZB_V:ZB_QI].reshape(batch, seq, ATT_KV_HEADS, ATT_DH))
            out_even["v_s"].append(v_s.reshape(db, dec_seq, ATT_KV_HEADS, ATT_DH))
            out_even["ik_p"].append(zc_p[:, MISC_KI:MISC_KI + IDX_DIM].reshape(batch, seq, IDX_DIM))
            out_even["ik_s"].append(ki_s.reshape(db, dec_seq, IDX_DIM))
        else:
            zp = _proj(xp_b, w_in_odd, j)
            zs = _proj(xs_b, w_in_odd, j)
            ya_p, h_p = _hgrn_prompt(zp, lb3, gain_h3, batch, seq, layer, j)
            yb_p, r_p = _ret_prompt(zp, cos_p, sin_p, gain_r3, batch, seq, j)
            xh = zs[:, :4 * HALF].reshape(db, 4 * HGRN_HEADS, HGRN_DK)
            xr = zs[:, 4 * HALF:].reshape(db, 4 * RET_HEADS, RET_DK)
            ya_s, yb_s, h_s, r_s = _odd_sample(
                xh, xr, lb_heads, cos_s, sin_s, g_hgrn.reshape(n_odd, HGRN_HEADS, HGRN_DV),
                g_ret.reshape(n_odd, RET_HEADS, RET_DK), state_hgrn, state_ret, layer, j)
            ya_s = ya_s.reshape(db, HALF)
            yb_s = yb_s.reshape(db, HALF)
            out_odd["h_p"].append(h_p)
            out_odd["h_s"].append(h_s)
            out_odd["r_p"].append(r_p)
            out_odd["r_s"].append(r_s)

        xp = _outproj(xp, ya_p, yb_p, w_o, ln_g3, ln_b3, layer)
        xs = _outproj(xs, ya_s, yb_s, w_o, ln_g3, ln_b3, layer)
        xp, xp_b = _ffn(xp, w_up, w_down, ln_g3, ln_b3, layer, 1, 2)
        xs, xs_b = _ffn(xs, w_up, w_down, ln_g3, ln_b3, layer, 1, 2)
        xp = _pe(xp, xp_b, pp, w_pg, w_pp, layer)
        xs = _pe(xs, xs_b, ps, w_pg, w_pp, layer)

    def stk(name, table, like):
        return jnp.stack(table[name]).astype(like.dtype)

    return (
        xp.reshape(batch, seq, D_MODEL), xs.reshape(db, dec_seq, D_MODEL),
        stk("C_p", out_even, state_mlstm_C), stk("C_s", out_even, state_mlstm_C),
        stk("n_p", out_even, state_mlstm_n), stk("n_s", out_even, state_mlstm_n),
        stk("m_p", out_even, state_mlstm_m), stk("m_s", out_even, state_mlstm_m),
        stk("k_p", out_even, cache_k), stk("k_s", out_even, cache_k),
        stk("v_p", out_even, cache_v), stk("v_s", out_even, cache_v),
        stk("ik_p", out_even, cache_idx_k), stk("ik_s", out_even, cache_idx_k),
        stk("h_p", out_odd, state_hgrn), stk("h_s", out_odd, state_hgrn),
        stk("r_p", out_odd, state_ret), stk("r_s", out_odd, state_ret),
    )
```

```python
import functools
import math

import jax
import jax.numpy as jnp
from jax import lax
from jax.experimental import pallas as pl
from jax.experimental.pallas import tpu as pltpu

F32 = jnp.float32
BF16 = jnp.bfloat16

D_MODEL = 2048
DEPTH = 2
PAGE_SIZE = 128
HALF = D_MODEL // 2
MLSTM_HEADS = 4
MLSTM_DH = HALF // MLSTM_HEADS
ATT_HEADS = 8
ATT_DH = HALF // ATT_HEADS
ATT_KV_HEADS = 2
ATT_GROUP = ATT_HEADS // ATT_KV_HEADS
IDX_HEADS = 8
IDX_DIM = 64
TOPK_MAX = 256
HGRN_HEADS = 8
HGRN_DK = 128
HGRN_DV = HALF // HGRN_HEADS
RET_HEADS = 4
RET_DK = HALF // RET_HEADS
RET_THETA = 10000.0
D_FF = 5632
P_DIM = 256
ALPHA = (2 * DEPTH) ** 0.25
LN_EPS = 1e-5

LANES = 128
SUBLANES = 8

ZA_Q, ZA_K, ZA_V, ZA_O = 0, HALF, 2 * HALF, 3 * HALF
ZA_WIDTH = 4 * HALF
ZB_Q = 0
ZB_K = ZB_Q + ATT_HEADS * ATT_DH
ZB_V = ZB_K + ATT_KV_HEADS * ATT_DH
ZB_QI = ZB_V + ATT_KV_HEADS * ATT_DH
ZB_WIDTH = ZB_QI + IDX_HEADS * IDX_DIM
MISC_KI, MISC_WI = 0, IDX_DIM
MISC_IG = MISC_WI + IDX_HEADS
MISC_FG = MISC_IG + MLSTM_HEADS

MLSTM_CHUNK = 128
RET_CHUNK = 128
HGRN_CHUNK = 64
HGRN_GROUP = 8
DSA_Q_TILE = 256
DSA_WIDTH_STEP = 512
PAGE_GROUP = 16
VMEM_LIMIT = 56 * 1024 * 1024
INT_MIN = -2 ** 31
INT_MAX = 2 ** 31 - 1
F32_INF_BITS = 0x7F800000
NEG_BIG = -1e30

NT_DIMS = (((1,), (1,)), ((), ()))
TN_DIMS = (((0,), (0,)), ((), ()))


def _cparams(*sem):
    return pltpu.CompilerParams(dimension_semantics=sem, vmem_limit_bytes=VMEM_LIMIT)


def _row_tile(m, largest=512):
    for t in (1024, 512, 256, 128, 64, 32, 16, 8):
        if t <= largest and m % t == 0:
            return t
    raise ValueError(f"row count {m} is not a multiple of 8")


def _col_tile(n):
    for t in (1024, 896, 512, 256, 128):
        if n % t == 0:
            return t
    raise ValueError(f"column count {n} is not a multiple of 128")


def _layer_norm(y, g, b):
    mu = jnp.mean(y, -1, keepdims=True)
    d = y - mu
    var = jnp.mean(d * d, -1, keepdims=True)
    return d * lax.rsqrt(var + LN_EPS) * g + b


def _rms_norm(h):
    return h * lax.rsqrt(jnp.mean(h * h, -1, keepdims=True) + LN_EPS)


def _sigmoid(x):
    return jax.nn.sigmoid(x)


def _silu(x):
    return x * jax.nn.sigmoid(x)


def _log_sigmoid(x):
    return jnp.minimum(x, 0.0) - jnp.log1p(jnp.exp(-jnp.abs(x)))


def _iota(shape, dim):
    return lax.broadcasted_iota(jnp.int32, shape, dim)


def _row_to_col(row):
    n = row.shape[1]
    eye = _iota((n, n), 0) == _iota((n, n), 1)
    return jnp.sum(jnp.where(eye, row, 0.0), axis=1, keepdims=True)


def _bdot(a, b):
    return jnp.dot(a.astype(BF16), b.astype(BF16), preferred_element_type=F32)


def _bdot_nt(a, b):
    return lax.dot_general(a.astype(BF16), b.astype(BF16), NT_DIMS, preferred_element_type=F32)


def _bdot_tn(a, b):
    return lax.dot_general(a.astype(BF16), b.astype(BF16), TN_DIMS, preferred_element_type=F32)


def _ffn_kernel(x_ref, wg_ref, wu_ref, wd_ref, g_ref, b_ref, o_ref, ob_ref, xb_ref, acc_ref):
    j = pl.program_id(1)

    @pl.when(j == 0)
    def _():
        xb_ref[...] = x_ref[...].astype(BF16)
        acc_ref[...] = jnp.zeros_like(acc_ref)

    xb = xb_ref[...]
    hg = jnp.dot(xb, wg_ref[...], preferred_element_type=F32)
    hu = jnp.dot(xb, wu_ref[...], preferred_element_type=F32)
    act = _silu(hg) * hu
    acc_ref[...] += jnp.dot(act.astype(BF16), wd_ref[...], preferred_element_type=F32)

    @pl.when(j == pl.num_programs(1) - 1)
    def _():
        y = _layer_norm(ALPHA * x_ref[...] + 0.5 * acc_ref[...], g_ref[...], b_ref[...])
        o_ref[...] = y
        ob_ref[...] = y.astype(BF16)


def _ffn(x, w_up, w_down, ln_g, ln_b, layer, which, ln_idx):
    m = x.shape[0]
    tm, tf = _row_tile(m), 512
    nf = D_FF // tf
    ln_row = layer * 3 + ln_idx
    return pl.pallas_call(
        _ffn_kernel,
        grid=(m // tm, nf),
        in_specs=[
            pl.BlockSpec((tm, D_MODEL), lambda i, j: (i, 0)),
            pl.BlockSpec((None, None, D_MODEL, tf), lambda i, j: (layer, which, 0, j)),
            pl.BlockSpec((None, None, D_MODEL, tf), lambda i, j: (layer, which, 0, j + nf)),
            pl.BlockSpec((None, None, tf, D_MODEL), lambda i, j: (layer, which, j, 0)),
            pl.BlockSpec((None, 1, D_MODEL), lambda i, j: (ln_row, 0, 0)),
            pl.BlockSpec((None, 1, D_MODEL), lambda i, j: (ln_row, 0, 0)),
        ],
        out_specs=[pl.BlockSpec((tm, D_MODEL), lambda i, j: (i, 0))] * 2,
        out_shape=[jax.ShapeDtypeStruct((m, D_MODEL), F32), jax.ShapeDtypeStruct((m, D_MODEL), BF16)],
        scratch_shapes=[pltpu.VMEM((tm, D_MODEL), BF16), pltpu.VMEM((tm, D_MODEL), F32)],
        compiler_params=_cparams("parallel", "arbitrary"),
        name="ffn",
    )(x, w_up, w_up, w_down, ln_g, ln_b)


def _proj_kernel(x_ref, w_ref, o_ref, wb_ref):
    @pl.when(pl.program_id(1) == 0)
    def _():
        wb_ref[...] = w_ref[...].astype(BF16)

    o_ref[...] = jnp.dot(x_ref[...], wb_ref[...], preferred_element_type=F32)


def _proj(xb, w, idx):
    m, n = xb.shape[0], w.shape[2]
    tm, tn = _row_tile(m, 1024), _col_tile(n)
    return pl.pallas_call(
        _proj_kernel,
        grid=(n // tn, m // tm),
        in_specs=[
            pl.BlockSpec((tm, D_MODEL), lambda j, i: (i, 0)),
            pl.BlockSpec((None, D_MODEL, tn), lambda j, i: (idx, 0, j)),
        ],
        out_specs=pl.BlockSpec((tm, tn), lambda j, i: (i, j)),
        out_shape=jax.ShapeDtypeStruct((m, n), F32),
        scratch_shapes=[pltpu.VMEM((D_MODEL, tn), BF16)],
        compiler_params=_cparams("parallel", "arbitrary"),
        name="in_proj",
    )(xb, w)


def _proj_t_kernel(x_ref, w_ref, o_ref, wb_ref):
    @pl.when(pl.program_id(1) == 0)
    def _():
        wb_ref[...] = w_ref[...].astype(BF16)

    o_ref[...] = lax.dot_general(x_ref[...], wb_ref[...], NT_DIMS, preferred_element_type=F32)


def _proj_t(xb, w_t, idx, n_cols):
    m = xb.shape[0]
    tm, tn = _row_tile(m, 1024), _col_tile(n_cols)
    return pl.pallas_call(
        _proj_t_kernel,
        grid=(n_cols // tn, m // tm),
        in_specs=[
            pl.BlockSpec((tm, D_MODEL), lambda j, i: (i, 0)),
            pl.BlockSpec((None, tn, D_MODEL), lambda j, i: (idx, j, 0)),
        ],
        out_specs=pl.BlockSpec((tm, tn), lambda j, i: (i, j)),
        out_shape=jax.ShapeDtypeStruct((m, n_cols), F32),
        scratch_shapes=[pltpu.VMEM((tn, D_MODEL), BF16)],
        compiler_params=_cparams("parallel", "arbitrary"),
        name="in_proj_t",
    )(xb, w_t)


def _outproj_kernel(x_ref, ya_ref, yb_ref, wa_ref, wb_ref, g_ref, b_ref, o_ref):
    tm = x_ref.shape[0]
    parts = 2 if tm % (2 * SUBLANES * 2) == 0 else 1
    for r in range(parts):
        rows = slice(r * (tm // parts), (r + 1) * (tm // parts))
        y = jnp.dot(ya_ref[rows, :], wa_ref[...], preferred_element_type=F32)
        y += jnp.dot(yb_ref[rows, :], wb_ref[...], preferred_element_type=F32)
        o_ref[rows, :] = _layer_norm(ALPHA * x_ref[rows, :] + y, g_ref[...], b_ref[...])


def _outproj(x, ya, yb, w_out, ln_g, ln_b, layer):
    m = x.shape[0]
    tm = _row_tile(m)
    ln_row = layer * 3 + 1
    return pl.pallas_call(
        _outproj_kernel,
        grid=(m // tm,),
        in_specs=[
            pl.BlockSpec((tm, D_MODEL), lambda i: (i, 0)),
            pl.BlockSpec((tm, HALF), lambda i: (i, 0)),
            pl.BlockSpec((tm, HALF), lambda i: (i, 0)),
            pl.BlockSpec((None, HALF, D_MODEL), lambda i: (layer, 0, 0)),
            pl.BlockSpec((None, HALF, D_MODEL), lambda i: (layer, 1, 0)),
            pl.BlockSpec((None, 1, D_MODEL), lambda i: (ln_row, 0, 0)),
            pl.BlockSpec((None, 1, D_MODEL), lambda i: (ln_row, 0, 0)),
        ],
        out_specs=pl.BlockSpec((tm, D_MODEL), lambda i: (i, 0)),
        out_shape=jax.ShapeDtypeStruct((m, D_MODEL), F32),
        compiler_params=_cparams("parallel"),
        name="out_proj",
    )(x, ya, yb, w_out, w_out, ln_g, ln_b)


def _pe_kernel(x_ref, xb_ref, p_ref, wg_ref, wp_ref, o_ref):
    gate = _sigmoid(jnp.dot(xb_ref[...], wg_ref[...], preferred_element_type=F32))
    o_ref[...] = x_ref[...] + gate * jnp.dot(p_ref[...].astype(BF16), wp_ref[...], preferred_element_type=F32)


def _pe(x, xb, p, w_gate, w_proj, layer):
    m = x.shape[0]
    tm = _row_tile(m)
    return pl.pallas_call(
        _pe_kernel,
        grid=(m // tm,),
        in_specs=[
            pl.BlockSpec((tm, D_MODEL), lambda i: (i, 0)),
            pl.BlockSpec((tm, D_MODEL), lambda i: (i, 0)),
            pl.BlockSpec((None, tm, P_DIM), lambda i: (layer, i, 0)),
            pl.BlockSpec((None, D_MODEL, D_MODEL), lambda i: (layer, 0, 0)),
            pl.BlockSpec((None, P_DIM, D_MODEL), lambda i: (layer, 0, 0)),
        ],
        out_specs=pl.BlockSpec((tm, D_MODEL), lambda i: (i, 0)),
        out_shape=jax.ShapeDtypeStruct((m, D_MODEL), F32),
        compiler_params=_cparams("parallel"),
        name="pe_gate",
    )(x, xb, p, w_gate, w_proj)


def _mlstm_prompt_kernel(q_ref, k_ref, v_ref, o_ref, misc_ref, bias_ref, gain_ref,
                         y_ref, c_ref, n_ref, m_ref, m_sc):
    chunk = q_ref.shape[0]
    dh = MLSTM_DH

    @pl.when(pl.program_id(1) == 0)
    def _():
        c_ref[...] = jnp.zeros_like(c_ref)
        n_ref[...] = jnp.zeros_like(n_ref)
        m_sc[...] = jnp.zeros_like(m_sc)

    gates = misc_ref[...] + bias_ref[...]
    ti = _iota((chunk, chunk), 0)
    si = _iota((chunk, chunk), 1)
    eye = ti == si
    causal = si <= ti
    for h in range(MLSTM_HEADS):
        cols = slice(h * dh, (h + 1) * dh)
        ig_col = gates[:, MISC_IG + h:MISC_IG + h + 1]
        lf_col = _log_sigmoid(gates[:, MISC_FG + h:MISC_FG + h + 1])
        lf_row = jnp.sum(jnp.where(eye, lf_col, 0.0), axis=0, keepdims=True)
        ig_row = jnp.sum(jnp.where(eye, ig_col, 0.0), axis=0, keepdims=True)
        b_col = jnp.sum(jnp.where(causal, lf_row, 0.0), axis=1, keepdims=True)
        b_row = jnp.sum(jnp.where(ti <= si, lf_col, 0.0), axis=0, keepdims=True)
        m_prev = m_sc[h][:, :1]
        dlog = jnp.where(causal, b_col - b_row + ig_row, -jnp.inf)
        g_col = b_col + m_prev
        m_t = jnp.maximum(g_col, jnp.max(dlog, axis=1, keepdims=True))
        dw = jnp.exp(dlog - m_t)
        gw = jnp.exp(g_col - m_t)

        q = q_ref[:, cols] * (dh ** -0.5)
        k = k_ref[:, cols]
        v = v_ref[:, cols]
        c_state = c_ref[h]
        n_state = n_ref[h]
        s = _bdot_nt(q, k) * dw
        num = _bdot(s, v) + gw * _bdot(q, c_state)
        den = jnp.sum(s, axis=1, keepdims=True) + gw * jnp.sum(q * n_state, axis=1, keepdims=True)
        hid = num / jnp.maximum(jnp.abs(den), jnp.exp(-m_t))

        b_last = b_col[chunk - 1:chunk, :]
        wlog = b_last - b_col + ig_col
        m_new = jnp.maximum(b_last + m_prev, jnp.max(wlog, axis=0, keepdims=True))
        w_col = jnp.exp(wlog - m_new)
        decay = jnp.exp(b_last + m_prev - m_new)
        kw = k * w_col
        c_ref[h] = decay * c_state + _bdot_tn(kw, v)
        n_ref[h] = decay * n_state + jnp.sum(kw, axis=0, keepdims=True)
        m_row = jnp.broadcast_to(m_new, (1, LANES))
        m_sc[h] = m_row
        m_ref[h] = m_row

        y_ref[:, cols] = (_rms_norm(hid) * gain_ref[:, cols] * _sigmoid(o_ref[:, cols])).astype(BF16)


def _mlstm_prompt(za, zc, bias_row, gain, batch, seq, layer_j):
    chunk = MLSTM_CHUNK
    nc = seq // chunk
    nh, dh = MLSTM_HEADS, MLSTM_DH

    def col(base):
        return lambda b, c: (b * nc + c, base // HALF)

    state = lambda b, c: (b, 0, 0, 0)
    return pl.pallas_call(
        _mlstm_prompt_kernel,
        grid=(batch, nc),
        in_specs=[
            pl.BlockSpec((chunk, HALF), col(ZA_Q)),
            pl.BlockSpec((chunk, HALF), col(ZA_K)),
            pl.BlockSpec((chunk, HALF), col(ZA_V)),
            pl.BlockSpec((chunk, HALF), col(ZA_O)),
            pl.BlockSpec((chunk, LANES), lambda b, c: (b * nc + c, 0)),
            pl.BlockSpec((1, LANES), lambda b, c: (0, 0)),
            pl.BlockSpec((None, 1, HALF), lambda b, c: (layer_j, 0, 0)),
        ],
        out_specs=[
            pl.BlockSpec((chunk, HALF), lambda b, c: (b * nc + c, 0)),
            pl.BlockSpec((None, nh, dh, dh), state),
            pl.BlockSpec((None, nh, 1, dh), state),
            pl.BlockSpec((None, nh, 1, LANES), state),
        ],
        out_shape=[
            jax.ShapeDtypeStruct((batch * seq, HALF), BF16),
            jax.ShapeDtypeStruct((batch, nh, dh, dh), F32),
            jax.ShapeDtypeStruct((batch, nh, 1, dh), F32),
            jax.ShapeDtypeStruct((batch, nh, 1, LANES), F32),
        ],
        scratch_shapes=[pltpu.VMEM((nh, 1, LANES), F32)],
        compiler_params=_cparams("parallel", "arbitrary"),
        name="mlstm_prompt",
    )(za, za, za, za, zc, bias_row, gain)


def _hgrn_lower_bound(lb_ref, layer):
    rows = [lb_ref[j] for j in range(DEPTH)]
    mx = functools.reduce(jnp.maximum, rows)
    e = [jnp.exp(r - mx) for r in rows]
    total = functools.reduce(jnp.add, e)
    acc = jnp.zeros_like(total)
    for j in range(1, layer + 1):
        acc = acc + e[j] / total
    return acc


def _block_mid_rows(b, half):
    rows, width = b.shape
    block = 2 * half
    if block >= SUBLANES:
        parts = [jnp.broadcast_to(b[r + half - 1:r + half, :], (block, width)) for r in range(0, rows, block)]
        return jnp.concatenate(parts, axis=0)
    b3 = b.reshape(rows // SUBLANES, SUBLANES, width)
    sub = _iota(b3.shape, 1)
    out = jnp.broadcast_to(b3[:, half - 1:half, :], b3.shape)
    for r in range(block, SUBLANES, block):
        out = jnp.where(sub >= r, jnp.broadcast_to(b3[:, r + half - 1:r + half, :], b3.shape), out)
    return out.reshape(rows, width)


def _hgrn_prompt_kernel(q_ref, f_ref, i_ref, g_ref, lb_ref, gain_ref, y_ref, s_ref, *, layer):
    chunk = q_ref.shape[0]

    @pl.when(pl.program_id(2) == 0)
    def _():
        s_ref[...] = jnp.zeros_like(s_ref)

    lb = _hgrn_lower_bound(lb_ref, layer)
    f_all = lb + (1.0 - lb) * _sigmoid(f_ref[...])
    lf = jnp.log(f_all)
    halves = [chunk >> (lvl + 1) for lvl in range(chunk.bit_length() - 1)]
    ti = _iota((chunk, chunk), 0)
    si = _iota((chunk, chunk), 1)
    row = _iota((chunk, 1), 0)
    level_mask = []
    second_half = []
    for half in halves:
        mid = (ti // (2 * half)) * (2 * half) + half - 1
        level_mask.append((si <= mid) & (ti > mid) & (si // (2 * half) == ti // (2 * half)))
        second_half.append(row % (2 * half) >= half)
    eye = ti == si
    b_all = jnp.dot(jnp.where(si <= ti, 1.0, 0.0), lf, preferred_element_type=F32,
                    precision=lax.Precision.HIGHEST)

    for h in range(HGRN_GROUP):
        cols = slice(h * HGRN_DK, (h + 1) * HGRN_DK)
        b = b_all[:, cols]
        kk = 1.0 - f_all[:, cols]
        q = _silu(q_ref[:, cols])
        v = i_ref[:, cols]
        b_last = b[chunk - 1:chunk, :]
        state = s_ref[h]

        attn = jnp.where(eye, jnp.sum(q * kk, axis=1, keepdims=True), 0.0)
        for lvl, half in enumerate(halves):
            x = jnp.where(second_half[lvl], q, kk) * jnp.exp(-jnp.abs(b - _block_mid_rows(b, half)))
            attn = attn + jnp.where(level_mask[lvl], _bdot_nt(x, x), 0.0)
        o = _bdot(attn, v) + _bdot(q * jnp.exp(b), state)

        s_ref[h] = _row_to_col(jnp.exp(b_last)) * state + _bdot_tn(kk * jnp.exp(b_last - b), v)
        y_ref[:, cols] = (_rms_norm(o) * gain_ref[:, cols] * _silu(g_ref[:, cols])).astype(BF16)


def _hgrn_prompt(z, lb3, gain, batch, seq, layer, layer_j):
    chunk = HGRN_CHUNK
    nc = seq // chunk
    n_groups = HGRN_HEADS // HGRN_GROUP
    dk = HGRN_GROUP * HGRN_DK

    def col(group):
        return lambda b, h, c: (b * nc + c, group * n_groups + h)

    return pl.pallas_call(
        functools.partial(_hgrn_prompt_kernel, layer=layer),
        grid=(batch, n_groups, nc),
        in_specs=[
            pl.BlockSpec((chunk, dk), col(0)),
            pl.BlockSpec((chunk, dk), col(1)),
            pl.BlockSpec((chunk, dk), col(2)),
            pl.BlockSpec((chunk, dk), col(3)),
            pl.BlockSpec((DEPTH, 1, dk), lambda b, h, c: (0, 0, h)),
            pl.BlockSpec((None, 1, dk), lambda b, h, c: (layer_j, 0, h)),
        ],
        out_specs=[
            pl.BlockSpec((chunk, dk), lambda b, h, c: (b * nc + c, h)),
            pl.BlockSpec((None, HGRN_GROUP, HGRN_DK, HGRN_DV), lambda b, h, c: (b, h, 0, 0)),
        ],
        out_shape=[
            jax.ShapeDtypeStruct((batch * seq, HALF), BF16),
            jax.ShapeDtypeStruct((batch, HGRN_HEADS, HGRN_DK, HGRN_DV), F32),
        ],
        compiler_params=_cparams("parallel", "parallel", "arbitrary"),
        name="hgrn_prompt",
    )(z, z, z, z, lb3, gain)


def _ret_log_gamma(h):
    return math.log1p(-2.0 ** (-5 - h))


def _rotate(x, cos, sin):
    half = x.shape[1] // 2
    x1, x2 = x[:, :half], x[:, half:]
    return jnp.concatenate([x1 * cos - x2 * sin, x1 * sin + x2 * cos], axis=1)


def _ret_prompt_kernel(q_ref, k_ref, v_ref, g_ref, cos_ref, sin_ref, gain_ref, y_ref, s_ref):
    chunk = q_ref.shape[0]

    @pl.when(pl.program_id(1) == 0)
    def _():
        s_ref[...] = jnp.zeros_like(s_ref)

    cos, sin = cos_ref[...], sin_ref[...]
    rel = (_iota((chunk, chunk), 0) - _iota((chunk, chunk), 1)).astype(F32)
    pos = _iota((chunk, 1), 0).astype(F32)
    for h in range(RET_HEADS):
        cols = slice(h * RET_DK, (h + 1) * RET_DK)
        lg = _ret_log_gamma(h)
        qr = _rotate(q_ref[:, cols], cos, sin)
        kr = _rotate(k_ref[:, cols], cos, sin) * (RET_DK ** -0.5)
        v = v_ref[:, cols]
        dmask = jnp.where(rel >= 0.0, jnp.exp(lg * jnp.maximum(rel, 0.0)), 0.0)
        state = s_ref[h]
        o = _bdot(_bdot_nt(qr, kr) * dmask, v) + _bdot(qr, state) * jnp.exp(lg * (pos + 1.0))
        tail = jnp.exp(lg * (chunk - 1.0 - pos))
        s_ref[h] = math.exp(lg * chunk) * state + _bdot_tn(kr * tail, v)
        y_ref[:, cols] = (_rms_norm(o) * gain_ref[:, cols] * _silu(g_ref[:, cols])).astype(BF16)


def _ret_prompt(z, cos, sin, gain, batch, seq, layer_j):
    chunk = RET_CHUNK
    nc = seq // chunk
    dk = RET_DK

    def col(group):
        return lambda b, c: (b * nc + c, 4 + group)

    return pl.pallas_call(
        _ret_prompt_kernel,
        grid=(batch, nc),
        in_specs=[
            pl.BlockSpec((chunk, HALF), col(0)),
            pl.BlockSpec((chunk, HALF), col(1)),
            pl.BlockSpec((chunk, HALF), col(2)),
            pl.BlockSpec((chunk, HALF), col(3)),
            pl.BlockSpec((chunk, dk // 2), lambda b, c: (c, 0)),
            pl.BlockSpec((chunk, dk // 2), lambda b, c: (c, 0)),
            pl.BlockSpec((None, 1, HALF), lambda b, c: (layer_j, 0, 0)),
        ],
        out_specs=[
            pl.BlockSpec((chunk, HALF), lambda b, c: (b * nc + c, 0)),
            pl.BlockSpec((None, RET_HEADS, dk, dk), lambda b, c: (b, 0, 0, 0)),
        ],
        out_shape=[
            jax.ShapeDtypeStruct((batch * seq, HALF), BF16),
            jax.ShapeDtypeStruct((batch, RET_HEADS, dk, dk), F32),
        ],
        compiler_params=_cparams("parallel", "arbitrary"),
        name="ret_prompt",
    )(z, z, z, z, cos, sin, gain)


def _ordered_float(key):
    bits = jnp.where(key < 0, (-key) | INT_MIN, key)
    return lax.bitcast_convert_type(bits, F32)


def _count(mask):
    return jnp.sum(jnp.where(mask, 1.0, 0.0), axis=1, keepdims=True)


def _kth_largest(count_ge, shape, n_sel):
    def body(it, t_off):
        cand = t_off | lax.shift_left(jnp.int32(1), 31 - it)
        return jnp.where(count_ge(_ordered_float(cand ^ INT_MIN)) >= n_sel, cand, t_off)

    t_off = lax.fori_loop(0, 32, body, jnp.zeros(shape, jnp.int32))
    return _ordered_float(jnp.maximum(t_off ^ INT_MIN, -F32_INF_BITS))


def _tie_cutoff(count_eq_below, shape, budget, n_bits):
    def body(it, cut):
        cand = cut | lax.shift_left(jnp.int32(1), n_bits - 1 - it)
        return jnp.where(count_eq_below(cand) <= budget, cand, cut)

    return lax.fori_loop(0, n_bits, body, jnp.zeros(shape, jnp.int32))


def _select_top(score, valid, kpos, n_sel, n_idx_bits):
    per_row = (score.shape[0], 1)
    thr = _kth_largest(lambda t: _count(score >= t), per_row, n_sel)
    above = score > thr
    tied = score == thr
    n_above = _count(above)
    need_cut = jnp.max(n_above + _count(tied & valid)) > n_sel
    cut = lax.cond(
        need_cut,
        lambda: _tie_cutoff(lambda j: _count(tied & (kpos < j)), per_row, n_sel - n_above, n_idx_bits),
        lambda: jnp.full(per_row, INT_MAX, jnp.int32))
    return valid & (above | (tied & (kpos < cut)))


def _dsa_prompt_body(q_ref, qi_ref, qmisc_ref, kb_ref, vb_ref, kib_ref, y_ref, n_sel, width):
    qblk = q_ref.shape[0]
    ki = kib_ref[:width, :]
    qmisc = qmisc_ref[...]
    score = jnp.zeros((qblk, width), F32)
    for h in range(IDX_HEADS):
        w = qmisc[:, MISC_WI + h:MISC_WI + h + 1] * (IDX_HEADS ** -0.5) * (IDX_DIM ** -0.5)
        score = score + jnp.maximum(_bdot_nt(qi_ref[:, h * IDX_DIM:(h + 1) * IDX_DIM], ki), 0.0) * w

    qpos = pl.program_id(1) * qblk + _iota((qblk, width), 0)
    kpos = _iota((qblk, width), 1)
    valid = kpos <= qpos
    sel = _select_top(jnp.where(valid, score, -jnp.inf), valid, kpos, n_sel, (width + 1).bit_length())
    bias = jnp.where(sel, 0.0, -jnp.inf)

    exp2_scale = (ATT_DH ** -0.5) * math.log2(math.e)
    for g in range(ATT_KV_HEADS):
        kg = kb_ref[:width, g * ATT_DH:(g + 1) * ATT_DH]
        vg = vb_ref[:width, g * ATT_DH:(g + 1) * ATT_DH]
        for hh in range(ATT_GROUP):
            cols = slice((g * ATT_GROUP + hh) * ATT_DH, (g * ATT_GROUP + hh + 1) * ATT_DH)
            logits = _bdot_nt(q_ref[:, cols], kg) + bias
            e = jnp.exp2((logits - jnp.max(logits, axis=1, keepdims=True)) * exp2_scale)
            y_ref[:, cols] = (_bdot(e, vg) / jnp.sum(e, axis=1, keepdims=True)).astype(BF16)


def _dsa_prompt_kernel(q_ref, k_ref, v_ref, qi_ref, qmisc_ref, kmisc_ref, y_ref, kb_ref, vb_ref, kib_ref, *, n_sel):
    tile = q_ref.shape[0]
    i = pl.program_id(1)

    @pl.when(i == 0)
    def _():
        kb_ref[...] = k_ref[...].astype(BF16)
        vb_ref[...] = v_ref[...].astype(BF16)
        kib_ref[...] = kmisc_ref[:, MISC_KI:MISC_KI + IDX_DIM].astype(BF16)

    seq = k_ref.shape[0]
    step = DSA_WIDTH_STEP if seq % DSA_WIDTH_STEP == 0 else seq
    tiles_per_step = step // tile
    for var in range(seq // step):
        @pl.when((i >= var * tiles_per_step) & (i < (var + 1) * tiles_per_step))
        def _(width=(var + 1) * step):
            _dsa_prompt_body(q_ref, qi_ref, qmisc_ref, kb_ref, vb_ref, kib_ref, y_ref, n_sel, width)


def _dsa_prompt(zb, zc, batch, seq):
    tile = DSA_Q_TILE if seq % DSA_WIDTH_STEP == 0 else seq
    nt = seq // tile
    n_sel = min(TOPK_MAX, seq // 4)
    kv_w = ATT_KV_HEADS * ATT_DH
    qi_w = IDX_HEADS * IDX_DIM
    return pl.pallas_call(
        functools.partial(_dsa_prompt_kernel, n_sel=n_sel),
        grid=(batch, nt),
        in_specs=[
            pl.BlockSpec((tile, HALF), lambda b, i: (b * nt + i, ZB_Q // HALF)),
            pl.BlockSpec((seq, kv_w), lambda b, i: (b, ZB_K // kv_w)),
            pl.BlockSpec((seq, kv_w), lambda b, i: (b, ZB_V // kv_w)),
            pl.BlockSpec((tile, qi_w), lambda b, i: (b * nt + i, ZB_QI // qi_w)),
            pl.BlockSpec((tile, LANES), lambda b, i: (b * nt + i, 0)),
            pl.BlockSpec((seq, LANES), lambda b, i: (b, 0)),
        ],
        out_specs=pl.BlockSpec((tile, HALF), lambda b, i: (b * nt + i, 0)),
        out_shape=jax.ShapeDtypeStruct((batch * seq, HALF), BF16),
        scratch_shapes=[pltpu.VMEM((seq, kv_w), BF16), pltpu.VMEM((seq, kv_w), BF16),
                        pltpu.VMEM((seq, IDX_DIM), BF16)],
        compiler_params=_cparams("parallel", "arbitrary"),
        name="dsa_prompt",
    )(zb, zb, zb, zb, zc, zc)


def _page_specs(block, n_pages, layer_j):
    def spec(g):
        return pl.BlockSpec((None, None) + block,
                            lambda b, s, pt: (layer_j, pt[b * n_pages + s * PAGE_GROUP + g], 0, 0))

    return [spec(g) for g in range(PAGE_GROUP)]


def _dsa_sample_score_kernel(pt_ref, *refs):
    ik_refs = refs[:PAGE_GROUP]
    qi_ref, wi_ref, o_ref = refs[PAGE_GROUP:]
    qi = qi_ref[...]
    w = wi_ref[...] * (IDX_HEADS ** -0.5)
    rows = []
    for ik_ref in ik_refs:
        r = _bdot(qi, ik_ref[...]) * (IDX_DIM ** -0.5)
        rows.append(jnp.sum(jnp.maximum(r, 0.0) * w, axis=0, keepdims=True))
    o_ref[...] = jnp.concatenate(rows, axis=0)


def _dsa_sample_scores(page_table, cache_ik_t, qi, wi, layer_j):
    dec_batch, n_pages = page_table.shape
    assert n_pages % PAGE_GROUP == 0
    grid_spec = pltpu.PrefetchScalarGridSpec(
        num_scalar_prefetch=1,
        grid=(dec_batch, n_pages // PAGE_GROUP),
        in_specs=_page_specs((IDX_DIM, PAGE_SIZE), n_pages, layer_j) + [
            pl.BlockSpec((None, IDX_HEADS, IDX_DIM), lambda b, s, pt: (b, 0, 0)),
            pl.BlockSpec((None, IDX_HEADS, 1), lambda b, s, pt: (b, 0, 0)),
        ],
        out_specs=pl.BlockSpec((None, PAGE_GROUP, PAGE_SIZE), lambda b, s, pt: (b, s, 0)),
    )
    return pl.pallas_call(
        _dsa_sample_score_kernel,
        grid_spec=grid_spec,
        out_shape=jax.ShapeDtypeStruct((dec_batch, n_pages, PAGE_SIZE), F32),
        compiler_params=_cparams("parallel", "arbitrary"),
        name="dsa_sample_scores",
    )(page_table.reshape(-1), *([cache_ik_t] * PAGE_GROUP), qi, wi)


def _dsa_sample_select_kernel(sc_ref, qi_ref, wi_ref, kin_ref, thr_ref, cut_ref, new_ref, *, n_sel):
    past = sc_ref.shape[1] * PAGE_SIZE
    r_new = jnp.sum(qi_ref[...] * kin_ref[...], axis=2, keepdims=True) * (IDX_DIM ** -0.5)
    key_new = jnp.sum(jnp.maximum(r_new, 0.0) * (wi_ref[...] * (IDX_HEADS ** -0.5)), axis=1, keepdims=True)
    key = sc_ref[...]
    idx = _iota(key.shape, 1) * PAGE_SIZE + _iota(key.shape, 2)
    per_req = key_new.shape

    def total(mask, mask_new):
        in_page = jnp.sum(jnp.where(mask, 1.0, 0.0), axis=2, keepdims=True)
        return jnp.sum(in_page, axis=1, keepdims=True) + jnp.where(mask_new, 1.0, 0.0)

    thr = _kth_largest(lambda t: total(key >= t, key_new >= t), per_req, n_sel)
    above, tied = key > thr, key == thr
    n_above = total(above, key_new > thr)
    need_cut = jnp.max(n_above + total(tied, key_new == thr)) > n_sel
    cut = lax.cond(
        need_cut,
        lambda: _tie_cutoff(lambda j: total(tied & (idx < j), (key_new == thr) & (past < j)),
                            per_req, n_sel - n_above, (past + 2).bit_length()),
        lambda: jnp.full(per_req, INT_MAX, jnp.int32))
    thr_ref[...] = jnp.broadcast_to(thr, thr_ref.shape)
    cut_ref[...] = jnp.broadcast_to(cut, cut_ref.shape)
    new_ref[...] = jnp.broadcast_to(key_new, new_ref.shape)


def _dsa_sample_select(scores, qi, wi, ki_new):
    dec_batch, n_pages, _ = scores.shape
    n_sel = min(TOPK_MAX, (n_pages * PAGE_SIZE + 1) // 4)
    out = lambda dtype: jax.ShapeDtypeStruct((dec_batch, 1, LANES), dtype)
    return pl.pallas_call(
        functools.partial(_dsa_sample_select_kernel, n_sel=n_sel),
        out_shape=[out(F32), out(jnp.int32), out(F32)],
        compiler_params=pltpu.CompilerParams(vmem_limit_bytes=VMEM_LIMIT),
        name="dsa_sample_select",
    )(scores, qi, wi, ki_new)


def _dsa_sample_attn_kernel(pt_ref, sc_ref, thr_ref, cut_ref, new_ref, q_ref, knew_ref, vnew_ref, *refs):
    kc_refs = refs[:PAGE_GROUP]
    vc_refs = refs[PAGE_GROUP:2 * PAGE_GROUP]
    y_ref, m_sc, l_sc, acc_sc = refs[2 * PAGE_GROUP:]
    step = pl.program_id(1)
    past = sc_ref.shape[0] * PAGE_SIZE
    group_w = PAGE_GROUP * PAGE_SIZE * ATT_KV_HEADS

    @pl.when(step == 0)
    def _():
        m_sc[...] = jnp.full_like(m_sc, NEG_BIG)
        l_sc[...] = jnp.zeros_like(l_sc)
        acc_sc[...] = jnp.zeros_like(acc_sc)

    thr = thr_ref[:, :1]
    cut = cut_ref[:, :1]
    scale = ATT_DH ** -0.5
    q = q_ref[...]
    first_group = _iota((ATT_HEADS, 1), 0) < ATT_GROUP

    def by_group(fn):
        return jnp.where(first_group, fn(0), fn(1))

    def online_update(logits, pv):
        m_old = m_sc[...]
        m_new = jnp.maximum(m_old, jnp.max(logits, axis=1, keepdims=True))
        alpha = jnp.exp(m_old - m_new)
        pr = jnp.exp(logits - m_new)
        l_sc[...] = alpha * l_sc[...] + jnp.sum(pr, axis=1, keepdims=True)
        acc_sc[...] = alpha * acc_sc[...] + pv(pr)
        m_sc[...] = m_new

    first_page = pl.multiple_of(step * PAGE_GROUP, PAGE_GROUP)
    key = sc_ref[pl.ds(first_page, PAGE_GROUP), :]
    idx = (first_page + _iota(key.shape, 0)) * PAGE_SIZE + _iota(key.shape, 1)
    sel = (key > thr) | ((key == thr) & (idx < cut))
    exp_shape = (PAGE_SIZE, PAGE_SIZE * ATT_KV_HEADS)
    expand = jnp.where(_iota(exp_shape, 1) // ATT_KV_HEADS == _iota(exp_shape, 0), 1.0, 0.0)
    sel_wide = _bdot(jnp.where(sel, 1.0, 0.0), expand)
    sel_row = jnp.concatenate([sel_wide[g:g + 1, :] for g in range(PAGE_GROUP)], axis=1)
    col_head = _iota((ATT_HEADS, group_w), 1) % ATT_KV_HEADS
    row_head = _iota((ATT_HEADS, group_w), 0) // ATT_GROUP
    mask = (sel_row > 0.5) & (col_head == row_head)
    kc = jnp.concatenate([r[...] for r in kc_refs], axis=0)
    vc = jnp.concatenate([r[...] for r in vc_refs], axis=0)
    logits = jnp.where(mask, _bdot_nt(q, kc) * scale, -jnp.inf)
    online_update(logits, lambda pr: _bdot(pr, vc))

    @pl.when(step == pl.num_programs(1) - 1)
    def _():
        key_new = new_ref[:, :1]
        sel_new = (key_new > thr) | ((key_new == thr) & (past < cut))
        k_new = by_group(lambda g: knew_ref[g:g + 1, :])
        v_new = by_group(lambda g: vnew_ref[g:g + 1, :])
        logit_new = jnp.sum(q * k_new, axis=1, keepdims=True) * scale
        logit_new = jnp.where(sel_new, logit_new, -jnp.inf)
        online_update(logit_new, lambda pr: pr * v_new)
        y_ref[...] = (acc_sc[...] / l_sc[...]).astype(BF16)


def _dsa_sample_attn(page_table, scores, select, q, k_new, v_new, cache_k, cache_v, layer_j):
    dec_batch, n_pages = page_table.shape
    per_b = lambda b, s, pt: (b, 0, 0)
    page_specs = _page_specs((PAGE_SIZE * ATT_KV_HEADS, ATT_DH), n_pages, layer_j)
    grid_spec = pltpu.PrefetchScalarGridSpec(
        num_scalar_prefetch=1,
        grid=(dec_batch, n_pages // PAGE_GROUP),
        in_specs=[
            pl.BlockSpec((None, n_pages, PAGE_SIZE), per_b),
            pl.BlockSpec((None, 1, LANES), per_b),
            pl.BlockSpec((None, 1, LANES), per_b),
            pl.BlockSpec((None, 1, LANES), per_b),
            pl.BlockSpec((None, ATT_HEADS, ATT_DH), per_b),
            pl.BlockSpec((None, ATT_KV_HEADS, ATT_DH), per_b),
            pl.BlockSpec((None, ATT_KV_HEADS, ATT_DH), per_b),
        ] + page_specs + page_specs,
        out_specs=pl.BlockSpec((None, ATT_HEADS, ATT_DH), per_b),
        scratch_shapes=[
            pltpu.VMEM((ATT_HEADS, 1), F32),
            pltpu.VMEM((ATT_HEADS, 1), F32),
            pltpu.VMEM((ATT_HEADS, ATT_DH), F32),
        ],
    )
    return pl.pallas_call(
        _dsa_sample_attn_kernel,
        grid_spec=grid_spec,
        out_shape=jax.ShapeDtypeStruct((dec_batch, ATT_HEADS, ATT_DH), BF16),
        compiler_params=_cparams("parallel", "arbitrary"),
        name="dsa_sample_attn",
    )(page_table.reshape(-1), scores, *select, q, k_new, v_new,
      *([cache_k] * PAGE_GROUP), *([cache_v] * PAGE_GROUP))


def _vec_mat(row, mat):
    return jnp.sum(_row_to_col(row) * mat, axis=0, keepdims=True)


def _mlstm_sample_kernel(x_ref, gate_ref, bias_ref, gain_ref, c_ref, n_ref, m_ref,
                         y_ref, c_out, n_out, m_out):
    gates = gate_ref[...] + bias_ref[...]
    for h in range(MLSTM_HEADS):
        q = x_ref[h:h + 1, :] * (MLSTM_DH ** -0.5)
        k = x_ref[MLSTM_HEADS + h:MLSTM_HEADS + h + 1, :]
        v = x_ref[2 * MLSTM_HEADS + h:2 * MLSTM_HEADS + h + 1, :]
        og = x_ref[3 * MLSTM_HEADS + h:3 * MLSTM_HEADS + h + 1, :]
        ig = gates[h:h + 1, :]
        lf = _log_sigmoid(gates[MLSTM_HEADS + h:MLSTM_HEADS + h + 1, :])
        m_prev = m_ref[h:h + 1, :]
        c_state = c_ref[h]
        n_state = n_ref[h:h + 1, :]
        g = lf + m_prev
        m_t = jnp.maximum(g, ig)
        dw = jnp.exp(ig - m_t)
        gw = jnp.exp(g - m_t)
        s = jnp.sum(q * k, axis=1, keepdims=True) * dw
        num = s * v + gw * _vec_mat(q, c_state)
        den = s + gw * jnp.sum(q * n_state, axis=1, keepdims=True)
        hid = num / jnp.maximum(jnp.abs(den), jnp.exp(-m_t))
        w = jnp.exp(ig - m_t)
        decay = jnp.exp(g - m_t)
        c_out[h] = decay * c_state + _row_to_col(k * w) * v
        n_out[h:h + 1, :] = decay * n_state + w * k
        m_out[h:h + 1, :] = m_t
        y_ref[h:h + 1, :] = (_rms_norm(hid) * gain_ref[h:h + 1, :] * _sigmoid(og)).astype(BF16)


def _mlstm_sample(x, gates, bias, gain, c_state, n_state, m_state, layer_j):
    db = x.shape[0]
    nh, dh = MLSTM_HEADS, MLSTM_DH
    per_b3 = lambda b: (b, 0, 0)
    return pl.pallas_call(
        _mlstm_sample_kernel,
        grid=(db,),
        in_specs=[
            pl.BlockSpec((None, 4 * nh, dh), per_b3),
            pl.BlockSpec((None, 2 * nh, 1), per_b3),
            pl.BlockSpec((2 * nh, 1), lambda b: (0, 0)),
            pl.BlockSpec((None, nh, dh), lambda b: (layer_j, 0, 0)),
            pl.BlockSpec((None, None, nh, dh, dh), lambda b: (layer_j, b, 0, 0, 0)),
            pl.BlockSpec((None, None, nh, dh), lambda b: (layer_j, b, 0, 0)),
            pl.BlockSpec((None, None, nh, 1), lambda b: (layer_j, b, 0, 0)),
        ],
        out_specs=[
            pl.BlockSpec((None, nh, dh), per_b3),
            pl.BlockSpec((None, nh, dh, dh), lambda b: (b, 0, 0, 0)),
            pl.BlockSpec((None, nh, dh), per_b3),
            pl.BlockSpec((None, nh, 1), per_b3),
        ],
        out_shape=[
            jax.ShapeDtypeStruct((db, nh, dh), BF16),
            jax.ShapeDtypeStruct((db, nh, dh, dh), F32),
            jax.ShapeDtypeStruct((db, nh, dh), F32),
            jax.ShapeDtypeStruct((db, nh, 1), F32),
        ],
        compiler_params=_cparams("parallel"),
        name="mlstm_sample",
    )(x, gates, bias, gain, c_state, n_state, m_state)


def _odd_sample_kernel(xh_ref, xr_ref, lb_ref, cos_ref, sin_ref, gh_ref, gr_ref, sh_ref, sr_ref,
                       yh_ref, yr_ref, sh_out, sr_out, *, layer):
    nh = HGRN_HEADS
    lb = _hgrn_lower_bound(lb_ref, layer)
    f = lb + (1.0 - lb) * _sigmoid(xh_ref[nh:2 * nh, :])
    q = _silu(xh_ref[0:nh, :])
    kk = 1.0 - f
    v = xh_ref[2 * nh:3 * nh, :]
    gate = xh_ref[3 * nh:4 * nh, :]
    qk = jnp.sum(q * kk, axis=1, keepdims=True)
    for h in range(nh):
        state = sh_ref[h]
        f_col = _row_to_col(f[h:h + 1, :])
        k_col = _row_to_col(kk[h:h + 1, :])
        v_row = v[h:h + 1, :]
        o = qk[h:h + 1, :] * v_row + _vec_mat(q[h:h + 1, :] * f[h:h + 1, :], state)
        sh_out[h] = f_col * state + k_col * v_row
        yh_ref[h:h + 1, :] = (_rms_norm(o) * gh_ref[h:h + 1, :] * _silu(gate[h:h + 1, :])).astype(BF16)

    nr = RET_HEADS
    cos, sin = cos_ref[...], sin_ref[...]
    qr = _rotate(xr_ref[0:nr, :], cos, sin)
    kr = _rotate(xr_ref[nr:2 * nr, :], cos, sin) * (RET_DK ** -0.5)
    vr = xr_ref[2 * nr:3 * nr, :]
    gr = xr_ref[3 * nr:4 * nr, :]
    qkr = jnp.sum(qr * kr, axis=1, keepdims=True)
    for h in range(nr):
        gamma = math.exp(_ret_log_gamma(h))
        state = sr_ref[h]
        v_row = vr[h:h + 1, :]
        o = qkr[h:h + 1, :] * v_row + _vec_mat(qr[h:h + 1, :], state) * gamma
        sr_out[h] = gamma * state + _row_to_col(kr[h:h + 1, :]) * v_row
        yr_ref[h:h + 1, :] = (_rms_norm(o) * gr_ref[h:h + 1, :] * _silu(gr[h:h + 1, :])).astype(BF16)


def _odd_sample(xh, xr, lb3, cos, sin, gain_h, gain_r, state_h, state_r, layer, layer_j):
    db = xh.shape[0]
    per_b3 = lambda b: (b, 0, 0)
    return pl.pallas_call(
        functools.partial(_odd_sample_kernel, layer=layer),
        grid=(db,),
        in_specs=[
            pl.BlockSpec((None, 4 * HGRN_HEADS, HGRN_DK), per_b3),
            pl.BlockSpec((None, 4 * RET_HEADS, RET_DK), per_b3),
            pl.BlockSpec((DEPTH, HGRN_HEADS, HGRN_DK), lambda b: (0, 0, 0)),
            pl.BlockSpec((1, RET_DK // 2), lambda b: (0, 0)),
            pl.BlockSpec((1, RET_DK // 2), lambda b: (0, 0)),
            pl.BlockSpec((None, HGRN_HEADS, HGRN_DV), lambda b: (layer_j, 0, 0)),
            pl.BlockSpec((None, RET_HEADS, RET_DK), lambda b: (layer_j, 0, 0)),
            pl.BlockSpec((None, None, HGRN_HEADS, HGRN_DK, HGRN_DV), lambda b: (layer_j, b, 0, 0, 0)),
            pl.BlockSpec((None, None, RET_HEADS, RET_DK, RET_DK), lambda b: (layer_j, b, 0, 0, 0)),
        ],
        out_specs=[
            pl.BlockSpec((None, HGRN_HEADS, HGRN_DV), per_b3),
            pl.BlockSpec((None, RET_HEADS, RET_DK), per_b3),
            pl.BlockSpec((None, HGRN_HEADS, HGRN_DK, HGRN_DV), lambda b: (b, 0, 0, 0)),
            pl.BlockSpec((None, RET_HEADS, RET_DK, RET_DK), lambda b: (b, 0, 0, 0)),
        ],
        out_shape=[
            jax.ShapeDtypeStruct((db, HGRN_HEADS, HGRN_DV), BF16),
            jax.ShapeDtypeStruct((db, RET_HEADS, RET_DK), BF16),
            jax.ShapeDtypeStruct((db, HGRN_HEADS, HGRN_DK, HGRN_DV), F32),
            jax.ShapeDtypeStruct((db, RET_HEADS, RET_DK, RET_DK), F32),
        ],
        compiler_params=_cparams("parallel"),
        name="odd_sample",
    )(xh, xr, lb3, cos, sin, gain_h, gain_r, state_h, state_r)


def _even_weight_groups(w):
    w_t = jnp.swapaxes(w, 1, 2)
    gates0 = ZA_WIDTH
    qb0 = gates0 + 2 * MLSTM_HEADS
    ki0 = qb0 + ZB_WIDTH
    end = ki0 + IDX_DIM + IDX_HEADS
    pad = jnp.zeros((w.shape[0], LANES - (end - ki0) - (qb0 - gates0), w.shape[1]), w.dtype)
    w_c = jnp.concatenate([w_t[:, ki0:end], w_t[:, gates0:qb0], pad], axis=1)
    return w_t, w_t[:, qb0:ki0], w_c


def _rot_tables(pos):
    inv = 1.0 / (RET_THETA ** jnp.linspace(0.0, 1.0, RET_DK // 2, dtype=F32))
    ang = pos.astype(F32)[:, None] * inv[None, :]
    return jnp.cos(ang), jnp.sin(ang)


def kernel(x_prompt, x_sample, state_mlstm_C, state_mlstm_n, state_mlstm_m, cache_k, cache_v, cache_idx_k,
           state_hgrn, state_ret, page_table, p_prompt, p_sample, ln_g, ln_b, w_ffn_up, w_ffn_down,
           w_in_even, b_gate_mlstm, g_mlstm, w_in_odd, hgrn_lb, g_hgrn, g_ret, w_out, w_pe_gate, w_pe_proj):
    batch, seq, _ = x_prompt.shape
    db, dec_seq, _ = x_sample.shape
    assert dec_seq == 1 and all(seq % c == 0 for c in (MLSTM_CHUNK, RET_CHUNK, HGRN_CHUNK))
    n_even = w_in_even.shape[0]
    n_odd = w_in_odd.shape[0]
    n_pages = page_table.shape[1]
    past = n_pages * PAGE_SIZE
    mp = batch * seq

    w_up = w_ffn_up.astype(BF16)
    w_down = w_ffn_down.astype(BF16)
    w_even_a, w_even_b, w_even_c = _even_weight_groups(w_in_even)
    w_o = w_out.astype(BF16)
    w_pg = w_pe_gate.astype(BF16)
    w_pp = w_pe_proj.astype(BF16)
    ln_g3 = ln_g.reshape(DEPTH * 3, 1, D_MODEL)
    ln_b3 = ln_b.reshape(DEPTH * 3, 1, D_MODEL)
    pp = p_prompt.reshape(DEPTH, mp, P_DIM)
    ps = p_sample.reshape(DEPTH, db, P_DIM)
    cos_p, sin_p = _rot_tables(jnp.arange(seq))
    cos_s, sin_s = _rot_tables(past + jnp.arange(dec_seq))
    lb3 = hgrn_lb.reshape(DEPTH, 1, HGRN_HEADS * HGRN_DK)
    lb_heads = hgrn_lb.reshape(DEPTH, HGRN_HEADS, HGRN_DK)
    bias_row = jnp.zeros((n_even, 1, LANES), F32).at[:, 0, MISC_IG:MISC_IG + 2 * MLSTM_HEADS].set(b_gate_mlstm)
    bias_col = b_gate_mlstm.reshape(n_even, 2 * MLSTM_HEADS, 1)
    gain_m3 = g_mlstm.reshape(n_even, 1, HALF)
    gain_h3 = g_hgrn.reshape(n_odd, 1, HALF)
    gain_r3 = g_ret.reshape(n_odd, 1, HALF)
    cache_k3 = cache_k.reshape(cache_k.shape[:2] + (PAGE_SIZE * ATT_KV_HEADS, ATT_DH))
    cache_v3 = cache_v.reshape(cache_v.shape[:2] + (PAGE_SIZE * ATT_KV_HEADS, ATT_DH))
    cache_ik_t = jnp.swapaxes(cache_idx_k, 2, 3)
    m_state = state_mlstm_m.reshape(n_even, db, MLSTM_HEADS, 1)

    xp = x_prompt.reshape(mp, D_MODEL)
    xs = x_sample.reshape(db, D_MODEL)
    out_even = {k: [] for k in ("C_p", "C_s", "n_p", "n_s", "m_p", "m_s", "k_p", "k_s", "v_p", "v_s", "ik_p", "ik_s")}
    out_odd = {k: [] for k in ("h_p", "h_s", "r_p", "r_s")}

    for layer in range(DEPTH):
        j = layer // 2
        xp, xp_b = _ffn(xp, w_up, w_down, ln_g3, ln_b3, layer, 0, 0)
        xs, xs_b = _ffn(xs, w_up, w_down, ln_g3, ln_b3, layer, 0, 0)
        if layer % 2 == 0:
            za_p, za_s = (_proj_t(xb, w_even_a, j, ZA_WIDTH) for xb in (xp_b, xs_b))
            zb_p, zb_s = (_proj_t(xb, w_even_b, j, ZB_WIDTH) for xb in (xp_b, xs_b))
            zc_p, zc_s = (_proj_t(xb, w_even_c, j, LANES) for xb in (xp_b, xs_b))
            ya_p, c_p, n_p, m_p = _mlstm_prompt(za_p, zc_p, bias_row[j], gain_m3, batch, seq, j)
            yb_p = _dsa_prompt(zb_p, zc_p, batch, seq)

            x4 = za_s.reshape(db, 4 * MLSTM_HEADS, MLSTM_DH)
            gates_s = zc_s[:, MISC_IG:MISC_IG + 2 * MLSTM_HEADS].reshape(db, 2 * MLSTM_HEADS, 1)
            ya_s, c_s, n_s, m_s = _mlstm_sample(x4, gates_s, bias_col[j], g_mlstm.reshape(n_even, MLSTM_HEADS, MLSTM_DH),
                                                state_mlstm_C, state_mlstm_n, m_state, j)
            q_s = zb_s[:, ZB_Q:ZB_K].reshape(db, ATT_HEADS, ATT_DH)
            k_s = zb_s[:, ZB_K:ZB_V].reshape(db, ATT_KV_HEADS, ATT_DH)
            v_s = zb_s[:, ZB_V:ZB_QI].reshape(db, ATT_KV_HEADS, ATT_DH)
            qi_s = zb_s[:, ZB_QI:].reshape(db, IDX_HEADS, IDX_DIM)
            ki_s = zc_s[:, MISC_KI:MISC_KI + IDX_DIM].reshape(db, 1, IDX_DIM)
            wi_s = zc_s[:, MISC_WI:MISC_WI + IDX_HEADS].reshape(db, IDX_HEADS, 1)
            scores = _dsa_sample_scores(page_table, cache_ik_t, qi_s, wi_s, j)
            select = _dsa_sample_select(scores, qi_s, wi_s, ki_s)
            yb_s = _dsa_sample_attn(page_table, scores, select, q_s, k_s, v_s, cache_k3, cache_v3, j)
            ya_s = ya_s.reshape(db, HALF)
            yb_s = yb_s.reshape(db, HALF)

            out_even["C_p"].append(c_p)
            out_even["C_s"].append(c_s)
            out_even["n_p"].append(n_p.reshape(batch, MLSTM_HEADS, MLSTM_DH))
            out_even["n_s"].append(n_s)
            out_even["m_p"].append(m_p[:, :, 0, 0])
            out_even["m_s"].append(m_s[:, :, 0])
            out_even["k_p"].append(zb_p[:, ZB_K:ZB_V].reshape(batch, seq, ATT_KV_HEADS, ATT_DH))
            out_even["k_s"].append(k_s.reshape(db, dec_seq, ATT_KV_HEADS, ATT_DH))
            out_even["v_p"].append(zb_p[:, ZB_V:ZB_QI].reshape(batch, seq, ATT_KV_HEADS, ATT_DH))
            out_even["v_s"].append(v_s.reshape(db, dec_seq, ATT_KV_HEADS, ATT_DH))
            out_even["ik_p"].append(zc_p[:, MISC_KI:MISC_KI + IDX_DIM].reshape(batch, seq, IDX_DIM))
            out_even["ik_s"].append(ki_s.reshape(db, dec_seq, IDX_DIM))
        else:
            zp = _proj(xp_b, w_in_odd, j)
            zs = _proj(xs_b, w_in_odd, j)
            ya_p, h_p = _hgrn_prompt(zp, lb3, gain_h3, batch, seq, layer, j)
            yb_p, r_p = _ret_prompt(zp, cos_p, sin_p, gain_r3, batch, seq, j)
            xh = zs[:, :4 * HALF].reshape(db, 4 * HGRN_HEADS, HGRN_DK)
            xr = zs[:, 4 * HALF:].reshape(db, 4 * RET_HEADS, RET_DK)
            ya_s, yb_s, h_s, r_s = _odd_sample(
                xh, xr, lb_heads, cos_s, sin_s, g_hgrn.reshape(n_odd, HGRN_HEADS, HGRN_DV),
                g_ret.reshape(n_odd, RET_HEADS, RET_DK), state_hgrn, state_ret, layer, j)
            ya_s = ya_s.reshape(db, HALF)
            yb_s = yb_s.reshape(db, HALF)
            out_odd["h_p"].append(h_p)
            out_odd["h_s"].append(h_s)
            out_odd["r_p"].append(r_p)
            out_odd["r_s"].append(r_s)

        xp = _outproj(xp, ya_p, yb_p, w_o, ln_g3, ln_b3, layer)
        xs = _outproj(xs, ya_s, yb_s, w_o, ln_g3, ln_b3, layer)
        xp, xp_b = _ffn(xp, w_up, w_down, ln_g3, ln_b3, layer, 1, 2)
        xs, xs_b = _ffn(xs, w_up, w_down, ln_g3, ln_b3, layer, 1, 2)
        xp = _pe(xp, xp_b, pp, w_pg, w_pp, layer)
        xs = _pe(xs, xs_b, ps, w_pg, w_pp, layer)

    def stk(name, table, like):
        return jnp.stack(table[name]).astype(like.dtype)

    return (
        xp.reshape(batch, seq, D_MODEL), xs.reshape(db, dec_seq, D_MODEL),
        stk("C_p", out_even, state_mlstm_C), stk("C_s", out_even, state_mlstm_C),
        stk("n_p", out_even, state_mlstm_n), stk("n_s", out_even, state_mlstm_n),
        stk("m_p", out_even, state_mlstm_m), stk("m_s", out_even, state_mlstm_m),
        stk("k_p", out_even, cache_k), stk("k_s", out_even, cache_k),
        stk("v_p", out_even, cache_v), stk("v_s", out_even, cache_v),
        stk("ik_p", out_even, cache_idx_k), stk("ik_s", out_even, cache_idx_k),
        stk("h_p", out_odd, state_hgrn), stk("h_s", out_odd, state_hgrn),
        stk("r_p", out_odd, state_ret), stk("r_s", out_odd, state_ret),
    )
```
